```python
import math
import jax
import jax.numpy as jnp
from jax import lax
import numpy as np

D_MODEL = 1024
BATCH = 2
SEQ = 8192
DEPTH = 2

HEAD_DIM = 64
N_MIXERS = 4
HEADS = D_MODEL // (N_MIXERS * HEAD_DIM)
GROUP_WIDTH = HEADS * HEAD_DIM
Q_BLOCK = 128
NEG_INF = -1e30

DIFF_DK = HEAD_DIM // 2
SWA_WINDOW = 128
SWA_KV_HEADS = 2
CMP_BLOCK = 32
CMP_STRIDE = 16
CMP_HIDDEN = 2 * HEAD_DIM
SEL_BLOCK = 64
SEL_TOPK = 16
NSA_WINDOW = 512
FORCE_SCORE = 1e4

N_EXPERTS = 64
N_EXPERT_GROUPS = 8
EXPERTS_PER_GROUP = N_EXPERTS // N_EXPERT_GROUPS
TOP_K = 2
D_EXPERT = D_MODEL // 2
ROW_BLOCK = 128

KV_WIDTH = SWA_KV_HEADS * HEAD_DIM
IN_SPLITS = (
    GROUP_WIDTH, GROUP_WIDTH, GROUP_WIDTH,
    GROUP_WIDTH, KV_WIDTH, KV_WIDTH,
    GROUP_WIDTH, GROUP_WIDTH, GROUP_WIDTH, HEADS,
    GROUP_WIDTH, HEAD_DIM, HEAD_DIM, HEAD_DIM, HEAD_DIM,
    HEAD_DIM, HEAD_DIM, 3 * HEADS,
)
IN_COLS = sum(IN_SPLITS)

kernel_name = 'hybrid_parallel_heads_moe_block'


def rms_norm(x, g, eps=1e-6):
    xf = x.astype(jnp.float32)
    y = xf * lax.rsqrt(jnp.mean(xf * xf, axis=-1, keepdims=True) + eps)
    return (y * g.astype(jnp.float32)).astype(x.dtype)


def split_heads(t, n_heads):
    b, s, _ = t.shape
    return t.reshape(b, s, n_heads, -1).transpose(0, 2, 1, 3)


def split_cols(p, widths):
    offs = np.cumsum((0,) + tuple(widths))
    return [p[..., int(offs[i]):int(offs[i + 1])] for i in range(len(widths))]


def alibi_slopes():
    n = 3 * HEADS
    m = 2.0 ** (-8.0 * np.arange(1, n + 1) / n)
    return jnp.asarray(m.reshape(HEADS, 3).T, dtype=jnp.float32)


def banded_blocks(t, n_prev):
    b, h, s, d = t.shape
    nb = s // Q_BLOCK
    tb = jnp.pad(t.reshape(b, h, nb, Q_BLOCK, d), ((0, 0), (0, 0), (n_prev, 0), (0, 0), (0, 0)))
    return jnp.concatenate([tb[:, :, j:j + nb] for j in range(n_prev + 1)], axis=3)


def band_geometry(nb, n_prev):
    blk = jnp.arange(nb)[:, None]
    qpos = blk * Q_BLOCK + jnp.arange(Q_BLOCK)[None, :]
    kpos = (blk - n_prev) * Q_BLOCK + jnp.arange((n_prev + 1) * Q_BLOCK)[None, :]
    dist = qpos[:, :, None] - kpos[:, None, :]
    return dist, (kpos >= 0)[:, None, :]


def diff_attention(q, k, v, lam_q1, lam_k1, lam_q2, lam_k2, subln_g, slopes, layer_idx):
    f32 = jnp.float32
    b, s, _ = q.shape
    nb = s // Q_BLOCK
    lam_init = 0.8 - 0.6 * math.exp(-0.3 * layer_idx)
    lam = (jnp.exp(jnp.sum(lam_q1.astype(f32) * lam_k1.astype(f32)))
           - jnp.exp(jnp.sum(lam_q2.astype(f32) * lam_k2.astype(f32))) + lam_init)
    q = q.astype(f32).reshape(b, nb, Q_BLOCK, HEADS, 2, DIFF_DK) * DIFF_DK ** -0.5
    qb = q.transpose(1, 0, 3, 4, 2, 5)
    kt = k.astype(f32).reshape(b, s, HEADS, 2, DIFF_DK).transpose(0, 2, 3, 1, 4)
    vt = split_heads(v.astype(f32), HEADS)
    kpos = jnp.arange(s)
    m = slopes[None, :, None, None, None]

    def block(args):
        qi, i = args
        tpos = i * Q_BLOCK + jnp.arange(Q_BLOCK)
        dist = (tpos[:, None] - kpos[None, :]).astype(f32)
        sc = jnp.einsum('bhmqd,bhmkd->bhmqk', qi, kt) - m * dist
        a = jax.nn.softmax(jnp.where(dist >= 0, sc, NEG_INF), axis=-1)
        w = a[:, :, 0] - lam * a[:, :, 1]
        return jnp.einsum('bhqk,bhkd->bhqd', w, vt)

    o = lax.map(block, (qb, jnp.arange(nb)))
    o = rms_norm(o, subln_g) * (1.0 - lam_init)
    return o.transpose(1, 0, 3, 2, 4).reshape(b, s, GROUP_WIDTH)


def swa_sink_attention(q, k, v, sinks, slopes):
    f32 = jnp.float32
    b, s, _ = q.shape
    nb = s // Q_BLOCK
    g = HEADS // SWA_KV_HEADS
    n_prev = SWA_WINDOW // Q_BLOCK
    q = q.astype(f32).reshape(b, nb, Q_BLOCK, SWA_KV_HEADS, g, HEAD_DIM).transpose(0, 3, 4, 1, 2, 5)
    q = q * HEAD_DIM ** -0.5
    kb = banded_blocks(split_heads(k.astype(f32), SWA_KV_HEADS), n_prev)
    vb = banded_blocks(split_heads(v.astype(f32), SWA_KV_HEADS), n_prev)
    dist, kvalid = band_geometry(nb, n_prev)
    valid = kvalid & (dist >= 0) & (dist < SWA_WINDOW)
    m = slopes.reshape(SWA_KV_HEADS, g)[None, :, :, None, None, None]
    sc = jnp.einsum('bcgnqd,bcnkd->bcgnqk', q, kb) - m * dist.astype(f32)
    sc = jnp.where(valid, sc, NEG_INF)
    sink = jnp.broadcast_to(sinks.astype(f32).reshape(SWA_KV_HEADS, g)[None, :, :, None, None, None],
                            sc.shape[:-1] + (1,))
    p = jax.nn.softmax(jnp.concatenate([sc, sink], axis=-1), axis=-1)[..., :-1]
    o = jnp.einsum('bcgnqk,bcnkd->bcgnqd', p, vb)
    return o.transpose(0, 3, 4, 1, 2, 5).reshape(b, s, GROUP_WIDTH)


def forgetting_attention(q, k, v, f_logit, b_f):
    f32 = jnp.float32
    b, s, _ = q.shape
    nb = s // Q_BLOCK
    q = split_heads(q.astype(f32), HEADS) * HEAD_DIM ** -0.5
    k = split_heads(k.astype(f32), HEADS)
    v = split_heads(v.astype(f32), HEADS)
    cum = jnp.cumsum(jax.nn.log_sigmoid(f_logit.astype(f32) + b_f.astype(f32)), axis=1)
    cum = cum.transpose(0, 2, 1)
    qb = q.reshape(b, HEADS, nb, Q_BLOCK, HEAD_DIM).transpose(2, 0, 1, 3, 4)
    cb = cum.reshape(b, HEADS, nb, Q_BLOCK).transpose(2, 0, 1, 3)
    kpos = jnp.arange(s)

    def block(args):
        qi, ci, i = args
        tpos = i * Q_BLOCK + jnp.arange(Q_BLOCK)
        sc = jnp.einsum('bhqd,bhkd->bhqk', qi, k) + ci[..., None] - cum[:, :, None, :]
        sc = jnp.where(tpos[:, None] >= kpos[None, :], sc, NEG_INF)
        return jnp.einsum('bhqk,bhkd->bhqd', jax.nn.softmax(sc, axis=-1), v)

    o = lax.map(block, (qb, cb, jnp.arange(nb)))
    return o.transpose(1, 0, 3, 2, 4).reshape(b, s, GROUP_WIDTH)


def compress(t, pos, w1, b1, w2):
    f32 = jnp.float32
    b, s, d = t.shape
    n_cmp = (s - CMP_BLOCK) // CMP_STRIDE + 1
    idx = np.arange(n_cmp)[:, None] * CMP_STRIDE + np.arange(CMP_BLOCK)[None, :]
    blocks = (t[:, idx] + pos.astype(f32)).reshape(b, n_cmp, CMP_BLOCK * d)
    hid = jax.nn.gelu(blocks @ w1.astype(f32) + b1.astype(f32))
    return hid @ w2.astype(f32)


def nsa_attention(q, k_cmp, v_cmp, k_sel, v_sel, k_win, v_win, gate_logit,
                  cmp_pos, cmp_w1, cmp_b1, cmp_w2, slopes):
    f32 = jnp.float32
    b, s, _ = q.shape
    nb = s // Q_BLOCK
    q = split_heads(q.astype(f32), HEADS) * HEAD_DIM ** -0.5
    qb = q.reshape(b, HEADS, nb, Q_BLOCK, HEAD_DIM).transpose(2, 0, 1, 3, 4)
    kc = compress(k_cmp.astype(f32), cmp_pos[0], cmp_w1[0], cmp_b1[0], cmp_w2[0])
    vc = compress(v_cmp.astype(f32), cmp_pos[1], cmp_w1[1], cmp_b1[1], cmp_w2[1])
    n_cmp = kc.shape[1]
    cmp_start = jnp.arange(n_cmp) * CMP_STRIDE
    cmp_end = cmp_start + CMP_BLOCK - 1
    n_sel = s // SEL_BLOCK
    n_top = min(SEL_TOPK, n_sel)
    sel_start = jnp.arange(n_sel) * SEL_BLOCK
    cover = jnp.clip(jnp.minimum(cmp_start[:, None] + CMP_BLOCK, sel_start[None, :] + SEL_BLOCK)
                     - jnp.maximum(cmp_start[:, None], sel_start[None, :]), 0).astype(f32) / CMP_BLOCK
    ks_blocks = k_sel.astype(f32).reshape(b, n_sel, SEL_BLOCK, HEAD_DIM)
    vs_blocks = v_sel.astype(f32).reshape(b, n_sel, SEL_BLOCK, HEAD_DIM)
    kw_pad = jnp.pad(k_win.astype(f32), ((0, 0), (NSA_WINDOW, 0), (0, 0)))
    vw_pad = jnp.pad(v_win.astype(f32), ((0, 0), (NSA_WINDOW, 0), (0, 0)))
    n_win_keys = NSA_WINDOW + Q_BLOCK
    m = slopes[None, :, None, None]
    gather_blocks = jax.vmap(lambda kb, ib: kb[ib])

    def block(args):
        qi, i = args
        t0 = i * Q_BLOCK
        tpos = t0 + jnp.arange(Q_BLOCK)
        dc = (tpos[:, None] - cmp_end[None, :]).astype(f32)
        cvalid = dc >= 0
        sc = jnp.einsum('bhqd,bnd->bhqn', qi, kc) - m * dc
        pc = jax.nn.softmax(jnp.where(cvalid, sc, NEG_INF), axis=-1) * cvalid
        o_cmp = jnp.einsum('bhqn,bnd->bhqd', pc, vc)
        imp = jnp.einsum('bhqn,nj->bqj', pc, cover)
        cur = (tpos // SEL_BLOCK)[:, None]
        blk = jnp.arange(n_sel)[None, :]
        forced = (blk == 0) | (blk == cur) | (blk == cur - 1)
        score = jnp.where(forced, FORCE_SCORE, jnp.where(blk <= cur, imp, -1.0))
        _, idx = lax.top_k(score, n_top)
        kg = gather_blocks(ks_blocks, idx).reshape(b, Q_BLOCK, n_top * SEL_BLOCK, HEAD_DIM)
        vg = gather_blocks(vs_blocks, idx).reshape(b, Q_BLOCK, n_top * SEL_BLOCK, HEAD_DIM)
        kp = (idx[..., None] * SEL_BLOCK + jnp.arange(SEL_BLOCK)).reshape(b, Q_BLOCK, -1)
        ds = (tpos[None, :, None] - kp).astype(f32)[:, None]
        ss = jnp.einsum('bhqd,bqkd->bhqk', qi, kg) - m * ds
        o_sel = jnp.einsum('bhqk,bqkd->bhqd',
                           jax.nn.softmax(jnp.where(ds >= 0, ss, NEG_INF), axis=-1), vg)
        kw = lax.dynamic_slice_in_dim(kw_pad, t0, n_win_keys, axis=1)
        vw = lax.dynamic_slice_in_dim(vw_pad, t0, n_win_keys, axis=1)
        wpos = t0 - NSA_WINDOW + jnp.arange(n_win_keys)
        dw = tpos[:, None] - wpos[None, :]
        wvalid = (wpos[None, :] >= 0) & (dw >= 0) & (dw < NSA_WINDOW)
        sw = jnp.einsum('bhqd,bkd->bhqk', qi, kw) - m * dw.astype(f32)
        o_win = jnp.einsum('bhqk,bkd->bhqd',
                           jax.nn.softmax(jnp.where(wvalid, sw, NEG_INF), axis=-1), vw)
        return jnp.stack([o_cmp, o_sel, o_win], axis=-1)

    o = lax.map(block, (qb, jnp.arange(nb)))
    g = jax.nn.sigmoid(gate_logit.astype(f32)).reshape(b, nb, Q_BLOCK, HEADS, 3)
    o = jnp.einsum('nbhqdr,bnqhr->bnqhd', o, g)
    return o.reshape(b, s, GROUP_WIDTH)


def moe_ffn(h, router_w, router_b, w_gate, w_up, w_down):
    f32 = jnp.float32
    b, s, d = h.shape
    t = b * s
    hf = h.reshape(t, d)
    aff = jax.nn.sigmoid(hf.astype(f32) @ router_w.astype(f32))
    sel = aff + router_b.astype(f32)
    grp_top = lax.top_k(sel.reshape(t, N_EXPERT_GROUPS, EXPERTS_PER_GROUP), TOP_K)[0]
    grp = jnp.argmax(jnp.sum(grp_top, axis=-1), axis=-1)
    in_grp = (jnp.arange(N_EXPERTS) // EXPERTS_PER_GROUP)[None, :] == grp[:, None]
    _, e_idx = lax.top_k(jnp.where(in_grp, sel, NEG_INF), TOP_K)
    gate = jnp.take_along_axis(aff, e_idx, axis=-1)
    gate = gate / jnp.sum(gate, axis=-1, keepdims=True)
    n_assign = t * TOP_K
    e_flat = e_idx.reshape(n_assign)
    tok = jnp.repeat(jnp.arange(t), TOP_K)
    order = jnp.argsort(e_flat)
    e_sorted = e_flat[order]
    counts = jnp.bincount(e_flat, length=N_EXPERTS)
    padded = (counts + ROW_BLOCK - 1) // ROW_BLOCK * ROW_BLOCK
    pend = jnp.cumsum(padded)
    rank = jnp.arange(n_assign) - (jnp.cumsum(counts) - counts)[e_sorted]
    dest = (pend - padded)[e_sorted] + rank
    n_blocks = (n_assign + N_EXPERTS * (ROW_BLOCK - 1) + ROW_BLOCK - 1) // ROW_BLOCK
    n_rows = n_blocks * ROW_BLOCK
    row_tok = jnp.full((n_rows,), t, jnp.int32).at[dest].set(tok[order])
    row_gate = jnp.zeros((n_rows,), f32).at[dest].set(gate.reshape(n_assign)[order])
    blk_expert = jnp.minimum(jnp.searchsorted(pend, jnp.arange(n_blocks) * ROW_BLOCK, side='right'),
                             N_EXPERTS - 1)
    rows = jnp.concatenate([hf, jnp.zeros((1, d), hf.dtype)], axis=0)[row_tok]
    rows = rows.reshape(n_blocks, ROW_BLOCK, d)

    def expert_rows(args):
        xb, e = args
        return (jax.nn.silu(xb @ w_gate[e]) * (xb @ w_up[e])) @ w_down[e]

    out = lax.map(expert_rows, (rows, blk_expert)).reshape(n_rows, d)
    y = jnp.zeros((t + 1, d), f32).at[row_tok].add(out.astype(f32) * row_gate[:, None])
    return y[:t].reshape(b, s, d).astype(h.dtype)


def setup_inputs(seed: int = 0) -> dict:
    key = jax.random.key(seed)
    ks = jax.random.split(key, 25)
    f32 = jnp.float32

    def nrm(k, shape, scale):
        return scale * jax.random.normal(k, shape, f32)

    d = D_MODEL
    d_mix = N_MIXERS * GROUP_WIDTH
    return {
        'x': nrm(ks[0], (BATCH, SEQ, d), 1.0),
        'c': nrm(ks[1], (BATCH, d), 1.0),
        'ada_w': nrm(ks[2], (DEPTH, d, 6 * d), 0.5 * d ** -0.5),
        'ada_b': nrm(ks[3], (DEPTH, 6 * d), 0.02),
        'norm_attn_g': 1.0 + nrm(ks[4], (DEPTH, d), 0.05),
        'norm_moe_g': 1.0 + nrm(ks[5], (DEPTH, d), 0.05),
        'norm_final_g': 1.0 + nrm(ks[6], (d,), 0.05),
        'w_in': nrm(ks[7], (DEPTH, d, IN_COLS), d ** -0.5),
        'w_out': nrm(ks[8], (DEPTH, d_mix, d), d_mix ** -0.5),
        'diff_lam_q1': nrm(ks[9], (DEPTH, DIFF_DK), 0.1),
        'diff_lam_k1': nrm(ks[10], (DEPTH, DIFF_DK), 0.1),
        'diff_lam_q2': nrm(ks[11], (DEPTH, DIFF_DK), 0.1),
        'diff_lam_k2': nrm(ks[12], (DEPTH, DIFF_DK), 0.1),
        'diff_subln_g': 1.0 + nrm(ks[13], (DEPTH, HEAD_DIM), 0.05),
        'swa_sinks': nrm(ks[14], (DEPTH, HEADS), 0.5),
        'fox_forget_b': 2.0 + nrm(ks[15], (DEPTH, HEADS), 0.1),
        'nsa_cmp_pos': nrm(ks[16], (DEPTH, 2, CMP_BLOCK, HEAD_DIM), 0.5),
        'nsa_cmp_w1': nrm(ks[17], (DEPTH, 2, CMP_BLOCK * HEAD_DIM, CMP_HIDDEN), (CMP_BLOCK * HEAD_DIM) ** -0.5),
        'nsa_cmp_b1': nrm(ks[18], (DEPTH, 2, CMP_HIDDEN), 0.02),
        'nsa_cmp_w2': nrm(ks[19], (DEPTH, 2, CMP_HIDDEN, HEAD_DIM), CMP_HIDDEN ** -0.5),
        'router_w': nrm(ks[20], (d, N_EXPERTS), d ** -0.5),
        'router_b': nrm(ks[21], (N_EXPERTS,), 0.01),
        'exp_w_gate': nrm(ks[22], (DEPTH, N_EXPERTS, d, D_EXPERT), d ** -0.5),
        'exp_w_up': nrm(ks[23], (DEPTH, N_EXPERTS, d, D_EXPERT), d ** -0.5),
        'exp_w_down': nrm(ks[24], (DEPTH, N_EXPERTS, D_EXPERT, d), D_EXPERT ** -0.5),
    }


def reference(x, c, ada_w, ada_b, norm_attn_g, norm_moe_g, norm_final_g, w_in, w_out,
              diff_lam_q1, diff_lam_k1, diff_lam_q2, diff_lam_k2, diff_subln_g,
              swa_sinks, fox_forget_b, nsa_cmp_pos, nsa_cmp_w1, nsa_cmp_b1, nsa_cmp_w2,
              router_w, router_b, exp_w_gate, exp_w_up, exp_w_down):
    slopes = alibi_slopes()
    cond = jax.nn.silu(c)
    for l in range(DEPTH):
        mod = (cond @ ada_w[l] + ada_b[l])[:, None, :]
        sh_a, sc_a, g_a, sh_m, sc_m, g_m = jnp.split(mod, 6, axis=-1)
        h = rms_norm(x, norm_attn_g[l]) * (1 + sc_a) + sh_a
        p = split_cols(h @ w_in[l], IN_SPLITS)
        ya = diff_attention(p[0], p[1], p[2], diff_lam_q1[l], diff_lam_k1[l], diff_lam_q2[l],
                            diff_lam_k2[l], diff_subln_g[l], slopes[0], l)
        yb = swa_sink_attention(p[3], p[4], p[5], swa_sinks[l], slopes[1])
        yc = forgetting_attention(p[6], p[7], p[8], p[9], fox_forget_b[l])
        yd = nsa_attention(p[10], p[11], p[12], p[13], p[14], p[15], p[16], p[17],
                           nsa_cmp_pos[l], nsa_cmp_w1[l], nsa_cmp_b1[l], nsa_cmp_w2[l], slopes[2])
        y = jnp.concatenate([ya, yb, yc, yd], axis=-1).astype(x.dtype) @ w_out[l]
        x = x + g_a * y
        h = rms_norm(x, norm_moe_g[l]) * (1 + sc_m) + sh_m
        x = x + g_m * moe_ffn(h, router_w, router_b, exp_w_gate[l], exp_w_up[l], exp_w_down[l])
    return rms_norm(x, norm_final_g)
```

```python
import functools
import math

import numpy as np
import jax
import jax.numpy as jnp
from jax import lax
from jax.experimental import pallas as pl
from jax.experimental.pallas import tpu as pltpu

F32 = jnp.float32
BF16 = jnp.bfloat16

LANES = 128
HEAD_DIM = 64
N_MIXERS = 4
HEADS = 4
GROUP_WIDTH = HEADS * HEAD_DIM
DIFF_DK = HEAD_DIM // 2
SWA_WINDOW = 128
SWA_KV_HEADS = 2
CMP_BLOCK = 32
CMP_STRIDE = 16
CMP_HIDDEN = 2 * HEAD_DIM
SEL_BLOCK = 64
SEL_TOPK = 16
NSA_WINDOW = 512
FORCE_SCORE = 1e4
NEG_INF = -1e30
MASK_BIG = 2.0 ** 100
N_EXPERTS = 64
EXPERTS_PER_GROUP = 8
EPS = 1e-6
VMEM_LIMIT = 56 * 1024 * 1024

IN_SPLITS = (
    GROUP_WIDTH, GROUP_WIDTH, GROUP_WIDTH,
    GROUP_WIDTH, SWA_KV_HEADS * HEAD_DIM, SWA_KV_HEADS * HEAD_DIM,
    GROUP_WIDTH, GROUP_WIDTH, GROUP_WIDTH, HEADS,
    GROUP_WIDTH, HEAD_DIM, HEAD_DIM, HEAD_DIM, HEAD_DIM,
    HEAD_DIM, HEAD_DIM, 3 * HEADS,
)

_NT = (((1,), (1,)), ((), ()))


def _alibi_slopes():
    n = 3 * HEADS
    m = 2.0 ** (-8.0 * np.arange(1, n + 1) / n)
    return m.reshape(HEADS, 3).T


SLOPES = _alibi_slopes()


def _dot(a, b):
    return jnp.dot(a, b, preferred_element_type=F32)


def _dot_nt(a, b):
    return lax.dot_general(a, b, _NT, preferred_element_type=F32)


def _split_bf16(a):
    hi = a.astype(BF16)
    lo = (a - hi.astype(F32)).astype(BF16)
    return hi, lo


def _dot_f32(a, b, nt=False):
    d = _dot_nt if nt else _dot
    ah, al = _split_bf16(a)
    bh, bl = _split_bf16(b)
    return d(ah, bh) + (d(ah, bl) + d(al, bh))


def _cparams(sem):
    return pltpu.CompilerParams(dimension_semantics=sem, vmem_limit_bytes=VMEM_LIMIT)


def _mod_kernel(c_ref, w_ref, b_ref, o_ref):
    c = c_ref[...]
    cond = c * jax.nn.sigmoid(c)
    o_ref[0] = _dot_f32(cond, w_ref[0]) + b_ref[0]


def _modulation(c, ada_w, ada_b):
    depth, d, n = ada_w.shape
    b = c.shape[0]
    rows = 8
    cp = jnp.zeros((rows, d), F32).at[:b].set(c)
    tn = 1536
    out = pl.pallas_call(
        _mod_kernel,
        grid=(depth, n // tn),
        in_specs=[
            pl.BlockSpec((rows, d), lambda l, j: (0, 0)),
            pl.BlockSpec((1, d, tn), lambda l, j: (l, 0, j)),
            pl.BlockSpec((1, 1, tn), lambda l, j: (l, 0, j)),
        ],
        out_specs=pl.BlockSpec((1, rows, tn), lambda l, j: (l, 0, j)),
        out_shape=jax.ShapeDtypeStruct((depth, rows, n), F32),
        compiler_params=_cparams(("arbitrary", "arbitrary")),
        name="adaln_mod",
    )(cp, ada_w, ada_b.reshape(depth, 1, n))
    return out[:, :b].reshape(depth, b, 6, d)


W_A, W_B, W_C, W_D, W_G, W_CMP = 768, 768, 768, 768, 768, 128
W_OFFS = np.cumsum((0, W_A, W_B, W_C, W_D, W_G, W_CMP))
W_TOTAL = int(W_OFFS[-1])


def _prep_w_in(w):
    offs = np.cumsum((0,) + IN_SPLITS)
    col = lambda i: w[:, int(offs[i]):int(offs[i + 1])]
    dup = lambda t: jnp.concatenate([t, t], axis=1)
    kb, vb = col(4), col(5)
    gates = col(17)
    gexp = [jnp.repeat(gates[:, h * 3 + r:h * 3 + r + 1], HEAD_DIM, axis=1)
            for r in range(3) for h in range(HEADS)]
    parts = [
        col(0) * DIFF_DK ** -0.5, col(1), col(2),
        col(3) * HEAD_DIM ** -0.5,
        dup(kb[:, :HEAD_DIM]), dup(kb[:, HEAD_DIM:]),
        dup(vb[:, :HEAD_DIM]), dup(vb[:, HEAD_DIM:]),
        col(6) * HEAD_DIM ** -0.5, col(7), col(8),
        col(10) * HEAD_DIM ** -0.5, dup(col(13)), dup(col(14)), dup(col(15)), dup(col(16)),
    ] + gexp + [col(11), col(12)]
    wp = jnp.concatenate(parts, axis=1).astype(BF16)
    wf = jnp.zeros((8, w.shape[0]), F32).at[:HEADS].set(col(9).T).astype(BF16)
    return wp, wf


def _lane_cumsum(y):
    n = y.shape[1]
    lane = lax.broadcasted_iota(jnp.int32, y.shape, 1)
    sh = 1
    while sh < n:
        y = y + jnp.where(lane >= sh, pltpu.roll(y, sh, 1), 0.0)
        sh *= 2
    return y


def _inproj_kernel(x_ref, mod_ref, g_ref, w_ref, wf_ref, bf_ref,
                   oa, ob, oc, od, og, ocmp, ofc, carry):
    i = pl.program_id(1)
    x = x_ref[0]
    ms = jnp.mean(x * x, axis=-1, keepdims=True)
    y = x * lax.rsqrt(ms + EPS) * g_ref[...]
    h = y * (1.0 + mod_ref[1:2, :]) + mod_ref[0:1, :]
    hb = h.astype(BF16)
    o = W_OFFS
    oa[0] = _dot(hb, w_ref[:, o[0]:o[1]]).astype(BF16)
    ob[0] = _dot(hb, w_ref[:, o[1]:o[2]]).astype(BF16)
    oc[0] = _dot(hb, w_ref[:, o[2]:o[3]]).astype(BF16)
    od[0] = _dot(hb, w_ref[:, o[3]:o[4]]).astype(BF16)
    og[0] = jax.nn.sigmoid(_dot(hb, w_ref[:, o[4]:o[5]])).astype(BF16)
    ocmp[0] = _dot(hb, w_ref[:, o[5]:o[6]])
    fl = _dot_nt(wf_ref[...], hb) + bf_ref[:, 0:1]
    ls = jnp.minimum(fl, 0.0) - jnp.log1p(jnp.exp(-jnp.abs(fl)))

    @pl.when(i == 0)
    def _():
        carry[...] = jnp.zeros_like(carry)

    cs = _lane_cumsum(ls) + carry[:, 0:1]
    ofc[0] = cs
    carry[...] = jnp.broadcast_to(cs[:, cs.shape[1] - 1:], carry.shape)


def _in_projection(x, mod_l, g, wp, wf, bf, tm):
    b, s, d = x.shape
    bf8 = jnp.zeros((8, LANES), F32).at[:HEADS, :].set(bf[:, None])
    outs = pl.pallas_call(
        _inproj_kernel,
        grid=(b, s // tm),
        in_specs=[
            pl.BlockSpec((1, tm, d), lambda bi, i: (bi, i, 0)),
            pl.BlockSpec((None, 6, d), lambda bi, i: (bi, 0, 0)),
            pl.BlockSpec((1, d), lambda bi, i: (0, 0)),
            pl.BlockSpec((d, W_TOTAL), lambda bi, i: (0, 0)),
            pl.BlockSpec((8, d), lambda bi, i: (0, 0)),
            pl.BlockSpec((8, LANES), lambda bi, i: (0, 0)),
        ],
        out_specs=[
            pl.BlockSpec((1, tm, W_A), lambda bi, i: (bi, i, 0)),
            pl.BlockSpec((1, tm, W_B), lambda bi, i: (bi, i, 0)),
            pl.BlockSpec((1, tm, W_C), lambda bi, i: (bi, i, 0)),
            pl.BlockSpec((1, tm, W_D), lambda bi, i: (bi, i, 0)),
            pl.BlockSpec((1, tm, W_G), lambda bi, i: (bi, i, 0)),
            pl.BlockSpec((1, tm, W_CMP), lambda bi, i: (bi, i, 0)),
            pl.BlockSpec((1, 8, tm), lambda bi, i: (bi, 0, i)),
        ],
        out_shape=[
            jax.ShapeDtypeStruct((b, s, W_A), BF16),
            jax.ShapeDtypeStruct((b, s, W_B), BF16),
            jax.ShapeDtypeStruct((b, s, W_C), BF16),
            jax.ShapeDtypeStruct((b, s, W_D), BF16),
            jax.ShapeDtypeStruct((b, s, W_G), BF16),
            jax.ShapeDtypeStruct((b, s, W_CMP), F32),
            jax.ShapeDtypeStruct((b, 8, s), F32),
        ],
        scratch_shapes=[pltpu.VMEM((8, LANES), F32)],
        compiler_params=_cparams(("arbitrary", "arbitrary")),
        name="in_projection",
    )(x, mod_l, g.reshape(1, d), wp, wf, bf8)
    return outs


def _flash_kernel(kind, t, layer_idx, *refs):
    it = iter(refs)
    q_ref, k_ref, v_ref = next(it), next(it), next(it)
    fc_ref = next(it) if kind == "C" else None
    un_ref, en_ref = (next(it), next(it)) if kind == "Dsel" else (None, None)
    sink_ref = next(it) if kind == "B" else None
    lam_ref, sg_ref = (next(it), next(it)) if kind == "A" else (None, None)
    o_ref, m_scr, l_scr, acc_scr = next(it), next(it), next(it), next(it)

    p = pl.program_id(1)
    i = pl.program_id(2)
    nstream = 4 if kind == "A" else 2
    width = LANES // nstream
    window = {"B": SWA_WINDOW, "Dwin": NSA_WINDOW}.get(kind)
    mixer = {"A": 0, "B": 1, "Dsel": 2, "Dwin": 2}.get(kind)

    q2 = q_ref[0]
    lane = lax.broadcasted_iota(jnp.int32, (t, LANES), 1)
    zero = jnp.zeros_like(q2)
    qms = [jnp.where((lane >= s * width) & (lane < (s + 1) * width), q2, zero)
           for s in range(nstream)]

    def head_of(s):
        return s // 2 if kind == "A" else s

    slopes = None
    if mixer is not None:
        sl = SLOPES[mixer]
        slopes = [jnp.where(p == 0, float(sl[head_of(s)]), float(sl[2 + head_of(s)]))
                  for s in range(nstream)]

    for s in range(nstream):
        if kind == "B":
            qrow = lax.broadcasted_iota(jnp.int32, (t, LANES), 0).astype(F32)
            m_scr[s] = sink_ref[2 * p + s] + slopes[s] * qrow
            l_scr[s] = jnp.ones((t, LANES), F32)
        else:
            m_scr[s] = jnp.full((t, LANES), NEG_INF, F32)
            l_scr[s] = jnp.zeros((t, LANES), F32)
        acc_scr[s] = jnp.zeros((t, LANES), F32)

    row = lax.broadcasted_iota(jnp.int32, (t, t), 0)
    colm = lax.broadcasted_iota(jnp.int32, (t, t), 1)

    def tile(j, diag):
        start = pl.multiple_of(j * t, t)
        k2 = k_ref[0, pl.ds(start, t), :]
        v2 = v_ref[0, pl.ds(start, t), :]
        col = lax.broadcasted_iota(jnp.int32, (1, t), 1)
        rel = ((j - i) * t + col).astype(F32)
        mb = None
        if kind == "Dsel":
            mb = _dot(un_ref[0], en_ref[:, pl.ds(start, t)])
        if diag:
            mask = row >= colm
        elif window is not None:
            mask = (row - colm) < (window - (i - j) * t)
        else:
            mask = None
        for s in range(nstream):
            sc = _dot_nt(qms[s], k2)
            if slopes is not None:
                sc = sc + slopes[s] * rel
            if kind == "C":
                sc = sc - fc_ref[0, pl.ds(2 * p + s, 1), pl.ds(start, t)]
            if mb is not None:
                sc = sc + mb
            if mask is not None:
                sc = jnp.where(mask, sc, NEG_INF)
            m_prev = m_scr[s]
            m_next = jnp.maximum(m_prev, jnp.max(sc, axis=1, keepdims=True))
            alpha = jnp.exp(m_prev - m_next)
            pexp = jnp.exp(sc - jnp.tile(m_next, (1, t // LANES)))
            l_scr[s] = alpha * l_scr[s] + jnp.sum(pexp, axis=1, keepdims=True)
            acc_scr[s] = alpha * acc_scr[s] + _dot(pexp.astype(BF16), v2)
            m_scr[s] = m_next

    tile(i, True)
    if window is not None:
        nprev = -(-(window - 1) // t)
        lo = jnp.maximum(i - nprev, 0)
    else:
        lo = 0

    def body(j, carry):
        tile(j, False)
        return carry

    lax.fori_loop(lo, i, body, 0)

    outs = [acc_scr[s] / l_scr[s] for s in range(nstream)]
    if kind == "A":
        lam_init = 0.8 - 0.6 * math.exp(-0.3 * layer_idx)
        t1 = jnp.sum(lam_ref[0:1, :] * lam_ref[1:2, :], axis=1, keepdims=True)
        t2 = jnp.sum(lam_ref[2:3, :] * lam_ref[3:4, :], axis=1, keepdims=True)
        lam = jnp.exp(t1) - jnp.exp(t2) + lam_init
        d0 = outs[0] - lam * outs[1]
        d1 = outs[2] - lam * outs[3]
        lo_half = lane < HEAD_DIM
        o = jnp.where(lo_half, d0, d1)
        sq = o * o
        ss_lo = jnp.sum(jnp.where(lo_half, sq, 0.0), axis=1, keepdims=True)
        ss_hi = jnp.sum(jnp.where(lo_half, 0.0, sq), axis=1, keepdims=True)
        ms = jnp.where(lo_half, ss_lo, ss_hi) * (1.0 / HEAD_DIM)
        o = o * lax.rsqrt(ms + EPS) * sg_ref[...] * (1.0 - lam_init)
    else:
        o = jnp.where(lane < HEAD_DIM, outs[0], outs[1])
    o_ref[0] = o.astype(o_ref.dtype)


def _flash(kind, t, layer_idx, q_arr, qoff, k_arr, koff, v_arr, voff, extras, shared_kv):
    b, s, _ = q_arr.shape
    nstream = 4 if kind == "A" else 2
    kv_idx = (lambda off: (lambda bi, p, i: (bi, 0, off))) if shared_kv else \
             (lambda off: (lambda bi, p, i: (bi, 0, off + p)))
    in_specs = [
        pl.BlockSpec((1, t, LANES), lambda bi, p, i: (bi, i, qoff + p)),
        pl.BlockSpec((1, s, LANES), kv_idx(koff)),
        pl.BlockSpec((1, s, LANES), kv_idx(voff)),
    ]
    args = [q_arr, k_arr, v_arr]
    if kind == "C":
        fcum, = extras
        in_specs.append(pl.BlockSpec((1, 8, s), lambda bi, p, i: (bi, 0, 0)))
        args.append(fcum)
    elif kind == "Dsel":
        unsel, eneg = extras
        nsel = unsel.shape[-1]
        in_specs.append(pl.BlockSpec((1, t, nsel), lambda bi, p, i: (bi, i, 0)))
        in_specs.append(pl.BlockSpec((nsel, s), lambda bi, p, i: (0, 0)))
        args += [unsel, eneg]
    elif kind == "B":
        sinks, = extras
        in_specs.append(pl.BlockSpec(memory_space=pltpu.SMEM))
        args.append(sinks)
    elif kind == "A":
        lam8, sg = extras
        in_specs.append(pl.BlockSpec((8, LANES), lambda bi, p, i: (0, 0)))
        in_specs.append(pl.BlockSpec((1, LANES), lambda bi, p, i: (0, 0)))
        args += [lam8, sg]
    return pl.pallas_call(
        functools.partial(_flash_kernel, kind, t, layer_idx),
        grid=(b, 2, s // t),
        in_specs=in_specs,
        out_specs=pl.BlockSpec((1, t, LANES), lambda bi, p, i: (bi, i, p)),
        out_shape=jax.ShapeDtypeStruct((b, s, 2 * LANES), BF16),
        scratch_shapes=[pltpu.VMEM((nstream, t, LANES), F32)] * 3,
        compiler_params=_cparams(("arbitrary", "arbitrary", "arbitrary")),
        name="flash_" + kind,
    )(*args)


def _gelu_tanh(x):
    return 0.5 * x * (1.0 + jnp.tanh(math.sqrt(2.0 / math.pi) * (x + 0.044715 * x * x * x)))


def _compress_kernel(t_ref, w1_ref, pos_ref, b1_ref, w2_ref, o_ref):
    tr = t_ref[0, 0]
    half = tr.shape[1]
    n = tr.shape[0]
    w1 = w1_ref[0]
    u = _dot_f32(tr, w1[:half])
    v = _dot_f32(tr, w1[half:])
    cpos = _dot_f32(pos_ref[0], w1)[0:1] + b1_ref[0]
    hid = u + pltpu.roll(v, n - 1, 0) + cpos
    o_ref[0, 0] = _dot_f32(_gelu_tanh(hid), w2_ref[0])


def _compress(kv_r, w1, pos8, b1, w2d):
    two, b, n, dd = kv_r.shape
    return pl.pallas_call(
        _compress_kernel,
        grid=(two, b),
        in_specs=[
            pl.BlockSpec((1, 1, n, dd), lambda c, bi: (c, bi, 0, 0)),
            pl.BlockSpec((1, 2 * dd, CMP_HIDDEN), lambda c, bi: (c, 0, 0)),
            pl.BlockSpec((1, 8, 2 * dd), lambda c, bi: (c, 0, 0)),
            pl.BlockSpec((1, 1, CMP_HIDDEN), lambda c, bi: (c, 0, 0)),
            pl.BlockSpec((1, CMP_HIDDEN, LANES), lambda c, bi: (c, 0, 0)),
        ],
        out_specs=pl.BlockSpec((1, 1, n, LANES), lambda c, bi: (c, bi, 0, 0)),
        out_shape=jax.ShapeDtypeStruct((two, b, n, LANES), F32),
        compiler_params=_cparams(("arbitrary", "arbitrary")),
        name="nsa_compress",
    )(kv_r, w1, pos8, b1, w2d)


def _cmp_kernel(t, q_ref, kc_ref, vc_ref, cov_ref, o_ref, un_ref):
    i = pl.program_id(1)
    ncp = kc_ref.shape[2]
    nsel = cov_ref.shape[1]
    kc = kc_ref[0, 0]
    vc = vc_ref[0, 0]
    kch, kcl = _split_bf16(kc)
    vch, vcl = _split_bf16(vc)
    lane = lax.broadcasted_iota(jnp.int32, (t, LANES), 1)
    tq = i * t + lax.broadcasted_iota(jnp.int32, (t, ncp), 0)
    nidx = lax.broadcasted_iota(jnp.int32, (t, ncp), 1)
    dci = tq - (nidx * CMP_STRIDE + CMP_BLOCK - 1)
    valid = (dci >= 0) & (nidx < ncp - 1)
    dc = dci.astype(F32)
    psum = jnp.zeros((t, ncp), F32)
    outs = []
    for pair in range(2):
        q2 = q_ref[0, :, pair * LANES:(pair + 1) * LANES]
        for g in range(2):
            h = 2 * pair + g
            qm = jnp.where((lane >= g * HEAD_DIM) & (lane < (g + 1) * HEAD_DIM), q2, jnp.zeros_like(q2))
            sc = _dot_nt(qm, kch) + _dot_nt(qm, kcl) - float(SLOPES[2][h]) * dc
            sc = jnp.where(valid, sc, NEG_INF)
            m = jnp.max(sc, axis=1, keepdims=True)
            pe = jnp.where(valid, jnp.exp(sc - m), 0.0)
            l = jnp.sum(pe, axis=1, keepdims=True)
            pc = pe * (1.0 / jnp.where(l > 0.0, l, 1.0))
            psum = psum + pc
            ph, plo = _split_bf16(pc)
            outs.append(_dot(ph, vch) + (_dot(ph, vcl) + _dot(plo, vch)))
    lo_half = lane < HEAD_DIM
    o_ref[0, :, 0:LANES] = jnp.where(lo_half, outs[0], outs[1]).astype(o_ref.dtype)
    o_ref[0, :, LANES:2 * LANES] = jnp.where(lo_half, outs[2], outs[3]).astype(o_ref.dtype)
    ph, plo = _split_bf16(psum)
    cov = cov_ref[...]
    imp = _dot(ph, cov) + _dot(plo, cov)
    blk = lax.broadcasted_iota(jnp.int32, (t, nsel), 1)
    cur = (i * t + lax.broadcasted_iota(jnp.int32, (t, nsel), 0)) // SEL_BLOCK
    forced = (blk == 0) | (blk == cur) | (blk == cur - 1)
    score = jnp.where(forced, FORCE_SCORE, jnp.where(blk <= cur, imp, -1.0))
    unsel = jnp.ones((t, nsel), F32)
    for _ in range(min(SEL_TOPK, nsel)):
        mx = jnp.max(score, axis=1, keepdims=True)
        idx = jnp.min(jnp.where(score == mx, blk, nsel), axis=1, keepdims=True)
        hit = blk == idx
        unsel = jnp.where(hit, 0.0, unsel)
        score = jnp.where(hit, -3.0, score)
    un_ref[0] = unsel.astype(un_ref.dtype)


def _cmp_attention(t, qd, kvc, cover):
    b, s, _ = qd.shape
    ncp = kvc.shape[2]
    nsel = cover.shape[1]
    return pl.pallas_call(
        functools.partial(_cmp_kernel, t),
        grid=(b, s // t),
        in_specs=[
            pl.BlockSpec((1, t, 2 * LANES), lambda bi, i: (bi, i, 0)),
            pl.BlockSpec((1, 1, ncp, LANES), lambda bi, i: (0, bi, 0, 0)),
            pl.BlockSpec((1, 1, ncp, LANES), lambda bi, i: (1, bi, 0, 0)),
            pl.BlockSpec((ncp, nsel), lambda bi, i: (0, 0)),
        ],
        out_specs=[
            pl.BlockSpec((1, t, 2 * LANES), lambda bi, i: (bi, i, 0)),
            pl.BlockSpec((1, t, nsel), lambda bi, i: (bi, i, 0)),
        ],
        out_shape=[
            jax.ShapeDtypeStruct((b, s, 2 * LANES), BF16),
            jax.ShapeDtypeStruct((b, s, nsel), BF16),
        ],
        compiler_params=_cparams(("arbitrary", "arbitrary")),
        name="nsa_cmp_select",
    )(qd, kvc, kvc, cover)


def _cover_matrix(s):
    ncp, nsel = s // CMP_STRIDE, s // SEL_BLOCK
    cs = np.arange(ncp)[:, None] * CMP_STRIDE
    ss = np.arange(nsel)[None, :] * SEL_BLOCK
    cov = np.clip(np.minimum(cs + CMP_BLOCK, ss + SEL_BLOCK) - np.maximum(cs, ss), 0, None) / CMP_BLOCK
    cov[ncp - 1] = 0.0
    return jnp.asarray(cov, BF16)


def _expand_neg(s):
    nsel = s // SEL_BLOCK
    e = (np.arange(s)[None, :] // SEL_BLOCK) == np.arange(nsel)[:, None]
    return jnp.asarray(np.where(e, -MASK_BIG, 0.0), BF16)


def _group_reduce(x, lane, op):
    n = x.shape[1]
    sh = 1
    while sh < EXPERTS_PER_GROUP:
        up = pltpu.roll(x, n - sh, 1)
        dn = pltpu.roll(x, sh, 1)
        x = op(x, jnp.where((lane & sh) == 0, up, dn))
        sh *= 2
    return x


def _outproj_kernel(ya, yb, yc, ocmp, osel, owin, gt, x_ref, mod_ref, wo_ref, g_ref,
                    rw_ref, rb_ref, xo_ref, h_ref, ri_ref, rg_ref, cnt_ref, carry):
    first = (pl.program_id(0) == 0) & (pl.program_id(1) == 0)
    gw = GROUP_WIDTH
    g = gt[0].astype(F32)
    yd = (g[:, 0:gw] * ocmp[0].astype(F32) + g[:, gw:2 * gw] * osel[0].astype(F32)
          + g[:, 2 * gw:3 * gw] * owin[0].astype(F32))
    ycat = jnp.concatenate([ya[0], yb[0], yc[0], yd.astype(BF16)], axis=1)
    y = _dot(ycat, wo_ref[...])
    x = x_ref[0] + mod_ref[2:3, :] * y
    xo_ref[0] = x
    ms = jnp.mean(x * x, axis=-1, keepdims=True)
    h = x * lax.rsqrt(ms + EPS) * g_ref[...]
    h = h * (1.0 + mod_ref[4:5, :]) + mod_ref[3:4, :]
    h_ref[0] = h
    tm = h.shape[0]
    logits = _dot_f32(h, rw_ref[...])
    lane = lax.broadcasted_iota(jnp.int32, (tm, LANES), 1)
    valid = lane < N_EXPERTS
    aff = jax.nn.sigmoid(logits)
    sel = jnp.where(valid, aff + rb_ref[...], NEG_INF)
    big = 4 * LANES
    g1 = _group_reduce(sel, lane, jnp.maximum)
    i1 = _group_reduce(jnp.where(sel == g1, lane, big), lane, jnp.minimum)
    sel_b = jnp.where(lane == i1, NEG_INF, sel)
    g2 = _group_reduce(sel_b, lane, jnp.maximum)
    i2 = _group_reduce(jnp.where(sel_b == g2, lane, big), lane, jnp.minimum)
    gs = jnp.where(valid, g1 + g2, NEG_INF)
    gm = jnp.max(gs, axis=1, keepdims=True)
    gl = jnp.min(jnp.where(gs == gm, lane, big), axis=1, keepdims=True)
    ingrp = (lane >= gl) & (lane < gl + EXPERTS_PER_GROUP)
    e1 = jnp.min(jnp.where(ingrp, i1, big), axis=1, keepdims=True)
    e2 = jnp.min(jnp.where(ingrp, i2, big), axis=1, keepdims=True)
    oh1 = lane == e1
    oh2 = lane == e2
    a1 = jnp.sum(jnp.where(oh1, aff, 0.0), axis=1, keepdims=True)
    a2 = jnp.sum(jnp.where(oh2, aff, 0.0), axis=1, keepdims=True)
    inv = 1.0 / (a1 + a2)
    @pl.when(first)
    def _():
        carry[...] = jnp.zeros_like(carry)

    ohs = jnp.where(oh1 | oh2, 1.0, 0.0)
    rr = lax.broadcasted_iota(jnp.int32, (tm, tm), 0)
    cc = lax.broadcasted_iota(jnp.int32, (tm, tm), 1)
    ltri = jnp.where(cc < rr, 1.0, 0.0).astype(BF16)
    before = _dot(ltri, ohs.astype(BF16)) + carry[0:1, :]
    r1 = jnp.sum(jnp.where(oh1, before, 0.0), axis=1, keepdims=True)
    r2 = jnp.sum(jnp.where(oh2, before, 0.0), axis=1, keepdims=True)
    total = carry[0:1, :] + jnp.sum(ohs, axis=0, keepdims=True)
    carry[...] = jnp.broadcast_to(total, carry.shape)
    cnt_ref[...] = jnp.broadcast_to(total, cnt_ref.shape).astype(jnp.int32)
    ri = jnp.where(lane == 0, e1, jnp.where(lane == 1, e2, jnp.where(
        lane == 2, r1.astype(jnp.int32), jnp.where(lane == 3, r2.astype(jnp.int32), 0))))
    rg = jnp.where(lane == 0, a1 * inv, jnp.where(lane == 1, a2 * inv, 0.0))
    ri_ref[0] = ri[:, 0:8]
    rg_ref[0] = rg[:, 0:8]


def _out_projection(ya, yb, yc, ocmp, osel, owin, gates, x, mod_l, wo, g, rw, rb, tm):
    b, s, d = x.shape
    gw = GROUP_WIDTH
    tok = lambda w: pl.BlockSpec((1, tm, w), lambda bi, i: (bi, i, 0))
    full = lambda shp: pl.BlockSpec(shp, lambda bi, i: (0,) * len(shp))
    return pl.pallas_call(
        _outproj_kernel,
        grid=(b, s // tm),
        in_specs=[tok(gw)] * 6 + [tok(3 * gw), tok(d),
                                  pl.BlockSpec((None, 6, d), lambda bi, i: (bi, 0, 0)),
                                  full((d, d)), full((1, d)), full((d, LANES)), full((1, LANES))],
        out_specs=[tok(d), tok(d), tok(8), tok(8), full((8, LANES))],
        out_shape=[
            jax.ShapeDtypeStruct((b, s, d), F32),
            jax.ShapeDtypeStruct((b, s, d), F32),
            jax.ShapeDtypeStruct((b, s, 8), jnp.int32),
            jax.ShapeDtypeStruct((b, s, 8), F32),
            jax.ShapeDtypeStruct((8, LANES), jnp.int32),
        ],
        scratch_shapes=[pltpu.VMEM((8, LANES), F32)],
        compiler_params=_cparams(("arbitrary", "arbitrary")),
        name="out_projection_router",
    )(ya, yb, yc, ocmp, osel, owin, gates, x, mod_l, wo, g.reshape(1, d), rw, rb)


def _dispatch_kernel(pos_ref, h_ref, xs_ref, sem):
    tm = h_ref.shape[0]

    def issue(r, c):
        for k in range(2):
            pltpu.make_async_copy(h_ref.at[pl.ds(r, 1), :],
                                  xs_ref.at[pl.ds(pos_ref[2 * r + k], 1), :], sem).start()
        return c

    lax.fori_loop(0, tm, issue, 0)

    def drain(r, c):
        pltpu.make_async_copy(h_ref.at[pl.ds(0, 1), :], xs_ref.at[pl.ds(0, 1), :], sem).wait()
        return c

    lax.fori_loop(0, 2 * tm, drain, 0)


def _dispatch(h, pos, tm):
    t, d = h.shape
    return pl.pallas_call(
        _dispatch_kernel,
        grid=(t // tm,),
        in_specs=[
            pl.BlockSpec((2 * tm,), lambda i: (i,), memory_space=pltpu.SMEM),
            pl.BlockSpec((tm, d), lambda i: (i, 0)),
        ],
        out_specs=pl.BlockSpec(memory_space=pl.ANY),
        out_shape=jax.ShapeDtypeStruct((2 * t, d), F32),
        scratch_shapes=[pltpu.SemaphoreType.DMA(())],
        compiler_params=_cparams(("arbitrary",)),
        name="moe_dispatch",
    )(pos, h)


def _expert_kernel(vb_ref, ve_ref, vlo_ref, vhi_ref, nv_ref,
                   xs_ref, wg_ref, wu_ref, wd_ref, ys_ref, wgb, wub, wdb):
    v = pl.program_id(0)
    rows = xs_ref.shape[0]
    prev = jnp.maximum(v - 1, 0)
    new_expert = (v == 0) | (ve_ref[v] != ve_ref[prev])
    new_block = (v == 0) | (vb_ref[v] != vb_ref[prev])

    @pl.when(v < nv_ref[0])
    def _():
        @pl.when(new_expert)
        def _():
            wgb[...] = wg_ref[0].astype(BF16)
            wub[...] = wu_ref[0].astype(BF16)
            wdb[...] = wd_ref[0].astype(BF16)

        x = xs_ref[...].astype(BF16)
        hg = _dot(x, wgb[...])
        hu = _dot(x, wub[...])
        hm = (hg * jax.nn.sigmoid(hg) * hu).astype(BF16)
        y = _dot(hm, wdb[...])
        r = vb_ref[v] * rows + lax.broadcasted_iota(jnp.int32, (rows, 1), 0)
        mine = (r >= vlo_ref[v]) & (r < vhi_ref[v])

        @pl.when(new_block)
        def _():
            ys_ref[...] = jnp.where(mine, y, 0.0)

        @pl.when(jnp.logical_not(new_block))
        def _():
            ys_ref[...] = jnp.where(mine, y, ys_ref[...])


def _experts(xs, meta, wg, wu, wd, rows, nvmax):
    n, d = xs.shape
    de = wg.shape[2]
    grid_spec = pltpu.PrefetchScalarGridSpec(
        num_scalar_prefetch=5,
        grid=(nvmax,),
        in_specs=[
            pl.BlockSpec((rows, d), lambda v, vb, ve, lo, hi, nv: (vb[v], 0)),
            pl.BlockSpec((1, d, de), lambda v, vb, ve, lo, hi, nv: (ve[v], 0, 0)),
            pl.BlockSpec((1, d, de), lambda v, vb, ve, lo, hi, nv: (ve[v], 0, 0)),
            pl.BlockSpec((1, de, d), lambda v, vb, ve, lo, hi, nv: (ve[v], 0, 0)),
        ],
        out_specs=pl.BlockSpec((rows, d), lambda v, vb, ve, lo, hi, nv: (vb[v], 0)),
        scratch_shapes=[pltpu.VMEM((d, de), BF16), pltpu.VMEM((d, de), BF16), pltpu.VMEM((de, d), BF16)],
    )
    return pl.pallas_call(
        _expert_kernel,
        grid_spec=grid_spec,
        out_shape=jax.ShapeDtypeStruct((n, d), F32),
        compiler_params=_cparams(("arbitrary",)),
        name="moe_experts",
    )(*meta, xs, wg, wu, wd)


def _combine_kernel(final, pos_ref, x_ref, rg_ref, mod_ref, g_ref, ys_ref, o_ref, buf, sem):
    tm = x_ref.shape[0]

    def issue(r, c):
        for k in range(2):
            pltpu.make_async_copy(ys_ref.at[pl.ds(pos_ref[2 * r + k], 1), :],
                                  buf.at[k, pl.ds(r, 1), :], sem).start()
        return c

    lax.fori_loop(0, tm, issue, 0)

    def drain(r, c):
        pltpu.make_async_copy(ys_ref.at[pl.ds(0, 1), :], buf.at[0, pl.ds(0, 1), :], sem).wait()
        return c

    lax.fori_loop(0, 2 * tm, drain, 0)
    rg = rg_ref[...]
    y = rg[:, 0:1] * buf[0] + rg[:, 1:2] * buf[1]
    x = x_ref[...] + mod_ref[5:6, :] * y
    if final:
        ms = jnp.mean(x * x, axis=-1, keepdims=True)
        x = x * lax.rsqrt(ms + EPS) * g_ref[...]
    o_ref[...] = x


def _combine(final, pos, x, rg, mod_l, g, ys, tm):
    t, d = x.shape
    per_b = t // mod_l.shape[0] // tm
    return pl.pallas_call(
        functools.partial(_combine_kernel, final),
        grid=(t // tm,),
        in_specs=[
            pl.BlockSpec((2 * tm,), lambda i: (i,), memory_space=pltpu.SMEM),
            pl.BlockSpec((tm, d), lambda i: (i, 0)),
            pl.BlockSpec((tm, 8), lambda i: (i, 0)),
            pl.BlockSpec((None, 6, d), lambda i: (i // per_b, 0, 0)),
            pl.BlockSpec((1, d), lambda i: (0, 0)),
            pl.BlockSpec(memory_space=pl.ANY),
        ],
        out_specs=pl.BlockSpec((tm, d), lambda i: (i, 0)),
        out_shape=jax.ShapeDtypeStruct((t, d), F32),
        scratch_shapes=[pltpu.VMEM((2, tm, d), F32), pltpu.SemaphoreType.DMA(())],
        compiler_params=_cparams(("arbitrary",)),
        name="moe_combine",
    )(pos, x, rg, mod_l, g.reshape(1, d), ys)


def _visit_plan(counts, rows, nblocks):
    ne = counts.shape[0]
    nvmax = nblocks + ne - 1
    ends = jnp.cumsum(counts)
    offs = ends - counts
    b_lo = offs // rows
    b_hi = jnp.maximum(ends - 1, 0) // rows
    nvis = jnp.where(counts > 0, b_hi - b_lo + 1, 0)
    vend = jnp.cumsum(nvis)
    vstart = vend - nvis
    nv = vend[-1]
    v = jnp.minimum(jnp.arange(nvmax), nv - 1)
    e = jnp.sum((v[:, None] >= vend[None, :]).astype(jnp.int32), axis=1)
    onehot = e[:, None] == jnp.arange(ne)[None, :]
    pick = lambda a: jnp.sum(jnp.where(onehot, a[None, :], 0), axis=1)
    blk = pick(b_lo) + v - pick(vstart)
    lo = jnp.maximum(pick(offs), blk * rows)
    hi = jnp.minimum(pick(ends), (blk + 1) * rows)
    i32 = lambda a: a.astype(jnp.int32)
    return (i32(blk), i32(e), i32(lo), i32(hi), i32(nv).reshape(1)), offs, nvmax


def _tiles(s):
    tm = min(512, s)
    return dict(tm=tm, t_full=min(512, s), t_b=min(128, s), t_w=min(256, s), t_cmp=min(128, s),
                tm_moe=min(512, s), rows=256)


def _layer(l, x, mod_l, p, consts, final):
    b, s, d = x.shape
    tl = _tiles(s)
    wp, wf = _prep_w_in(p["w_in"])
    qa, qb, qc, qd, gates, kvcmp, fcum = _in_projection(
        x, mod_l, p["norm_attn_g"], wp, wf, p["fox_forget_b"], tl["tm"])
    lam8 = jnp.zeros((8, LANES), F32).at[:4, :DIFF_DK].set(
        jnp.stack([p["diff_lam_q1"], p["diff_lam_k1"], p["diff_lam_q2"], p["diff_lam_k2"]]))
    sg = jnp.tile(p["diff_subln_g"], 2).reshape(1, LANES)
    ya = _flash("A", tl["t_full"], l, qa, 0, qa, 2, qa, 4, (lam8, sg), False)
    yb = _flash("B", tl["t_b"], l, qb, 0, qb, 2, qb, 4, (p["swa_sinks"],), False)
    yc = _flash("C", tl["t_full"], l, qc, 0, qc, 2, qc, 4, (fcum,), False)
    n16 = s // CMP_STRIDE
    kv_r = jnp.stack([kvcmp[..., :HEAD_DIM].reshape(b, n16, CMP_STRIDE * HEAD_DIM),
                      kvcmp[..., HEAD_DIM:].reshape(b, n16, CMP_STRIDE * HEAD_DIM)])
    pos8 = jnp.zeros((2, 8, CMP_BLOCK * HEAD_DIM), F32).at[:, 0].set(
        p["nsa_cmp_pos"].reshape(2, CMP_BLOCK * HEAD_DIM))
    w2d = jnp.concatenate([p["nsa_cmp_w2"], p["nsa_cmp_w2"]], axis=-1)
    kvc = _compress(kv_r, p["nsa_cmp_w1"], pos8, p["nsa_cmp_b1"].reshape(2, 1, CMP_HIDDEN), w2d)
    ocmp, unsel = _cmp_attention(tl["t_cmp"], qd, kvc, consts["cover"])
    osel = _flash("Dsel", tl["t_full"], l, qd, 0, qd, 2, qd, 3, (unsel, consts["eneg"]), True)
    owin = _flash("Dwin", tl["t_w"], l, qd, 0, qd, 4, qd, 5, (), True)
    rw = jnp.zeros((d, LANES), F32).at[:, :N_EXPERTS].set(p["router_w"])
    rb = jnp.zeros((1, LANES), F32).at[0, :N_EXPERTS].set(p["router_b"])
    xm, h2, ri, rg, cnt = _out_projection(
        ya, yb, yc, ocmp, osel, owin, gates, x, mod_l, p["w_out"].astype(BF16),
        p["norm_moe_g"], rw, rb, tl["tm"])
    t = b * s
    counts = cnt[0, :N_EXPERTS]
    rows = tl["rows"]
    meta, offs, nvmax = _visit_plan(counts, rows, 2 * t // rows)
    ri = ri.reshape(t, 8)
    onehot = ri[:, 0:2, None] == jnp.arange(N_EXPERTS)[None, None, :]
    pos = (jnp.sum(jnp.where(onehot, offs[None, None, :], 0), axis=-1) + ri[:, 2:4]).astype(jnp.int32)
    pos = pos.reshape(2 * t)
    xs = _dispatch(h2.reshape(t, d), pos, tl["tm_moe"])
    ys = _experts(xs, meta, p["exp_w_gate"], p["exp_w_up"], p["exp_w_down"], rows, nvmax)
    out = _combine(final, pos, xm.reshape(t, d), rg.reshape(t, 8), mod_l, p["norm_final_g"], ys, tl["tm_moe"])
    return out.reshape(b, s, d)


def kernel(x, c, ada_w, ada_b, norm_attn_g, norm_moe_g, norm_final_g, w_in, w_out, diff_lam_q1, diff_lam_k1, diff_lam_q2, diff_lam_k2, diff_subln_g, swa_sinks, fox_forget_b, nsa_cmp_pos, nsa_cmp_w1, nsa_cmp_b1, nsa_cmp_w2, router_w, router_b, exp_w_gate, exp_w_up, exp_w_down):
    depth = ada_w.shape[0]
    s = x.shape[1]
    mod = _modulation(c, ada_w, ada_b)
    consts = dict(cover=_cover_matrix(s), eneg=_expand_neg(s))
    for l in range(depth):
        p = dict(
            norm_attn_g=norm_attn_g[l], norm_moe_g=norm_moe_g[l], norm_final_g=norm_final_g,
            w_in=w_in[l], w_out=w_out[l],
            diff_lam_q1=diff_lam_q1[l], diff_lam_k1=diff_lam_k1[l],
            diff_lam_q2=diff_lam_q2[l], diff_lam_k2=diff_lam_k2[l], diff_subln_g=diff_subln_g[l],
            swa_sinks=swa_sinks[l], fox_forget_b=fox_forget_b[l],
            nsa_cmp_pos=nsa_cmp_pos[l], nsa_cmp_w1=nsa_cmp_w1[l], nsa_cmp_b1=nsa_cmp_b1[l],
            nsa_cmp_w2=nsa_cmp_w2[l], router_w=router_w, router_b=router_b,
            exp_w_gate=exp_w_gate[l], exp_w_up=exp_w_up[l], exp_w_down=exp_w_down[l],
        )
        x = _layer(l, x, mod[l], p, consts, final=(l == depth - 1))
    return x
```

```python
import functools
import math

import numpy as np
import jax
import jax.numpy as jnp
from jax import lax
from jax.experimental import pallas as pl
from jax.experimental.pallas import tpu as pltpu

F32 = jnp.float32
BF16 = jnp.bfloat16

LANES = 128
HEAD_DIM = 64
N_MIXERS = 4
HEADS = 4
GROUP_WIDTH = HEADS * HEAD_DIM
DIFF_DK = HEAD_DIM // 2
SWA_WINDOW = 128
SWA_KV_HEADS = 2
CMP_BLOCK = 32
CMP_STRIDE = 16
CMP_HIDDEN = 2 * HEAD_DIM
SEL_BLOCK = 64
SEL_TOPK = 16
NSA_WINDOW = 512
FORCE_SCORE = 1e4
NEG_INF = -1e30
MASK_BIG = 2.0 ** 100
N_EXPERTS = 64
EXPERTS_PER_GROUP = 8
EPS = 1e-6
LOG2E = math.log2(math.e)
VMEM_LIMIT = 56 * 1024 * 1024

IN_SPLITS = (
    GROUP_WIDTH, GROUP_WIDTH, GROUP_WIDTH,
    GROUP_WIDTH, SWA_KV_HEADS * HEAD_DIM, SWA_KV_HEADS * HEAD_DIM,
    GROUP_WIDTH, GROUP_WIDTH, GROUP_WIDTH, HEADS,
    GROUP_WIDTH, HEAD_DIM, HEAD_DIM, HEAD_DIM, HEAD_DIM,
    HEAD_DIM, HEAD_DIM, 3 * HEADS,
)

_NT = (((1,), (1,)), ((), ()))


def _alibi_slopes():
    n = 3 * HEADS
    m = 2.0 ** (-8.0 * np.arange(1, n + 1) / n)
    return m.reshape(HEADS, 3).T


SLOPES = _alibi_slopes()


def _dot(a, b):
    return jnp.dot(a, b, preferred_element_type=F32)


def _dot_nt(a, b):
    return lax.dot_general(a, b, _NT, preferred_element_type=F32)


def _split_bf16(a):
    hi = a.astype(BF16)
    lo = (a - hi.astype(F32)).astype(BF16)
    return hi, lo


def _dot_f32(a, b, nt=False):
    d = _dot_nt if nt else _dot
    ah, al = _split_bf16(a)
    bh, bl = _split_bf16(b)
    return d(ah, bh) + (d(ah, bl) + d(al, bh))


def _cparams(sem):
    return pltpu.CompilerParams(dimension_semantics=sem, vmem_limit_bytes=VMEM_LIMIT)


def _mod_kernel(c_ref, w_ref, b_ref, o_ref):
    c = c_ref[...]
    cond = c * jax.nn.sigmoid(c)
    o_ref[0] = _dot_f32(cond, w_ref[0]) + b_ref[0]


def _modulation(c, ada_w, ada_b):
    depth, d, n = ada_w.shape
    b = c.shape[0]
    rows = 8
    cp = jnp.zeros((rows, d), F32).at[:b].set(c)
    tn = 1536
    out = pl.pallas_call(
        _mod_kernel,
        grid=(depth, n // tn),
        in_specs=[
            pl.BlockSpec((rows, d), lambda l, j: (0, 0)),
            pl.BlockSpec((1, d, tn), lambda l, j: (l, 0, j)),
            pl.BlockSpec((1, 1, tn), lambda l, j: (l, 0, j)),
        ],
        out_specs=pl.BlockSpec((1, rows, tn), lambda l, j: (l, 0, j)),
        out_shape=jax.ShapeDtypeStruct((depth, rows, n), F32),
        compiler_params=_cparams(("arbitrary", "arbitrary")),
        name="adaln_mod",
    )(cp, ada_w, ada_b.reshape(depth, 1, n))
    return out[:, :b].reshape(depth, b, 6, d)


W_A, W_B, W_C, W_D, W_G, W_CMP = 768, 768, 768, 768, 768, 128
W_OFFS = np.cumsum((0, W_A, W_B, W_C, W_D, W_G, W_CMP))
W_TOTAL = int(W_OFFS[-1])


def _prep_w_in(w):
    offs = np.cumsum((0,) + IN_SPLITS)
    col = lambda i: w[:, int(offs[i]):int(offs[i + 1])]
    dup = lambda t: jnp.concatenate([t, t], axis=1)
    kb, vb = col(4), col(5)
    gates = col(17)
    gexp = [jnp.repeat(gates[:, h * 3 + r:h * 3 + r + 1], HEAD_DIM, axis=1)
            for r in range(3) for h in range(HEADS)]
    qs64 = LOG2E * HEAD_DIM ** -0.5
    parts = [
        col(0) * (LOG2E * DIFF_DK ** -0.5), col(1), col(2),
        col(3) * qs64,
        dup(kb[:, :HEAD_DIM]), dup(kb[:, HEAD_DIM:]),
        dup(vb[:, :HEAD_DIM]), dup(vb[:, HEAD_DIM:]),
        col(6) * qs64, col(7), col(8),
        col(10) * qs64, dup(col(13)), dup(col(14)), dup(col(15)), dup(col(16)),
    ] + gexp + [col(11), col(12)]
    wp = jnp.concatenate(parts, axis=1).astype(BF16)
    wf = jnp.zeros((8, w.shape[0]), F32).at[:HEADS].set(col(9).T).astype(BF16)
    return wp, wf


def _lane_cumsum(y):
    n = y.shape[1]
    lane = lax.broadcasted_iota(jnp.int32, y.shape, 1)
    sh = 1
    while sh < n:
        y = y + jnp.where(lane >= sh, pltpu.roll(y, sh, 1), 0.0)
        sh *= 2
    return y


def _inproj_kernel(x_ref, mod_ref, g_ref, w_ref, wf_ref, bf_ref,
                   oa, ob, oc, od, og, ocmp, ofc, carry):
    i = pl.program_id(1)
    x = x_ref[0]
    ms = jnp.mean(x * x, axis=-1, keepdims=True)
    y = x * lax.rsqrt(ms + EPS) * g_ref[...]
    h = y * (1.0 + mod_ref[1:2, :]) + mod_ref[0:1, :]
    hb = h.astype(BF16)
    o = W_OFFS
    oa[0] = _dot(hb, w_ref[:, o[0]:o[1]]).astype(BF16)
    ob[0] = _dot(hb, w_ref[:, o[1]:o[2]]).astype(BF16)
    oc[0] = _dot(hb, w_ref[:, o[2]:o[3]]).astype(BF16)
    od[0] = _dot(hb, w_ref[:, o[3]:o[4]]).astype(BF16)
    og[0] = jax.nn.sigmoid(_dot(hb, w_ref[:, o[4]:o[5]])).astype(BF16)
    ocmp[0] = _dot(hb, w_ref[:, o[5]:o[6]])
    fl = _dot_nt(wf_ref[...], hb) + bf_ref[:, 0:1]
    ls = jnp.minimum(fl, 0.0) - jnp.log1p(jnp.exp(-jnp.abs(fl)))

    @pl.when(i == 0)
    def _():
        carry[...] = jnp.zeros_like(carry)

    cs = _lane_cumsum(ls) + carry[:, 0:1]
    ofc[0] = cs
    carry[...] = jnp.broadcast_to(cs[:, cs.shape[1] - 1:], carry.shape)


def _in_projection(x, mod_l, g, wp, wf, bf, tm):
    b, s, d = x.shape
    bf8 = jnp.zeros((8, LANES), F32).at[:HEADS, :].set(bf[:, None])
    outs = pl.pallas_call(
        _inproj_kernel,
        grid=(b, s // tm),
        in_specs=[
            pl.BlockSpec((1, tm, d), lambda bi, i: (bi, i, 0)),
            pl.BlockSpec((None, 6, d), lambda bi, i: (bi, 0, 0)),
            pl.BlockSpec((1, d), lambda bi, i: (0, 0)),
            pl.BlockSpec((d, W_TOTAL), lambda bi, i: (0, 0)),
            pl.BlockSpec((8, d), lambda bi, i: (0, 0)),
            pl.BlockSpec((8, LANES), lambda bi, i: (0, 0)),
        ],
        out_specs=[
            pl.BlockSpec((1, tm, W_A), lambda bi, i: (bi, i, 0)),
            pl.BlockSpec((1, tm, W_B), lambda bi, i: (bi, i, 0)),
            pl.BlockSpec((1, tm, W_C), lambda bi, i: (bi, i, 0)),
            pl.BlockSpec((1, tm, W_D), lambda bi, i: (bi, i, 0)),
            pl.BlockSpec((1, tm, W_G), lambda bi, i: (bi, i, 0)),
            pl.BlockSpec((1, tm, W_CMP), lambda bi, i: (bi, i, 0)),
            pl.BlockSpec((1, 8, tm), lambda bi, i: (bi, 0, i)),
        ],
        out_shape=[
            jax.ShapeDtypeStruct((b, s, W_A), BF16),
            jax.ShapeDtypeStruct((b, s, W_B), BF16),
            jax.ShapeDtypeStruct((b, s, W_C), BF16),
            jax.ShapeDtypeStruct((b, s, W_D), BF16),
            jax.ShapeDtypeStruct((b, s, W_G), BF16),
            jax.ShapeDtypeStruct((b, s, W_CMP), F32),
            jax.ShapeDtypeStruct((b, 8, s), F32),
        ],
        scratch_shapes=[pltpu.VMEM((8, LANES), F32)],
        compiler_params=_cparams(("arbitrary", "arbitrary")),
        name="in_projection",
    )(x, mod_l, g.reshape(1, d), wp, wf, bf8)
    return outs


def _flash_kernel(kind, t, layer_idx, *refs):
    it = iter(refs)
    q_ref, k_ref, v_ref = next(it), next(it), next(it)
    fc_ref = next(it) if kind == "C" else None
    un_ref = next(it) if kind == "Dsel" else None
    lam_ref, sg_ref = (next(it), next(it)) if kind == "A" else (None, None)
    o_ref, m_scr, l_scr, acc_scr = next(it), next(it), next(it), next(it)

    p = pl.program_id(1)
    i = pl.program_id(2)
    nstream = 4 if kind == "A" else 2
    width = LANES // nstream
    mixer = {"A": 0, "Dsel": 2}.get(kind)

    q2 = q_ref[0]
    lane = lax.broadcasted_iota(jnp.int32, (t, LANES), 1)
    zero = jnp.zeros_like(q2)
    qms = [jnp.where((lane >= s * width) & (lane < (s + 1) * width), q2, zero)
           for s in range(nstream)]
    if kind == "Dsel":
        qms = [jnp.concatenate([qm, un_ref[0]], axis=1) for qm in qms]

    def head_of(s):
        return s // 2 if kind == "A" else s

    slopes = None
    if mixer is not None:
        sl = SLOPES[mixer] * LOG2E
        slopes = [jnp.where(p == 0, float(sl[head_of(s)]), float(sl[2 + head_of(s)]))
                  for s in range(nstream)]

    for s in range(nstream):
        m_scr[s] = jnp.full((t, LANES), NEG_INF, F32)
        l_scr[s] = jnp.zeros((t, LANES), F32)
        acc_scr[s] = jnp.zeros((t, LANES), F32)

    def tile(j, diag):
        start = pl.multiple_of(j * t, t)
        k2 = k_ref[0, pl.ds(start, t), :]
        v2 = v_ref[0, pl.ds(start, t), :]
        col = lax.broadcasted_iota(jnp.int32, (1, t), 1)
        rel = ((j - i) * t + col).astype(F32)
        if diag:
            mask = (lax.broadcasted_iota(jnp.int32, (t, t), 0)
                    >= lax.broadcasted_iota(jnp.int32, (t, t), 1))
        for s in range(nstream):
            sc = _dot_nt(qms[s], k2)
            if slopes is not None:
                sc = sc + slopes[s] * rel
            if kind == "C":
                sc = sc - LOG2E * fc_ref[0, pl.ds(2 * p + s, 1), pl.ds(start, t)]
            if diag:
                sc = jnp.where(mask, sc, NEG_INF)
            m_prev = m_scr[s]
            m_next = jnp.maximum(m_prev, jnp.max(sc, axis=1, keepdims=True))
            alpha = jnp.exp2(m_prev - m_next)
            pexp = jnp.exp2(sc - jnp.tile(m_next, (1, t // LANES)))
            l_scr[s] = alpha * l_scr[s] + jnp.sum(pexp, axis=1, keepdims=True)
            acc_scr[s] = alpha * acc_scr[s] + _dot(pexp.astype(BF16), v2)
            m_scr[s] = m_next

    tile(i, True)

    def body(j, carry):
        tile(j, False)
        return carry

    lax.fori_loop(0, i, body, 0)

    outs = [acc_scr[s] / l_scr[s] for s in range(nstream)]
    if kind == "A":
        lam_init = 0.8 - 0.6 * math.exp(-0.3 * layer_idx)
        t1 = jnp.sum(lam_ref[0:1, :] * lam_ref[1:2, :], axis=1, keepdims=True)
        t2 = jnp.sum(lam_ref[2:3, :] * lam_ref[3:4, :], axis=1, keepdims=True)
        lam = jnp.exp(t1) - jnp.exp(t2) + lam_init
        d0 = outs[0] - lam * outs[1]
        d1 = outs[2] - lam * outs[3]
        lo_half = lane < HEAD_DIM
        o = jnp.where(lo_half, d0, d1)
        sq = o * o
        ss_lo = jnp.sum(jnp.where(lo_half, sq, 0.0), axis=1, keepdims=True)
        ss_hi = jnp.sum(jnp.where(lo_half, 0.0, sq), axis=1, keepdims=True)
        ms = jnp.where(lo_half, ss_lo, ss_hi) * (1.0 / HEAD_DIM)
        o = o * lax.rsqrt(ms + EPS) * sg_ref[...] * (1.0 - lam_init)
    else:
        o = jnp.where(lane < HEAD_DIM, outs[0], outs[1])
    o_ref[0] = o.astype(o_ref.dtype)


def _flash(kind, t, layer_idx, q_arr, qoff, k_arr, koff, v_arr, voff, extras):
    b, s, _ = q_arr.shape
    nstream = 4 if kind == "A" else 2
    shared_kv = kind == "Dsel"
    kw = k_arr.shape[-1] if shared_kv else LANES
    kv_idx = (lambda off: (lambda bi, p, i: (bi, 0, off))) if shared_kv else \
             (lambda off: (lambda bi, p, i: (bi, 0, off + p)))
    in_specs = [
        pl.BlockSpec((1, t, LANES), lambda bi, p, i: (bi, i, qoff + p)),
        pl.BlockSpec((1, s, kw), kv_idx(koff)),
        pl.BlockSpec((1, s, LANES), kv_idx(voff)),
    ]
    args = [q_arr, k_arr, v_arr]
    if kind == "C":
        fcum, = extras
        in_specs.append(pl.BlockSpec((1, 8, s), lambda bi, p, i: (bi, 0, 0)))
        args.append(fcum)
    elif kind == "Dsel":
        unsel, = extras
        nsel = unsel.shape[-1]
        in_specs.append(pl.BlockSpec((1, t, nsel), lambda bi, p, i: (bi, i, 0)))
        args.append(unsel)
    elif kind == "A":
        lam8, sg = extras
        in_specs.append(pl.BlockSpec((8, LANES), lambda bi, p, i: (0, 0)))
        in_specs.append(pl.BlockSpec((1, LANES), lambda bi, p, i: (0, 0)))
        args += [lam8, sg]
    return pl.pallas_call(
        functools.partial(_flash_kernel, kind, t, layer_idx),
        grid=(b, 2, s // t),
        in_specs=in_specs,
        out_specs=pl.BlockSpec((1, t, LANES), lambda bi, p, i: (bi, i, p)),
        out_shape=jax.ShapeDtypeStruct((b, s, 2 * LANES), BF16),
        scratch_shapes=[pltpu.VMEM((nstream, t, LANES), F32)] * 3,
        compiler_params=_cparams(("arbitrary", "arbitrary", "arbitrary")),
        name="flash_" + kind,
    )(*args)


def _window_kernel(kind, t, wpad, *refs):
    it = iter(refs)
    q_ref, k_ref, v_ref = next(it), next(it), next(it)
    sink_ref = next(it) if kind == "B" else None
    o_ref = next(it)
    p = pl.program_id(1)
    i = pl.program_id(2)
    window = SWA_WINDOW if kind == "B" else NSA_WINDOW
    sl = SLOPES[1 if kind == "B" else 2] * LOG2E
    nk = wpad + t
    start = pl.multiple_of(jnp.maximum(i * t - wpad, 0), LANES)
    k2 = k_ref[0, pl.ds(start, nk), :]
    v2 = v_ref[0, pl.ds(start, nk), :]
    q2 = q_ref[0]
    lane = lax.broadcasted_iota(jnp.int32, (t, LANES), 1)
    zero = jnp.zeros_like(q2)
    dist = ((i * t - start) + lax.broadcasted_iota(jnp.int32, (t, nk), 0)
            - lax.broadcasted_iota(jnp.int32, (t, nk), 1))
    valid = (dist >= 0) & (dist < window)
    rel = (start - i * t + lax.broadcasted_iota(jnp.int32, (1, nk), 1)).astype(F32)
    qrow = lax.broadcasted_iota(jnp.int32, (t, 1), 0).astype(F32)
    outs = []
    for s in range(2):
        slope = jnp.where(p == 0, float(sl[s]), float(sl[2 + s]))
        qm = jnp.where((lane >= s * HEAD_DIM) & (lane < (s + 1) * HEAD_DIM), q2, zero)
        sc = jnp.where(valid, _dot_nt(qm, k2) + slope * rel, NEG_INF)
        m = jnp.max(sc, axis=1, keepdims=True)
        if kind == "B":
            sink = LOG2E * sink_ref[2 * p + s] + slope * qrow
            m = jnp.maximum(m, sink)
        pexp = jnp.exp2(sc - m)
        l = jnp.sum(pexp, axis=1, keepdims=True)
        if kind == "B":
            l = l + jnp.exp2(sink - m)
        outs.append(_dot(pexp.astype(BF16), v2) / l)
    o_ref[0] = jnp.where(lane < HEAD_DIM, outs[0], outs[1]).astype(o_ref.dtype)


def _window(kind, t, wpad, q_arr, qoff, k_arr, koff, v_arr, voff, extras):
    b, s, _ = q_arr.shape
    shared_kv = kind == "Dwin"
    kv_idx = (lambda off: (lambda bi, p, i: (bi, 0, off))) if shared_kv else \
             (lambda off: (lambda bi, p, i: (bi, 0, off + p)))
    in_specs = [
        pl.BlockSpec((1, t, LANES), lambda bi, p, i: (bi, i, qoff + p)),
        pl.BlockSpec((1, s, LANES), kv_idx(koff)),
        pl.BlockSpec((1, s, LANES), kv_idx(voff)),
    ]
    args = [q_arr, k_arr, v_arr]
    if kind == "B":
        in_specs.append(pl.BlockSpec(memory_space=pltpu.SMEM))
        args.append(extras[0])
    return pl.pallas_call(
        functools.partial(_window_kernel, kind, t, wpad),
        grid=(b, 2, s // t),
        in_specs=in_specs,
        out_specs=pl.BlockSpec((1, t, LANES), lambda bi, p, i: (bi, i, p)),
        out_shape=jax.ShapeDtypeStruct((b, s, 2 * LANES), BF16),
        compiler_params=_cparams(("arbitrary", "arbitrary", "arbitrary")),
        name="window_" + kind,
    )(*args)


def _gelu_tanh(x):
    return 0.5 * x * (1.0 + jnp.tanh(math.sqrt(2.0 / math.pi) * (x + 0.044715 * x * x * x)))


def _compress_kernel(t_ref, w1_ref, pos_ref, b1_ref, w2_ref, o_ref):
    tr = t_ref[0, 0]
    half = tr.shape[1]
    n = tr.shape[0]
    w1 = w1_ref[0]
    u = _dot_f32(tr, w1[:half])
    v = _dot_f32(tr, w1[half:])
    cpos = _dot_f32(pos_ref[0], w1)[0:1] + b1_ref[0]
    hid = u + pltpu.roll(v, n - 1, 0) + cpos
    o_ref[0, 0] = _dot_f32(_gelu_tanh(hid), w2_ref[0])


def _compress(kv_r, w1, pos8, b1, w2d):
    two, b, n, dd = kv_r.shape
    return pl.pallas_call(
        _compress_kernel,
        grid=(two, b),
        in_specs=[
            pl.BlockSpec((1, 1, n, dd), lambda c, bi: (c, bi, 0, 0)),
            pl.BlockSpec((1, 2 * dd, CMP_HIDDEN), lambda c, bi: (c, 0, 0)),
            pl.BlockSpec((1, 8, 2 * dd), lambda c, bi: (c, 0, 0)),
            pl.BlockSpec((1, 1, CMP_HIDDEN), lambda c, bi: (c, 0, 0)),
            pl.BlockSpec((1, CMP_HIDDEN, LANES), lambda c, bi: (c, 0, 0)),
        ],
        out_specs=pl.BlockSpec((1, 1, n, LANES), lambda c, bi: (c, bi, 0, 0)),
        out_shape=jax.ShapeDtypeStruct((two, b, n, LANES), F32),
        compiler_params=_cparams(("arbitrary", "arbitrary")),
        name="nsa_compress",
    )(kv_r, w1, pos8, b1, w2d)


def _cmp_kernel(t, q_ref, kc_ref, vc_ref, cov_ref, o_ref, un_ref):
    i = pl.program_id(1)
    ncp = kc_ref.shape[2]
    nsel = cov_ref.shape[0]
    kc = kc_ref[0, 0]
    vc = vc_ref[0, 0]
    kch, kcl = _split_bf16(kc)
    vch, vcl = _split_bf16(vc)
    lane = lax.broadcasted_iota(jnp.int32, (t, LANES), 1)
    tq = i * t + lax.broadcasted_iota(jnp.int32, (t, ncp), 0)
    nidx = lax.broadcasted_iota(jnp.int32, (t, ncp), 1)
    dci = tq - (nidx * CMP_STRIDE + CMP_BLOCK - 1)
    valid = (dci >= 0) & (nidx < ncp - 1)
    dc = dci.astype(F32)
    psum = jnp.zeros((t, ncp), F32)
    outs = []
    for pair in range(2):
        q2 = q_ref[0, :, pair * LANES:(pair + 1) * LANES]
        for g in range(2):
            h = 2 * pair + g
            qm = jnp.where((lane >= g * HEAD_DIM) & (lane < (g + 1) * HEAD_DIM), q2, jnp.zeros_like(q2))
            sc = _dot_nt(qm, kch) + _dot_nt(qm, kcl) - float(SLOPES[2][h] * LOG2E) * dc
            sc = jnp.where(valid, sc, NEG_INF)
            m = jnp.max(sc, axis=1, keepdims=True)
            pe = jnp.where(valid, jnp.exp2(sc - m), 0.0)
            l = jnp.sum(pe, axis=1, keepdims=True)
            pc = pe * (1.0 / jnp.where(l > 0.0, l, 1.0))
            psum = psum + pc
            ph, plo = _split_bf16(pc)
            outs.append(_dot(ph, vch) + (_dot(ph, vcl) + _dot(plo, vch)))
    lo_half = lane < HEAD_DIM
    o_ref[0, :, 0:LANES] = jnp.where(lo_half, outs[0], outs[1]).astype(o_ref.dtype)
    o_ref[0, :, LANES:2 * LANES] = jnp.where(lo_half, outs[2], outs[3]).astype(o_ref.dtype)
    ph, plo = _split_bf16(psum)
    cov = cov_ref[...]
    imp = _dot_nt(cov, ph) + _dot_nt(cov, plo)
    blk = lax.broadcasted_iota(jnp.int32, (nsel, t), 0)
    cur = (i * t + lax.broadcasted_iota(jnp.int32, (nsel, t), 1)) // SEL_BLOCK
    forced = (blk == 0) | (blk == cur) | (blk == cur - 1)
    score = jnp.where(forced, FORCE_SCORE, jnp.where(blk <= cur, imp, -1.0))
    unsel = jnp.ones((nsel, t), F32)
    for _ in range(min(SEL_TOPK, nsel)):
        mx = jnp.max(score, axis=0, keepdims=True)
        idx = jnp.min(jnp.where(score == mx, blk, nsel), axis=0, keepdims=True)
        hit = blk == idx
        unsel = jnp.where(hit, 0.0, unsel)
        score = jnp.where(hit, -3.0, score)
    un_ref[0] = unsel.T.astype(un_ref.dtype)


def _cmp_attention(t, qd, kvc, cover_t):
    b, s, _ = qd.shape
    ncp = kvc.shape[2]
    nsel = cover_t.shape[0]
    return pl.pallas_call(
        functools.partial(_cmp_kernel, t),
        grid=(b, s // t),
        in_specs=[
            pl.BlockSpec((1, t, 2 * LANES), lambda bi, i: (bi, i, 0)),
            pl.BlockSpec((1, 1, ncp, LANES), lambda bi, i: (0, bi, 0, 0)),
            pl.BlockSpec((1, 1, ncp, LANES), lambda bi, i: (1, bi, 0, 0)),
            pl.BlockSpec((nsel, ncp), lambda bi, i: (0, 0)),
        ],
        out_specs=[
            pl.BlockSpec((1, t, 2 * LANES), lambda bi, i: (bi, i, 0)),
            pl.BlockSpec((1, t, nsel), lambda bi, i: (bi, i, 0)),
        ],
        out_shape=[
            jax.ShapeDtypeStruct((b, s, 2 * LANES), BF16),
            jax.ShapeDtypeStruct((b, s, nsel), BF16),
        ],
        compiler_params=_cparams(("arbitrary", "arbitrary")),
        name="nsa_cmp_select",
    )(qd, kvc, kvc, cover_t)


def _cover_matrix(s):
    ncp, nsel = s // CMP_STRIDE, s // SEL_BLOCK
    cs = np.arange(ncp)[:, None] * CMP_STRIDE
    ss = np.arange(nsel)[None, :] * SEL_BLOCK
    cov = np.clip(np.minimum(cs + CMP_BLOCK, ss + SEL_BLOCK) - np.maximum(cs, ss), 0, None) / CMP_BLOCK
    cov[ncp - 1] = 0.0
    return jnp.asarray(cov.T, BF16)


def _expand_neg(s):
    nsel = s // SEL_BLOCK
    e = (np.arange(s)[:, None] // SEL_BLOCK) == np.arange(nsel)[None, :]
    return jnp.asarray(np.where(e, -MASK_BIG, 0.0), BF16)


def _outproj_kernel(ya, yb, yc, ocmp, osel, owin, gt, x_ref, mod_ref, wo_ref, g_ref,
                    rwt_ref, rb_ref, xo_ref, h_ref, ri_ref, rg_ref, cnt_ref, carry):
    first = (pl.program_id(0) == 0) & (pl.program_id(1) == 0)
    gw = GROUP_WIDTH
    g = gt[0].astype(F32)
    yd = (g[:, 0:gw] * ocmp[0].astype(F32) + g[:, gw:2 * gw] * osel[0].astype(F32)
          + g[:, 2 * gw:3 * gw] * owin[0].astype(F32))
    ycat = jnp.concatenate([ya[0], yb[0], yc[0], yd.astype(BF16)], axis=1)
    y = _dot(ycat, wo_ref[...])
    x = x_ref[0] + mod_ref[2:3, :] * y
    xo_ref[0] = x
    ms = jnp.mean(x * x, axis=-1, keepdims=True)
    h = x * lax.rsqrt(ms + EPS) * g_ref[...]
    h = h * (1.0 + mod_ref[4:5, :]) + mod_ref[3:4, :]
    h_ref[0] = h
    tm = h.shape[0]
    ng = LANES // EXPERTS_PER_GROUP
    big = 4 * LANES
    logits = _dot_f32(rwt_ref[...], h, nt=True)
    eidx = lax.broadcasted_iota(jnp.int32, (LANES, tm), 0)
    aff = jax.nn.sigmoid(logits)
    sel = jnp.where(eidx < N_EXPERTS, aff + rb_ref[:, 0:1], NEG_INF)
    sel3 = sel.reshape(ng, EXPERTS_PER_GROUP, tm)
    e3 = eidx.reshape(ng, EXPERTS_PER_GROUP, tm)
    g1 = jnp.max(sel3, axis=1, keepdims=True)
    i1 = jnp.min(jnp.where(sel3 == g1, e3, big), axis=1, keepdims=True)
    sel_b = jnp.where(e3 == i1, NEG_INF, sel3)
    g2 = jnp.max(sel_b, axis=1, keepdims=True)
    i2 = jnp.min(jnp.where(sel_b == g2, e3, big), axis=1, keepdims=True)
    gs = g1 + g2
    gidx = lax.broadcasted_iota(jnp.int32, (ng, 1, tm), 0)
    gm = jnp.max(gs, axis=0, keepdims=True)
    best = gidx == jnp.min(jnp.where(gs == gm, gidx, big), axis=0, keepdims=True)
    e1 = jnp.min(jnp.where(best, i1, big), axis=0)
    e2 = jnp.min(jnp.where(best, i2, big), axis=0)
    oh1 = eidx == e1
    oh2 = eidx == e2
    a1 = jnp.sum(jnp.where(oh1, aff, 0.0), axis=0, keepdims=True)
    a2 = jnp.sum(jnp.where(oh2, aff, 0.0), axis=0, keepdims=True)
    inv = 1.0 / (a1 + a2)
    @pl.when(first)
    def _():
        carry[...] = jnp.zeros_like(carry)

    ohs = jnp.where(oh1 | oh2, 1.0, 0.0)
    rr = lax.broadcasted_iota(jnp.int32, (tm, tm), 0)
    cc = lax.broadcasted_iota(jnp.int32, (tm, tm), 1)
    earlier = jnp.where(rr < cc, 1.0, 0.0).astype(BF16)
    before = _dot(ohs.astype(BF16), earlier) + carry[:, 0:1]
    r1 = jnp.sum(jnp.where(oh1, before, 0.0), axis=0, keepdims=True)
    r2 = jnp.sum(jnp.where(oh2, before, 0.0), axis=0, keepdims=True)
    total = carry[:, 0:1] + jnp.sum(ohs, axis=1, keepdims=True)
    carry[...] = jnp.broadcast_to(total, carry.shape)
    cnt_ref[...] = jnp.broadcast_to(total, cnt_ref.shape).astype(jnp.int32)
    row = lax.broadcasted_iota(jnp.int32, (8, tm), 0)
    ri_ref[0] = jnp.where(row == 0, e1, jnp.where(row == 1, e2, jnp.where(
        row == 2, r1.astype(jnp.int32), jnp.where(row == 3, r2.astype(jnp.int32), 0))))
    rg_ref[0] = jnp.where(row == 0, a1 * inv, jnp.where(row == 1, a2 * inv, 0.0))


def _out_projection(ya, yb, yc, ocmp, osel, owin, gates, x, mod_l, wo, g, rwt, rb, tm):
    b, s, d = x.shape
    gw = GROUP_WIDTH
    tok = lambda w: pl.BlockSpec((1, tm, w), lambda bi, i: (bi, i, 0))
    full = lambda shp: pl.BlockSpec(shp, lambda bi, i: (0,) * len(shp))
    rowblk = pl.BlockSpec((1, 8, tm), lambda bi, i: (bi, 0, i))
    return pl.pallas_call(
        _outproj_kernel,
        grid=(b, s // tm),
        in_specs=[tok(gw)] * 6 + [tok(3 * gw), tok(d),
                                  pl.BlockSpec((None, 6, d), lambda bi, i: (bi, 0, 0)),
                                  full((d, d)), full((1, d)), full((LANES, d)), full((LANES, 1))],
        out_specs=[tok(d), tok(d), rowblk, rowblk, full((LANES, LANES))],
        out_shape=[
            jax.ShapeDtypeStruct((b, s, d), F32),
            jax.ShapeDtypeStruct((b, s, d), F32),
            jax.ShapeDtypeStruct((b, 8, s), jnp.int32),
            jax.ShapeDtypeStruct((b, 8, s), F32),
            jax.ShapeDtypeStruct((LANES, LANES), jnp.int32),
        ],
        scratch_shapes=[pltpu.VMEM((LANES, LANES), F32)],
        compiler_params=_cparams(("arbitrary", "arbitrary")),
        name="out_projection_router",
    )(ya, yb, yc, ocmp, osel, owin, gates, x, mod_l, wo, g.reshape(1, d), rwt, rb)


def _dispatch_kernel(p0_ref, p1_ref, h_ref, xs_ref, sem):
    tm = h_ref.shape[0]

    def issue(r, c):
        for pos_ref in (p0_ref, p1_ref):
            pltpu.make_async_copy(h_ref.at[pl.ds(r, 1), :],
                                  xs_ref.at[pl.ds(pos_ref[r], 1), :], sem).start()
        return c

    lax.fori_loop(0, tm, issue, 0, unroll=4)
    for _ in range(2):
        pltpu.make_async_copy(h_ref, xs_ref.at[pl.ds(0, tm), :], sem).wait()


def _dispatch(h, pos0, pos1, tm):
    t, d = h.shape
    return pl.pallas_call(
        _dispatch_kernel,
        grid=(t // tm,),
        in_specs=[
            pl.BlockSpec((tm,), lambda i: (i,), memory_space=pltpu.SMEM),
            pl.BlockSpec((tm,), lambda i: (i,), memory_space=pltpu.SMEM),
            pl.BlockSpec((tm, d), lambda i: (i, 0)),
        ],
        out_specs=pl.BlockSpec(memory_space=pl.ANY),
        out_shape=jax.ShapeDtypeStruct((2 * t, d), F32),
        scratch_shapes=[pltpu.SemaphoreType.DMA(())],
        compiler_params=_cparams(("arbitrary",)),
        name="moe_dispatch",
    )(pos0, pos1, h)


def _expert_kernel(vb_ref, ve_ref, vlo_ref, vhi_ref, nv_ref,
                   xs_ref, wg_ref, wu_ref, wd_ref, ys_ref, wgb, wub, wdb):
    v = pl.program_id(0)
    rows = xs_ref.shape[0]
    prev = jnp.maximum(v - 1, 0)
    new_expert = (v == 0) | (ve_ref[v] != ve_ref[prev])
    new_block = (v == 0) | (vb_ref[v] != vb_ref[prev])

    @pl.when(v < nv_ref[0])
    def _():
        @pl.when(new_expert)
        def _():
            wgb[...] = wg_ref[0, 0].astype(BF16)
            wub[...] = wu_ref[0, 0].astype(BF16)
            wdb[...] = wd_ref[0, 0].astype(BF16)

        x = xs_ref[...].astype(BF16)
        hg = _dot(x, wgb[...])
        hu = _dot(x, wub[...])
        hm = (hg * jax.nn.sigmoid(hg) * hu).astype(BF16)
        y = _dot(hm, wdb[...])
        r = vb_ref[v] * rows + lax.broadcasted_iota(jnp.int32, (rows, 1), 0)
        mine = (r >= vlo_ref[v]) & (r < vhi_ref[v])

        @pl.when(new_block)
        def _():
            ys_ref[...] = jnp.where(mine, y, 0.0)

        @pl.when(jnp.logical_not(new_block))
        def _():
            ys_ref[...] = jnp.where(mine, y, ys_ref[...])


def _experts(xs, meta, layer, wg, wu, wd, rows, nvmax):
    n, d = xs.shape
    de = wg.shape[3]
    grid_spec = pltpu.PrefetchScalarGridSpec(
        num_scalar_prefetch=5,
        grid=(nvmax,),
        in_specs=[
            pl.BlockSpec((rows, d), lambda v, vb, ve, lo, hi, nv: (vb[v], 0)),
            pl.BlockSpec((1, 1, d, de), lambda v, vb, ve, lo, hi, nv: (layer, ve[v], 0, 0)),
            pl.BlockSpec((1, 1, d, de), lambda v, vb, ve, lo, hi, nv: (layer, ve[v], 0, 0)),
            pl.BlockSpec((1, 1, de, d), lambda v, vb, ve, lo, hi, nv: (layer, ve[v], 0, 0)),
        ],
        out_specs=pl.BlockSpec((rows, d), lambda v, vb, ve, lo, hi, nv: (vb[v], 0)),
        scratch_shapes=[pltpu.VMEM((d, de), BF16), pltpu.VMEM((d, de), BF16), pltpu.VMEM((de, d), BF16)],
    )
    return pl.pallas_call(
        _expert_kernel,
        grid_spec=grid_spec,
        out_shape=jax.ShapeDtypeStruct((n, d), F32),
        compiler_params=_cparams(("arbitrary",)),
        name="moe_experts",
    )(*meta, xs, wg, wu, wd)


def _combine_kernel(final, p0_ref, p1_ref, x_ref, rg_ref, mod_ref, g_ref, ys_ref, o_ref, buf, sem):
    tm = x_ref.shape[0]

    def issue(r, c):
        for k, pos_ref in enumerate((p0_ref, p1_ref)):
            pltpu.make_async_copy(ys_ref.at[pl.ds(pos_ref[r], 1), :],
                                  buf.at[k, pl.ds(r, 1), :], sem).start()
        return c

    lax.fori_loop(0, tm, issue, 0, unroll=4)
    for k in range(2):
        pltpu.make_async_copy(ys_ref.at[pl.ds(0, tm), :], buf.at[k], sem).wait()
    rg = rg_ref[...]
    y = rg[:, 0:1] * buf[0] + rg[:, 1:2] * buf[1]
    x = x_ref[...] + mod_ref[5:6, :] * y
    if final:
        ms = jnp.mean(x * x, axis=-1, keepdims=True)
        x = x * lax.rsqrt(ms + EPS) * g_ref[...]
    o_ref[...] = x


def _combine(final, pos0, pos1, x, rg, mod_l, g, ys, tm):
    t, d = x.shape
    per_b = t // mod_l.shape[0] // tm
    return pl.pallas_call(
        functools.partial(_combine_kernel, final),
        grid=(t // tm,),
        in_specs=[
            pl.BlockSpec((tm,), lambda i: (i,), memory_space=pltpu.SMEM),
            pl.BlockSpec((tm,), lambda i: (i,), memory_space=pltpu.SMEM),
            pl.BlockSpec((tm, d), lambda i: (i, 0)),
            pl.BlockSpec((tm, 8), lambda i: (i, 0)),
            pl.BlockSpec((None, 6, d), lambda i: (i // per_b, 0, 0)),
            pl.BlockSpec((1, d), lambda i: (0, 0)),
            pl.BlockSpec(memory_space=pl.ANY),
        ],
        out_specs=pl.BlockSpec((tm, d), lambda i: (i, 0)),
        out_shape=jax.ShapeDtypeStruct((t, d), F32),
        scratch_shapes=[pltpu.VMEM((2, tm, d), F32), pltpu.SemaphoreType.DMA(())],
        compiler_params=_cparams(("arbitrary",)),
        name="moe_combine",
    )(pos0, pos1, x, rg, mod_l, g.reshape(1, d), ys)


def _visit_plan(counts, rows, nblocks):
    ne = counts.shape[0]
    nvmax = nblocks + ne - 1
    ends = jnp.cumsum(counts)
    offs = ends - counts
    b_lo = offs // rows
    b_hi = jnp.maximum(ends - 1, 0) // rows
    nvis = jnp.where(counts > 0, b_hi - b_lo + 1, 0)
    vend = jnp.cumsum(nvis)
    vstart = vend - nvis
    nv = vend[-1]
    v = jnp.minimum(jnp.arange(nvmax), nv - 1)
    e = jnp.sum((v[:, None] >= vend[None, :]).astype(jnp.int32), axis=1)
    onehot = e[:, None] == jnp.arange(ne)[None, :]
    pick = lambda a: jnp.sum(jnp.where(onehot, a[None, :], 0), axis=1)
    blk = pick(b_lo) + v - pick(vstart)
    lo = jnp.maximum(pick(offs), blk * rows)
    hi = jnp.minimum(pick(ends), (blk + 1) * rows)
    i32 = lambda a: a.astype(jnp.int32)
    return (i32(blk), i32(e), i32(lo), i32(hi), i32(nv).reshape(1)), offs, nvmax


def _tiles(s):
    tm = min(512, s)
    return dict(tm=tm, t_full=min(512, s), t_win=min(256, s), t_cmp=min(256, s),
                tm_moe=min(512, s), rows=256)


def _layer(l, x, mod_l, p, consts, final):
    b, s, d = x.shape
    tl = _tiles(s)
    wp, wf = _prep_w_in(p["w_in"])
    qa, qb, qc, qd, gates, kvcmp, fcum = _in_projection(
        x, mod_l, p["norm_attn_g"], wp, wf, p["fox_forget_b"], tl["tm"])
    lam8 = jnp.zeros((8, LANES), F32).at[:4, :DIFF_DK].set(
        jnp.stack([p["diff_lam_q1"], p["diff_lam_k1"], p["diff_lam_q2"], p["diff_lam_k2"]]))
    sg = jnp.tile(p["diff_subln_g"], 2).reshape(1, LANES)
    ya = _flash("A", tl["t_full"], l, qa, 0, qa, 2, qa, 4, (lam8, sg))
    yb = _window("B", tl["t_win"], SWA_WINDOW, qb, 0, qb, 2, qb, 4, (p["swa_sinks"],))
    yc = _flash("C", tl["t_full"], l, qc, 0, qc, 2, qc, 4, (fcum,))
    n16 = s // CMP_STRIDE
    kv_r = jnp.stack([kvcmp[..., :HEAD_DIM].reshape(b, n16, CMP_STRIDE * HEAD_DIM),
                      kvcmp[..., HEAD_DIM:].reshape(b, n16, CMP_STRIDE * HEAD_DIM)])
    pos8 = jnp.zeros((2, 8, CMP_BLOCK * HEAD_DIM), F32).at[:, 0].set(
        p["nsa_cmp_pos"].reshape(2, CMP_BLOCK * HEAD_DIM))
    w2d = jnp.concatenate([p["nsa_cmp_w2"], p["nsa_cmp_w2"]], axis=-1)
    kvc = _compress(kv_r, p["nsa_cmp_w1"], pos8, p["nsa_cmp_b1"].reshape(2, 1, CMP_HIDDEN), w2d)
    ocmp, unsel = _cmp_attention(tl["t_cmp"], qd, kvc, consts["cover"])
    ksel = jnp.concatenate([qd[..., 2 * LANES:3 * LANES],
                            jnp.broadcast_to(consts["eneg"][None], (b,) + consts["eneg"].shape)], axis=-1)
    osel = _flash("Dsel", tl["t_full"], l, qd, 0, ksel, 0, qd, 3, (unsel,))
    owin = _window("Dwin", tl["t_win"], NSA_WINDOW, qd, 0, qd, 4, qd, 5, ())
    rwt = jnp.zeros((LANES, d), F32).at[:N_EXPERTS].set(p["router_w"].T)
    rb = jnp.zeros((LANES, 1), F32).at[:N_EXPERTS, 0].set(p["router_b"])
    xm, h2, ri, rg, cnt = _out_projection(
        ya, yb, yc, ocmp, osel, owin, gates, x, mod_l, p["w_out"].astype(BF16),
        p["norm_moe_g"], rwt, rb, tl["tm"])
    t = b * s
    counts = cnt[:N_EXPERTS, 0]
    rows = tl["rows"]
    meta, offs, nvmax = _visit_plan(counts, rows, 2 * t // rows)
    onehot = ri[:, 0:2, :, None] == jnp.arange(N_EXPERTS)[None, None, None, :]
    pos = (jnp.sum(jnp.where(onehot, offs[None, None, None, :], 0), axis=-1) + ri[:, 2:4]).astype(jnp.int32)
    pos0, pos1 = pos[:, 0].reshape(t), pos[:, 1].reshape(t)
    rgt = rg.transpose(0, 2, 1).reshape(t, 8)
    xs = _dispatch(h2.reshape(t, d), pos0, pos1, tl["tm_moe"])
    ys = _experts(xs, meta, l, p["exp_w_gate"], p["exp_w_up"], p["exp_w_down"], rows, nvmax)
    out = _combine(final, pos0, pos1, xm.reshape(t, d), rgt, mod_l, p["norm_final_g"], ys, tl["tm_moe"])
    return out.reshape(b, s, d)


def kernel(x, c, ada_w, ada_b, norm_attn_g, norm_moe_g, norm_final_g, w_in, w_out, diff_lam_q1, diff_lam_k1, diff_lam_q2, diff_lam_k2, diff_subln_g, swa_sinks, fox_forget_b, nsa_cmp_pos, nsa_cmp_w1, nsa_cmp_b1, nsa_cmp_w2, router_w, router_b, exp_w_gate, exp_w_up, exp_w_down):
    depth = ada_w.shape[0]
    s = x.shape[1]
    mod = _modulation(c, ada_w, ada_b)
    consts = dict(cover=_cover_matrix(s), eneg=_expand_neg(s))
    for l in range(depth):
        p = dict(
            norm_attn_g=norm_attn_g[l], norm_moe_g=norm_moe_g[l], norm_final_g=norm_final_g,
            w_in=w_in[l], w_out=w_out[l],
            diff_lam_q1=diff_lam_q1[l], diff_lam_k1=diff_lam_k1[l],
            diff_lam_q2=diff_lam_q2[l], diff_lam_k2=diff_lam_k2[l], diff_subln_g=diff_subln_g[l],
            swa_sinks=swa_sinks[l], fox_forget_b=fox_forget_b[l],
            nsa_cmp_pos=nsa_cmp_pos[l], nsa_cmp_w1=nsa_cmp_w1[l], nsa_cmp_b1=nsa_cmp_b1[l],
            nsa_cmp_w2=nsa_cmp_w2[l], router_w=router_w, router_b=router_b,
            exp_w_gate=exp_w_gate, exp_w_up=exp_w_up, exp_w_down=exp_w_down,
        )
        x = _layer(l, x, mod[l], p, consts, final=(l == depth - 1))
    return x
```

```python
import functools
import math

import numpy as np
import jax
import jax.numpy as jnp
from jax import lax
from jax.experimental import pallas as pl
from jax.experimental.pallas import tpu as pltpu

F32 = jnp.float32
BF16 = jnp.bfloat16

LANES = 128
HEAD_DIM = 64
N_MIXERS = 4
HEADS = 4
GROUP_WIDTH = HEADS * HEAD_DIM
DIFF_DK = HEAD_DIM // 2
SWA_WINDOW = 128
SWA_KV_HEADS = 2
CMP_BLOCK = 32
CMP_STRIDE = 16
CMP_HIDDEN = 2 * HEAD_DIM
SEL_BLOCK = 64
SEL_TOPK = 16
NSA_WINDOW = 512
FORCE_SCORE = 1e4
NEG_INF = -1e30
MASK_BIG = 2.0 ** 100
N_EXPERTS = 64
EXPERTS_PER_GROUP = 8
EPS = 1e-6
LOG2E = math.log2(math.e)
VMEM_LIMIT = 56 * 1024 * 1024

IN_SPLITS = (
    GROUP_WIDTH, GROUP_WIDTH, GROUP_WIDTH,
    GROUP_WIDTH, SWA_KV_HEADS * HEAD_DIM, SWA_KV_HEADS * HEAD_DIM,
    GROUP_WIDTH, GROUP_WIDTH, GROUP_WIDTH, HEADS,
    GROUP_WIDTH, HEAD_DIM, HEAD_DIM, HEAD_DIM, HEAD_DIM,
    HEAD_DIM, HEAD_DIM, 3 * HEADS,
)

_NT = (((1,), (1,)), ((), ()))


def _alibi_slopes():
    n = 3 * HEADS
    m = 2.0 ** (-8.0 * np.arange(1, n + 1) / n)
    return m.reshape(HEADS, 3).T


SLOPES = _alibi_slopes()


def _dot(a, b):
    return jnp.dot(a, b, preferred_element_type=F32)


def _dot_nt(a, b):
    return lax.dot_general(a, b, _NT, preferred_element_type=F32)


def _split_bf16(a):
    hi = a.astype(BF16)
    lo = (a - hi.astype(F32)).astype(BF16)
    return hi, lo


def _dot_f32(a, b, nt=False):
    d = _dot_nt if nt else _dot
    ah, al = _split_bf16(a)
    bh, bl = _split_bf16(b)
    return d(ah, bh) + (d(ah, bl) + d(al, bh))


def _cparams(sem):
    return pltpu.CompilerParams(dimension_semantics=sem, vmem_limit_bytes=VMEM_LIMIT)


def _mod_kernel(c_ref, w_ref, b_ref, o_ref):
    c = c_ref[...]
    cond = c * jax.nn.sigmoid(c)
    o_ref[0] = _dot_f32(cond, w_ref[0]) + b_ref[0]


def _modulation(c, ada_w, ada_b):
    depth, d, n = ada_w.shape
    b = c.shape[0]
    rows = 8
    cp = jnp.zeros((rows, d), F32).at[:b].set(c)
    tn = 1536
    out = pl.pallas_call(
        _mod_kernel,
        grid=(depth, n // tn),
        in_specs=[
            pl.BlockSpec((rows, d), lambda l, j: (0, 0)),
            pl.BlockSpec((1, d, tn), lambda l, j: (l, 0, j)),
            pl.BlockSpec((1, 1, tn), lambda l, j: (l, 0, j)),
        ],
        out_specs=pl.BlockSpec((1, rows, tn), lambda l, j: (l, 0, j)),
        out_shape=jax.ShapeDtypeStruct((depth, rows, n), F32),
        compiler_params=_cparams(("arbitrary", "arbitrary")),
        name="adaln_mod",
    )(cp, ada_w, ada_b.reshape(depth, 1, n))
    return out[:, :b].reshape(depth, b, 6, d)


W_A, W_B, W_C, W_D, W_G, W_CMP = 768, 768, 768, 768, 768, 128
W_OFFS = np.cumsum((0, W_A, W_B, W_C, W_D, W_G, W_CMP))
W_TOTAL = int(W_OFFS[-1])


def _prep_w_in(w):
    offs = np.cumsum((0,) + IN_SPLITS)
    col = lambda i: w[:, int(offs[i]):int(offs[i + 1])]
    dup = lambda t: jnp.concatenate([t, t], axis=1)
    kb, vb = col(4), col(5)
    gates = col(17)
    gexp = [jnp.repeat(gates[:, h * 3 + r:h * 3 + r + 1], HEAD_DIM, axis=1)
            for r in range(3) for h in range(HEADS)]
    qs64 = LOG2E * HEAD_DIM ** -0.5
    parts = [
        col(0) * (LOG2E * DIFF_DK ** -0.5), col(1), col(2),
        col(3) * qs64,
        dup(kb[:, :HEAD_DIM]), dup(kb[:, HEAD_DIM:]),
        dup(vb[:, :HEAD_DIM]), dup(vb[:, HEAD_DIM:]),
        col(6) * qs64, col(7), col(8),
        col(10) * qs64, dup(col(13)), dup(col(14)), dup(col(15)), dup(col(16)),
    ] + gexp + [col(11), col(12)]
    wp = jnp.concatenate(parts, axis=1).astype(BF16)
    wf = jnp.zeros((8, w.shape[0]), F32).at[:HEADS].set(col(9).T).astype(BF16)
    return wp, wf


def _lane_cumsum(y):
    n = y.shape[1]
    lane = lax.broadcasted_iota(jnp.int32, y.shape, 1)
    sh = 1
    while sh < n:
        y = y + jnp.where(lane >= sh, pltpu.roll(y, sh, 1), 0.0)
        sh *= 2
    return y


def _inproj_kernel(x_ref, mod_ref, g_ref, w_ref, wf_ref, bf_ref,
                   oa, ob, oc, od, og, ocmp, ofc, carry):
    i = pl.program_id(1)
    x = x_ref[0]
    ms = jnp.mean(x * x, axis=-1, keepdims=True)
    y = x * lax.rsqrt(ms + EPS) * g_ref[...]
    h = y * (1.0 + mod_ref[1:2, :]) + mod_ref[0:1, :]
    hb = h.astype(BF16)
    o = W_OFFS
    oa[0] = _dot(hb, w_ref[:, o[0]:o[1]]).astype(BF16)
    ob[0] = _dot(hb, w_ref[:, o[1]:o[2]]).astype(BF16)
    oc[0] = _dot(hb, w_ref[:, o[2]:o[3]]).astype(BF16)
    od[0] = _dot(hb, w_ref[:, o[3]:o[4]]).astype(BF16)
    og[0] = jax.nn.sigmoid(_dot(hb, w_ref[:, o[4]:o[5]])).astype(BF16)
    ocmp[0] = _dot(hb, w_ref[:, o[5]:o[6]])
    fl = _dot_nt(wf_ref[...], hb) + bf_ref[:, 0:1]
    ls = jnp.minimum(fl, 0.0) - jnp.log1p(jnp.exp(-jnp.abs(fl)))

    @pl.when(i == 0)
    def _():
        carry[...] = jnp.zeros_like(carry)

    cs = _lane_cumsum(ls) + carry[:, 0:1]
    ofc[0] = cs
    carry[...] = jnp.broadcast_to(cs[:, cs.shape[1] - 1:], carry.shape)


def _in_projection(x, mod_l, g, wp, wf, bf, tm):
    b, s, d = x.shape
    bf8 = jnp.zeros((8, LANES), F32).at[:HEADS, :].set(bf[:, None])
    outs = pl.pallas_call(
        _inproj_kernel,
        grid=(b, s // tm),
        in_specs=[
            pl.BlockSpec((1, tm, d), lambda bi, i: (bi, i, 0)),
            pl.BlockSpec((None, 6, d), lambda bi, i: (bi, 0, 0)),
            pl.BlockSpec((1, d), lambda bi, i: (0, 0)),
            pl.BlockSpec((d, W_TOTAL), lambda bi, i: (0, 0)),
            pl.BlockSpec((8, d), lambda bi, i: (0, 0)),
            pl.BlockSpec((8, LANES), lambda bi, i: (0, 0)),
        ],
        out_specs=[
            pl.BlockSpec((1, tm, W_A), lambda bi, i: (bi, i, 0)),
            pl.BlockSpec((1, tm, W_B), lambda bi, i: (bi, i, 0)),
            pl.BlockSpec((1, tm, W_C), lambda bi, i: (bi, i, 0)),
            pl.BlockSpec((1, tm, W_D), lambda bi, i: (bi, i, 0)),
            pl.BlockSpec((1, tm, W_G), lambda bi, i: (bi, i, 0)),
            pl.BlockSpec((1, tm, W_CMP), lambda bi, i: (bi, i, 0)),
            pl.BlockSpec((1, 8, tm), lambda bi, i: (bi, 0, i)),
        ],
        out_shape=[
            jax.ShapeDtypeStruct((b, s, W_A), BF16),
            jax.ShapeDtypeStruct((b, s, W_B), BF16),
            jax.ShapeDtypeStruct((b, s, W_C), BF16),
            jax.ShapeDtypeStruct((b, s, W_D), BF16),
            jax.ShapeDtypeStruct((b, s, W_G), BF16),
            jax.ShapeDtypeStruct((b, s, W_CMP), F32),
            jax.ShapeDtypeStruct((b, 8, s), F32),
        ],
        scratch_shapes=[pltpu.VMEM((8, LANES), F32)],
        compiler_params=_cparams(("arbitrary", "arbitrary")),
        name="in_projection",
    )(x, mod_l, g.reshape(1, d), wp, wf, bf8)
    return outs


def _flash_kernel(kind, t, layer_idx, *refs):
    it = iter(refs)
    q_ref, k_ref, v_ref = next(it), next(it), next(it)
    fc_ref = next(it) if kind == "C" else None
    un_ref, use_ref = (next(it), next(it)) if kind == "Dsel" else (None, None)
    lam_ref, sg_ref = (next(it), next(it)) if kind == "A" else (None, None)
    o_ref, m_scr, l_scr, acc_scr = next(it), next(it), next(it), next(it)

    p = pl.program_id(1)
    i = pl.program_id(2)
    nstream = 4 if kind == "A" else 2
    width = LANES // nstream
    mixer = {"A": 0, "Dsel": 2}.get(kind)

    q2 = q_ref[0]
    lane = lax.broadcasted_iota(jnp.int32, (t, LANES), 1)
    zero = jnp.zeros_like(q2)
    qms = [jnp.where((lane >= s * width) & (lane < (s + 1) * width), q2, zero)
           for s in range(nstream)]
    if kind == "Dsel":
        qms = [jnp.concatenate([qm, un_ref[0]], axis=1) for qm in qms]

    def head_of(s):
        return s // 2 if kind == "A" else s

    slopes = None
    if mixer is not None:
        sl = SLOPES[mixer] * LOG2E
        slopes = [jnp.where(p == 0, float(sl[head_of(s)]), float(sl[2 + head_of(s)]))
                  for s in range(nstream)]

    for s in range(nstream):
        m_scr[s] = jnp.full((t, LANES), NEG_INF, F32)
        l_scr[s] = jnp.zeros((t, LANES), F32)
        acc_scr[s] = jnp.zeros((t, LANES), F32)

    def tile(first_key, nk, diag):
        start = pl.multiple_of(first_key, t)
        k2 = k_ref[0, pl.ds(start, nk), :]
        v2 = v_ref[0, pl.ds(start, nk), :]
        col = lax.broadcasted_iota(jnp.int32, (1, nk), 1)
        rel = (start - i * t + col).astype(F32)
        if diag:
            mask = (lax.broadcasted_iota(jnp.int32, (t, t), 0)
                    >= lax.broadcasted_iota(jnp.int32, (t, t), 1))
        for s in range(nstream):
            sc = _dot_nt(qms[s], k2)
            if slopes is not None:
                sc = sc + slopes[s] * rel
            if kind == "C":
                sc = sc - LOG2E * fc_ref[0, pl.ds(2 * p + s, 1), pl.ds(start, nk)]
            if diag:
                sc = jnp.where(mask, sc, NEG_INF)
            m_prev = m_scr[s]
            m_next = jnp.maximum(m_prev, jnp.max(sc, axis=1, keepdims=True))
            alpha = jnp.exp2(m_prev - m_next)
            pexp = jnp.exp2(sc - jnp.tile(m_next, (1, nk // LANES)))
            l_scr[s] = alpha * l_scr[s] + jnp.sum(pexp, axis=1, keepdims=True)
            acc_scr[s] = alpha * acc_scr[s] + _dot(pexp.astype(BF16), v2)
            m_scr[s] = m_next

    tile(i * t, t, True)

    if kind == "Dsel":
        nq = pl.num_programs(2)
        base = (pl.program_id(0) * nq + i) * nq

        def body(j, carry):
            @pl.when(use_ref[base + j] != 0)
            def _():
                tile(j * t, t, False)
            return carry

        lax.fori_loop(0, i, body, 0)
    else:
        def body(j, carry):
            tile(j * (2 * t), 2 * t, False)
            return carry

        lax.fori_loop(0, i // 2, body, 0)

        @pl.when(i % 2 == 1)
        def _():
            tile((i - 1) * t, t, False)

    outs = [acc_scr[s] / l_scr[s] for s in range(nstream)]
    if kind == "A":
        lam_init = 0.8 - 0.6 * math.exp(-0.3 * layer_idx)
        t1 = jnp.sum(lam_ref[0:1, :] * lam_ref[1:2, :], axis=1, keepdims=True)
        t2 = jnp.sum(lam_ref[2:3, :] * lam_ref[3:4, :], axis=1, keepdims=True)
        lam = jnp.exp(t1) - jnp.exp(t2) + lam_init
        d0 = outs[0] - lam * outs[1]
        d1 = outs[2] - lam * outs[3]
        lo_half = lane < HEAD_DIM
        o = jnp.where(lo_half, d0, d1)
        sq = o * o
        ss_lo = jnp.sum(jnp.where(lo_half, sq, 0.0), axis=1, keepdims=True)
        ss_hi = jnp.sum(jnp.where(lo_half, 0.0, sq), axis=1, keepdims=True)
        ms = jnp.where(lo_half, ss_lo, ss_hi) * (1.0 / HEAD_DIM)
        o = o * lax.rsqrt(ms + EPS) * sg_ref[...] * (1.0 - lam_init)
    else:
        o = jnp.where(lane < HEAD_DIM, outs[0], outs[1])
    o_ref[0] = o.astype(o_ref.dtype)


def _flash(kind, t, layer_idx, q_arr, qoff, k_arr, koff, v_arr, voff, extras):
    b, s, _ = q_arr.shape
    nstream = 4 if kind == "A" else 2
    shared_kv = kind == "Dsel"
    kw = k_arr.shape[-1] if shared_kv else LANES
    kv_idx = (lambda off: (lambda bi, p, i: (bi, 0, off))) if shared_kv else \
             (lambda off: (lambda bi, p, i: (bi, 0, off + p)))
    in_specs = [
        pl.BlockSpec((1, t, LANES), lambda bi, p, i: (bi, i, qoff + p)),
        pl.BlockSpec((1, s, kw), kv_idx(koff)),
        pl.BlockSpec((1, s, LANES), kv_idx(voff)),
    ]
    args = [q_arr, k_arr, v_arr]
    if kind == "C":
        fcum, = extras
        in_specs.append(pl.BlockSpec((1, 8, s), lambda bi, p, i: (bi, 0, 0)))
        args.append(fcum)
    elif kind == "Dsel":
        unsel, = extras
        nsel = unsel.shape[-1]
        nq, per = s // t, t // SEL_BLOCK
        used = (unsel.reshape(b, nq, t, nq, per) == 0).any(axis=(2, 4)).astype(jnp.int32).reshape(-1)
        in_specs.append(pl.BlockSpec((1, t, nsel), lambda bi, p, i: (bi, i, 0)))
        in_specs.append(pl.BlockSpec(memory_space=pltpu.SMEM))
        args += [unsel, used]
    elif kind == "A":
        lam8, sg = extras
        in_specs.append(pl.BlockSpec((8, LANES), lambda bi, p, i: (0, 0)))
        in_specs.append(pl.BlockSpec((1, LANES), lambda bi, p, i: (0, 0)))
        args += [lam8, sg]
    return pl.pallas_call(
        functools.partial(_flash_kernel, kind, t, layer_idx),
        grid=(b, 2, s // t),
        in_specs=in_specs,
        out_specs=pl.BlockSpec((1, t, LANES), lambda bi, p, i: (bi, i, p)),
        out_shape=jax.ShapeDtypeStruct((b, s, 2 * LANES), BF16),
        scratch_shapes=[pltpu.VMEM((nstream, t, LANES), F32)] * 3,
        compiler_params=_cparams(("arbitrary", "arbitrary", "arbitrary")),
        name="flash_" + kind,
    )(*args)


def _window_kernel(kind, t, wpad, *refs):
    it = iter(refs)
    q_ref, k_ref, v_ref = next(it), next(it), next(it)
    sink_ref = next(it) if kind == "B" else None
    o_ref = next(it)
    p = pl.program_id(1)
    i = pl.program_id(2)
    window = SWA_WINDOW if kind == "B" else NSA_WINDOW
    sl = SLOPES[1 if kind == "B" else 2] * LOG2E
    nk = wpad + t
    start = pl.multiple_of(jnp.maximum(i * t - wpad, 0), LANES)
    k2 = k_ref[0, pl.ds(start, nk), :]
    v2 = v_ref[0, pl.ds(start, nk), :]
    q2 = q_ref[0]
    lane = lax.broadcasted_iota(jnp.int32, (t, LANES), 1)
    zero = jnp.zeros_like(q2)
    dist = ((i * t - start) + lax.broadcasted_iota(jnp.int32, (t, nk), 0)
            - lax.broadcasted_iota(jnp.int32, (t, nk), 1))
    valid = (dist >= 0) & (dist < window)
    rel = (start - i * t + lax.broadcasted_iota(jnp.int32, (1, nk), 1)).astype(F32)
    qrow = lax.broadcasted_iota(jnp.int32, (t, 1), 0).astype(F32)
    outs = []
    for s in range(2):
        slope = jnp.where(p == 0, float(sl[s]), float(sl[2 + s]))
        qm = jnp.where((lane >= s * HEAD_DIM) & (lane < (s + 1) * HEAD_DIM), q2, zero)
        sc = jnp.where(valid, _dot_nt(qm, k2) + slope * rel, NEG_INF)
        m = jnp.max(sc, axis=1, keepdims=True)
        if kind == "B":
            sink = LOG2E * sink_ref[2 * p + s] + slope * qrow
            m = jnp.maximum(m, sink)
        pexp = jnp.exp2(sc - m)
        l = jnp.sum(pexp, axis=1, keepdims=True)
        if kind == "B":
            l = l + jnp.exp2(sink - m)
        outs.append(_dot(pexp.astype(BF16), v2) / l)
    o_ref[0] = jnp.where(lane < HEAD_DIM, outs[0], outs[1]).astype(o_ref.dtype)


def _window(kind, t, wpad, q_arr, qoff, k_arr, koff, v_arr, voff, extras):
    b, s, _ = q_arr.shape
    shared_kv = kind == "Dwin"
    kv_idx = (lambda off: (lambda bi, p, i: (bi, 0, off))) if shared_kv else \
             (lambda off: (lambda bi, p, i: (bi, 0, off + p)))
    in_specs = [
        pl.BlockSpec((1, t, LANES), lambda bi, p, i: (bi, i, qoff + p)),
        pl.BlockSpec((1, s, LANES), kv_idx(koff)),
        pl.BlockSpec((1, s, LANES), kv_idx(voff)),
    ]
    args = [q_arr, k_arr, v_arr]
    if kind == "B":
        in_specs.append(pl.BlockSpec(memory_space=pltpu.SMEM))
        args.append(extras[0])
    return pl.pallas_call(
        functools.partial(_window_kernel, kind, t, wpad),
        grid=(b, 2, s // t),
        in_specs=in_specs,
        out_specs=pl.BlockSpec((1, t, LANES), lambda bi, p, i: (bi, i, p)),
        out_shape=jax.ShapeDtypeStruct((b, s, 2 * LANES), BF16),
        compiler_params=_cparams(("arbitrary", "arbitrary", "arbitrary")),
        name="window_" + kind,
    )(*args)


def _gelu_tanh(x):
    return 0.5 * x * (1.0 + jnp.tanh(math.sqrt(2.0 / math.pi) * (x + 0.044715 * x * x * x)))


def _compress_kernel(t_ref, w1_ref, pos_ref, b1_ref, w2_ref, o_ref):
    tr = t_ref[0, 0]
    half = tr.shape[1]
    n = tr.shape[0]
    w1 = w1_ref[0]
    u = _dot_f32(tr, w1[:half])
    v = _dot_f32(tr, w1[half:])
    cpos = _dot_f32(pos_ref[0], w1)[0:1] + b1_ref[0]
    hid = u + pltpu.roll(v, n - 1, 0) + cpos
    o_ref[0, 0] = _dot_f32(_gelu_tanh(hid), w2_ref[0])


def _compress(kv_r, w1, pos8, b1, w2d):
    two, b, n, dd = kv_r.shape
    return pl.pallas_call(
        _compress_kernel,
        grid=(two, b),
        in_specs=[
            pl.BlockSpec((1, 1, n, dd), lambda c, bi: (c, bi, 0, 0)),
            pl.BlockSpec((1, 2 * dd, CMP_HIDDEN), lambda c, bi: (c, 0, 0)),
            pl.BlockSpec((1, 8, 2 * dd), lambda c, bi: (c, 0, 0)),
            pl.BlockSpec((1, 1, CMP_HIDDEN), lambda c, bi: (c, 0, 0)),
            pl.BlockSpec((1, CMP_HIDDEN, LANES), lambda c, bi: (c, 0, 0)),
        ],
        out_specs=pl.BlockSpec((1, 1, n, LANES), lambda c, bi: (c, bi, 0, 0)),
        out_shape=jax.ShapeDtypeStruct((two, b, n, LANES), F32),
        compiler_params=_cparams(("arbitrary", "arbitrary")),
        name="nsa_compress",
    )(kv_r, w1, pos8, b1, w2d)


def _cmp_kernel(t, q_ref, kc_ref, vc_ref, cov_ref, o_ref, un_ref):
    i = pl.program_id(1)
    ncp = kc_ref.shape[2]
    nsel = cov_ref.shape[0]
    kc = kc_ref[0, 0]
    vc = vc_ref[0, 0]
    kch, kcl = _split_bf16(kc)
    vcb = vc.astype(BF16)
    lane = lax.broadcasted_iota(jnp.int32, (t, LANES), 1)
    tq = i * t + lax.broadcasted_iota(jnp.int32, (t, ncp), 0)
    nidx = lax.broadcasted_iota(jnp.int32, (t, ncp), 1)
    cmp_end = nidx * CMP_STRIDE + (CMP_BLOCK - 1)
    valid = (tq >= cmp_end) & (nidx < ncp - 1)
    end_row = cmp_end[0:1, :].astype(F32)
    psum = jnp.zeros((t, ncp), F32)
    outs = []
    for pair in range(2):
        q2 = q_ref[0, :, pair * LANES:(pair + 1) * LANES]
        for g in range(2):
            h = 2 * pair + g
            qm = jnp.where((lane >= g * HEAD_DIM) & (lane < (g + 1) * HEAD_DIM), q2, jnp.zeros_like(q2))
            sc = _dot_nt(qm, kch) + _dot_nt(qm, kcl) + float(SLOPES[2][h] * LOG2E) * end_row
            sc = jnp.where(valid, sc, NEG_INF)
            m = jnp.max(sc, axis=1, keepdims=True)
            pe = jnp.exp2(sc - m)
            l = jnp.sum(pe, axis=1, keepdims=True)
            pc = pe * jnp.where(m > 0.5 * NEG_INF, 1.0 / l, 0.0)
            psum = psum + pc
            outs.append(_dot(pc.astype(BF16), vcb))
    lo_half = lane < HEAD_DIM
    o_ref[0, :, 0:LANES] = jnp.where(lo_half, outs[0], outs[1]).astype(o_ref.dtype)
    o_ref[0, :, LANES:2 * LANES] = jnp.where(lo_half, outs[2], outs[3]).astype(o_ref.dtype)
    ph, plo = _split_bf16(psum)
    cov = cov_ref[...]
    imp = _dot_nt(cov, ph) + _dot_nt(cov, plo)
    blk = lax.broadcasted_iota(jnp.int32, (nsel, t), 0)
    cur = (i * t + lax.broadcasted_iota(jnp.int32, (nsel, t), 1)) // SEL_BLOCK
    forced = (blk == 0) | (blk == cur) | (blk == cur - 1)
    score = jnp.where(forced, FORCE_SCORE, jnp.where(blk <= cur, imp, -1.0))
    unsel = jnp.ones((nsel, t), F32)
    for _ in range(min(SEL_TOPK, nsel)):
        mx = jnp.max(score, axis=0, keepdims=True)
        idx = jnp.min(jnp.where(score == mx, blk, nsel), axis=0, keepdims=True)
        hit = blk == idx
        unsel = jnp.where(hit, 0.0, unsel)
        score = jnp.where(hit, -3.0, score)
    un_ref[0] = unsel.T.astype(un_ref.dtype)


def _cmp_attention(t, qd, kvc, cover_t):
    b, s, _ = qd.shape
    ncp = kvc.shape[2]
    nsel = cover_t.shape[0]
    return pl.pallas_call(
        functools.partial(_cmp_kernel, t),
        grid=(b, s // t),
        in_specs=[
            pl.BlockSpec((1, t, 2 * LANES), lambda bi, i: (bi, i, 0)),
            pl.BlockSpec((1, 1, ncp, LANES), lambda bi, i: (0, bi, 0, 0)),
            pl.BlockSpec((1, 1, ncp, LANES), lambda bi, i: (1, bi, 0, 0)),
            pl.BlockSpec((nsel, ncp), lambda bi, i: (0, 0)),
        ],
        out_specs=[
            pl.BlockSpec((1, t, 2 * LANES), lambda bi, i: (bi, i, 0)),
            pl.BlockSpec((1, t, nsel), lambda bi, i: (bi, i, 0)),
        ],
        out_shape=[
            jax.ShapeDtypeStruct((b, s, 2 * LANES), BF16),
            jax.ShapeDtypeStruct((b, s, nsel), BF16),
        ],
        compiler_params=_cparams(("arbitrary", "arbitrary")),
        name="nsa_cmp_select",
    )(qd, kvc, kvc, cover_t)


def _cover_matrix(s):
    ncp, nsel = s // CMP_STRIDE, s // SEL_BLOCK
    cs = np.arange(ncp)[:, None] * CMP_STRIDE
    ss = np.arange(nsel)[None, :] * SEL_BLOCK
    cov = np.clip(np.minimum(cs + CMP_BLOCK, ss + SEL_BLOCK) - np.maximum(cs, ss), 0, None) / CMP_BLOCK
    cov[ncp - 1] = 0.0
    return jnp.asarray(cov.T, BF16)


def _expand_neg(s):
    nsel = s // SEL_BLOCK
    e = (np.arange(s)[:, None] // SEL_BLOCK) == np.arange(nsel)[None, :]
    return jnp.asarray(np.where(e, -MASK_BIG, 0.0), BF16)


def _outproj_kernel(ya, yb, yc, ocmp, osel, owin, gt, x_ref, mod_ref, wo_ref, g_ref,
                    rwt_ref, rb_ref, xo_ref, h_ref, ri_ref, rg_ref, cnt_ref, carry):
    first = (pl.program_id(0) == 0) & (pl.program_id(1) == 0)
    gw = GROUP_WIDTH
    g = gt[0].astype(F32)
    yd = (g[:, 0:gw] * ocmp[0].astype(F32) + g[:, gw:2 * gw] * osel[0].astype(F32)
          + g[:, 2 * gw:3 * gw] * owin[0].astype(F32))
    ycat = jnp.concatenate([ya[0], yb[0], yc[0], yd.astype(BF16)], axis=1)
    y = _dot(ycat, wo_ref[...])
    x = x_ref[0] + mod_ref[2:3, :] * y
    xo_ref[0] = x
    ms = jnp.mean(x * x, axis=-1, keepdims=True)
    h = x * lax.rsqrt(ms + EPS) * g_ref[...]
    h = h * (1.0 + mod_ref[4:5, :]) + mod_ref[3:4, :]
    h_ref[0] = h
    tm = h.shape[0]
    ng = LANES // EXPERTS_PER_GROUP
    big = 4 * LANES
    logits = _dot_f32(rwt_ref[...], h, nt=True)
    eidx = lax.broadcasted_iota(jnp.int32, (LANES, tm), 0)
    aff = jax.nn.sigmoid(logits)
    sel = jnp.where(eidx < N_EXPERTS, aff + rb_ref[:, 0:1], NEG_INF)
    sel3 = sel.reshape(ng, EXPERTS_PER_GROUP, tm)
    e3 = eidx.reshape(ng, EXPERTS_PER_GROUP, tm)
    g1 = jnp.max(sel3, axis=1, keepdims=True)
    i1 = jnp.min(jnp.where(sel3 == g1, e3, big), axis=1, keepdims=True)
    sel_b = jnp.where(e3 == i1, NEG_INF, sel3)
    g2 = jnp.max(sel_b, axis=1, keepdims=True)
    i2 = jnp.min(jnp.where(sel_b == g2, e3, big), axis=1, keepdims=True)
    gs = g1 + g2
    gidx = lax.broadcasted_iota(jnp.int32, (ng, 1, tm), 0)
    gm = jnp.max(gs, axis=0, keepdims=True)
    best = gidx == jnp.min(jnp.where(gs == gm, gidx, big), axis=0, keepdims=True)
    e1 = jnp.min(jnp.where(best, i1, big), axis=0)
    e2 = jnp.min(jnp.where(best, i2, big), axis=0)
    oh1 = eidx == e1
    oh2 = eidx == e2
    a1 = jnp.sum(jnp.where(oh1, aff, 0.0), axis=0, keepdims=True)
    a2 = jnp.sum(jnp.where(oh2, aff, 0.0), axis=0, keepdims=True)
    inv = 1.0 / (a1 + a2)
    @pl.when(first)
    def _():
        carry[...] = jnp.zeros_like(carry)

    ohs = jnp.where(oh1 | oh2, 1.0, 0.0)
    rr = lax.broadcasted_iota(jnp.int32, (tm, tm), 0)
    cc = lax.broadcasted_iota(jnp.int32, (tm, tm), 1)
    earlier = jnp.where(rr < cc, 1.0, 0.0).astype(BF16)
    before = _dot(ohs.astype(BF16), earlier) + carry[:, 0:1]
    r1 = jnp.sum(jnp.where(oh1, before, 0.0), axis=0, keepdims=True)
    r2 = jnp.sum(jnp.where(oh2, before, 0.0), axis=0, keepdims=True)
    total = carry[:, 0:1] + jnp.sum(ohs, axis=1, keepdims=True)
    carry[...] = jnp.broadcast_to(total, carry.shape)
    cnt_ref[...] = jnp.broadcast_to(total, cnt_ref.shape).astype(jnp.int32)
    row = lax.broadcasted_iota(jnp.int32, (8, tm), 0)
    ri_ref[0] = jnp.where(row == 0, e1, jnp.where(row == 1, e2, jnp.where(
        row == 2, r1.astype(jnp.int32), jnp.where(row == 3, r2.astype(jnp.int32), 0))))
    rg_ref[0] = jnp.where(row == 0, a1 * inv, jnp.where(row == 1, a2 * inv, 0.0))


def _out_projection(ya, yb, yc, ocmp, osel, owin, gates, x, mod_l, wo, g, rwt, rb, tm):
    b, s, d = x.shape
    gw = GROUP_WIDTH
    tok = lambda w: pl.BlockSpec((1, tm, w), lambda bi, i: (bi, i, 0))
    full = lambda shp: pl.BlockSpec(shp, lambda bi, i: (0,) * len(shp))
    rowblk = pl.BlockSpec((1, 8, tm), lambda bi, i: (bi, 0, i))
    return pl.pallas_call(
        _outproj_kernel,
        grid=(b, s // tm),
        in_specs=[tok(gw)] * 6 + [tok(3 * gw), tok(d),
                                  pl.BlockSpec((None, 6, d), lambda bi, i: (bi, 0, 0)),
                                  full((d, d)), full((1, d)), full((LANES, d)), full((LANES, 1))],
        out_specs=[tok(d), tok(d), rowblk, rowblk, full((LANES, LANES))],
        out_shape=[
            jax.ShapeDtypeStruct((b, s, d), F32),
            jax.ShapeDtypeStruct((b, s, d), F32),
            jax.ShapeDtypeStruct((b, 8, s), jnp.int32),
            jax.ShapeDtypeStruct((b, 8, s), F32),
            jax.ShapeDtypeStruct((LANES, LANES), jnp.int32),
        ],
        scratch_shapes=[pltpu.VMEM((LANES, LANES), F32)],
        compiler_params=_cparams(("arbitrary", "arbitrary")),
        name="out_projection_router",
    )(ya, yb, yc, ocmp, osel, owin, gates, x, mod_l, wo, g.reshape(1, d), rwt, rb)


def _dispatch_kernel(p0_ref, p1_ref, h_ref, xs_ref, sem):
    tm = h_ref.shape[0]

    def issue(r, c):
        for pos_ref in (p0_ref, p1_ref):
            pltpu.make_async_copy(h_ref.at[pl.ds(r, 1), :],
                                  xs_ref.at[pl.ds(pos_ref[r], 1), :], sem).start()
        return c

    lax.fori_loop(0, tm, issue, 0, unroll=4)
    for _ in range(2):
        pltpu.make_async_copy(h_ref, xs_ref.at[pl.ds(0, tm), :], sem).wait()


def _dispatch(h, pos0, pos1, tm):
    t, d = h.shape
    return pl.pallas_call(
        _dispatch_kernel,
        grid=(t // tm,),
        in_specs=[
            pl.BlockSpec((tm,), lambda i: (i,), memory_space=pltpu.SMEM),
            pl.BlockSpec((tm,), lambda i: (i,), memory_space=pltpu.SMEM),
            pl.BlockSpec((tm, d), lambda i: (i, 0)),
        ],
        out_specs=pl.BlockSpec(memory_space=pl.ANY),
        out_shape=jax.ShapeDtypeStruct((2 * t, d), F32),
        scratch_shapes=[pltpu.SemaphoreType.DMA(())],
        compiler_params=_cparams(("arbitrary",)),
        name="moe_dispatch",
    )(pos0, pos1, h)


def _expert_kernel(vb_ref, ve_ref, vlo_ref, vhi_ref, nv_ref,
                   xs_ref, wg_ref, wu_ref, wd_ref, ys_ref, wgb, wub, wdb):
    v = pl.program_id(0)
    rows = xs_ref.shape[0]
    prev = jnp.maximum(v - 1, 0)
    new_expert = (v == 0) | (ve_ref[v] != ve_ref[prev])
    new_block = (v == 0) | (vb_ref[v] != vb_ref[prev])

    @pl.when(v < nv_ref[0])
    def _():
        @pl.when(new_expert)
        def _():
            wgb[...] = wg_ref[0, 0].astype(BF16)
            wub[...] = wu_ref[0, 0].astype(BF16)
            wdb[...] = wd_ref[0, 0].astype(BF16)

        x = xs_ref[...].astype(BF16)
        hg = _dot(x, wgb[...])
        hu = _dot(x, wub[...])
        hm = (hg * jax.nn.sigmoid(hg) * hu).astype(BF16)
        y = _dot(hm, wdb[...])
        r = vb_ref[v] * rows + lax.broadcasted_iota(jnp.int32, (rows, 1), 0)
        mine = (r >= vlo_ref[v]) & (r < vhi_ref[v])

        @pl.when(new_block)
        def _():
            ys_ref[...] = jnp.where(mine, y, 0.0)

        @pl.when(jnp.logical_not(new_block))
        def _():
            ys_ref[...] = jnp.where(mine, y, ys_ref[...])


def _experts(xs, meta, layer, wg, wu, wd, rows, nvmax):
    n, d = xs.shape
    de = wg.shape[3]
    grid_spec = pltpu.PrefetchScalarGridSpec(
        num_scalar_prefetch=5,
        grid=(nvmax,),
        in_specs=[
            pl.BlockSpec((rows, d), lambda v, vb, ve, lo, hi, nv: (vb[v], 0)),
            pl.BlockSpec((1, 1, d, de), lambda v, vb, ve, lo, hi, nv: (layer, ve[v], 0, 0)),
            pl.BlockSpec((1, 1, d, de), lambda v, vb, ve, lo, hi, nv: (layer, ve[v], 0, 0)),
            pl.BlockSpec((1, 1, de, d), lambda v, vb, ve, lo, hi, nv: (layer, ve[v], 0, 0)),
        ],
        out_specs=pl.BlockSpec((rows, d), lambda v, vb, ve, lo, hi, nv: (vb[v], 0)),
        scratch_shapes=[pltpu.VMEM((d, de), BF16), pltpu.VMEM((d, de), BF16), pltpu.VMEM((de, d), BF16)],
    )
    return pl.pallas_call(
        _expert_kernel,
        grid_spec=grid_spec,
        out_shape=jax.ShapeDtypeStruct((n, d), F32),
        compiler_params=_cparams(("arbitrary",)),
        name="moe_experts",
    )(*meta, xs, wg, wu, wd)


def _combine_kernel(final, p0_ref, p1_ref, x_ref, rg_ref, mod_ref, g_ref, ys_ref, o_ref, buf, sem):
    tm = x_ref.shape[0]

    def issue(r, c):
        for k, pos_ref in enumerate((p0_ref, p1_ref)):
            pltpu.make_async_copy(ys_ref.at[pl.ds(pos_ref[r], 1), :],
                                  buf.at[k, pl.ds(r, 1), :], sem).start()
        return c

    lax.fori_loop(0, tm, issue, 0, unroll=4)
    for k in range(2):
        pltpu.make_async_copy(ys_ref.at[pl.ds(0, tm), :], buf.at[k], sem).wait()
    rg = rg_ref[...]
    y = rg[:, 0:1] * buf[0] + rg[:, 1:2] * buf[1]
    x = x_ref[...] + mod_ref[5:6, :] * y
    if final:
        ms = jnp.mean(x * x, axis=-1, keepdims=True)
        x = x * lax.rsqrt(ms + EPS) * g_ref[...]
    o_ref[...] = x


def _combine(final, pos0, pos1, x, rg, mod_l, g, ys, tm):
    t, d = x.shape
    per_b = t // mod_l.shape[0] // tm
    return pl.pallas_call(
        functools.partial(_combine_kernel, final),
        grid=(t // tm,),
        in_specs=[
            pl.BlockSpec((tm,), lambda i: (i,), memory_space=pltpu.SMEM),
            pl.BlockSpec((tm,), lambda i: (i,), memory_space=pltpu.SMEM),
            pl.BlockSpec((tm, d), lambda i: (i, 0)),
            pl.BlockSpec((tm, 8), lambda i: (i, 0)),
            pl.BlockSpec((None, 6, d), lambda i: (i // per_b, 0, 0)),
            pl.BlockSpec((1, d), lambda i: (0, 0)),
            pl.BlockSpec(memory_space=pl.ANY),
        ],
        out_specs=pl.BlockSpec((tm, d), lambda i: (i, 0)),
        out_shape=jax.ShapeDtypeStruct((t, d), F32),
        scratch_shapes=[pltpu.VMEM((2, tm, d), F32), pltpu.SemaphoreType.DMA(())],
        compiler_params=_cparams(("arbitrary",)),
        name="moe_combine",
    )(pos0, pos1, x, rg, mod_l, g.reshape(1, d), ys)


def _visit_plan(counts, rows, nblocks):
    ne = counts.shape[0]
    nvmax = nblocks + ne - 1
    ends = jnp.cumsum(counts)
    offs = ends - counts
    b_lo = offs // rows
    b_hi = jnp.maximum(ends - 1, 0) // rows
    nvis = jnp.where(counts > 0, b_hi - b_lo + 1, 0)
    vend = jnp.cumsum(nvis)
    vstart = vend - nvis
    nv = vend[-1]
    v = jnp.minimum(jnp.arange(nvmax), nv - 1)
    e = jnp.sum((v[:, None] >= vend[None, :]).astype(jnp.int32), axis=1)
    onehot = e[:, None] == jnp.arange(ne)[None, :]
    pick = lambda a: jnp.sum(jnp.where(onehot, a[None, :], 0), axis=1)
    blk = pick(b_lo) + v - pick(vstart)
    lo = jnp.maximum(pick(offs), blk * rows)
    hi = jnp.minimum(pick(ends), (blk + 1) * rows)
    i32 = lambda a: a.astype(jnp.int32)
    return (i32(blk), i32(e), i32(lo), i32(hi), i32(nv).reshape(1)), offs, nvmax


def _tiles(s):
    tm = min(512, s)
    return dict(tm=tm, t_full=min(512, s), t_win=min(256, s), t_cmp=min(256, s),
                tm_moe=min(512, s), rows=256)


def _layer(l, x, mod_l, p, consts, final):
    b, s, d = x.shape
    tl = _tiles(s)
    wp, wf = _prep_w_in(p["w_in"])
    qa, qb, qc, qd, gates, kvcmp, fcum = _in_projection(
        x, mod_l, p["norm_attn_g"], wp, wf, p["fox_forget_b"], tl["tm"])
    lam8 = jnp.zeros((8, LANES), F32).at[:4, :DIFF_DK].set(
        jnp.stack([p["diff_lam_q1"], p["diff_lam_k1"], p["diff_lam_q2"], p["diff_lam_k2"]]))
    sg = jnp.tile(p["diff_subln_g"], 2).reshape(1, LANES)
    ya = _flash("A", tl["t_full"], l, qa, 0, qa, 2, qa, 4, (lam8, sg))
    yb = _window("B", tl["t_win"], SWA_WINDOW, qb, 0, qb, 2, qb, 4, (p["swa_sinks"],))
    yc = _flash("C", tl["t_full"], l, qc, 0, qc, 2, qc, 4, (fcum,))
    n16 = s // CMP_STRIDE
    kv_r = jnp.stack([kvcmp[..., :HEAD_DIM].reshape(b, n16, CMP_STRIDE * HEAD_DIM),
                      kvcmp[..., HEAD_DIM:].reshape(b, n16, CMP_STRIDE * HEAD_DIM)])
    pos8 = jnp.zeros((2, 8, CMP_BLOCK * HEAD_DIM), F32).at[:, 0].set(
        p["nsa_cmp_pos"].reshape(2, CMP_BLOCK * HEAD_DIM))
    w2d = jnp.concatenate([p["nsa_cmp_w2"], p["nsa_cmp_w2"]], axis=-1)
    kvc = _compress(kv_r, p["nsa_cmp_w1"], pos8, p["nsa_cmp_b1"].reshape(2, 1, CMP_HIDDEN), w2d)
    ocmp, unsel = _cmp_attention(tl["t_cmp"], qd, kvc, consts["cover"])
    ksel = jnp.concatenate([qd[..., 2 * LANES:3 * LANES],
                            jnp.broadcast_to(consts["eneg"][None], (b,) + consts["eneg"].shape)], axis=-1)
    osel = _flash("Dsel", tl["t_full"], l, qd, 0, ksel, 0, qd, 3, (unsel,))
    owin = _window("Dwin", tl["t_win"], NSA_WINDOW, qd, 0, qd, 4, qd, 5, ())
    rwt = jnp.zeros((LANES, d), F32).at[:N_EXPERTS].set(p["router_w"].T)
    rb = jnp.zeros((LANES, 1), F32).at[:N_EXPERTS, 0].set(p["router_b"])
    xm, h2, ri, rg, cnt = _out_projection(
        ya, yb, yc, ocmp, osel, owin, gates, x, mod_l, p["w_out"].astype(BF16),
        p["norm_moe_g"], rwt, rb, tl["tm"])
    t = b * s
    counts = cnt[:N_EXPERTS, 0]
    rows = tl["rows"]
    meta, offs, nvmax = _visit_plan(counts, rows, 2 * t // rows)
    onehot = ri[:, 0:2, :, None] == jnp.arange(N_EXPERTS)[None, None, None, :]
    pos = (jnp.sum(jnp.where(onehot, offs[None, None, None, :], 0), axis=-1) + ri[:, 2:4]).astype(jnp.int32)
    pos0, pos1 = pos[:, 0].reshape(t), pos[:, 1].reshape(t)
    rgt = rg.transpose(0, 2, 1).reshape(t, 8)
    xs = _dispatch(h2.reshape(t, d), pos0, pos1, tl["tm_moe"])
    ys = _experts(xs, meta, l, p["exp_w_gate"], p["exp_w_up"], p["exp_w_down"], rows, nvmax)
    out = _combine(final, pos0, pos1, xm.reshape(t, d), rgt, mod_l, p["norm_final_g"], ys, tl["tm_moe"])
    return out.reshape(b, s, d)


def kernel(x, c, ada_w, ada_b, norm_attn_g, norm_moe_g, norm_final_g, w_in, w_out, diff_lam_q1, diff_lam_k1, diff_lam_q2, diff_lam_k2, diff_subln_g, swa_sinks, fox_forget_b, nsa_cmp_pos, nsa_cmp_w1, nsa_cmp_b1, nsa_cmp_w2, router_w, router_b, exp_w_gate, exp_w_up, exp_w_down):
    depth = ada_w.shape[0]
    s = x.shape[1]
    mod = _modulation(c, ada_w, ada_b)
    consts = dict(cover=_cover_matrix(s), eneg=_expand_neg(s))
    for l in range(depth):
        p = dict(
            norm_attn_g=norm_attn_g[l], norm_moe_g=norm_moe_g[l], norm_final_g=norm_final_g,
            w_in=w_in[l], w_out=w_out[l],
            diff_lam_q1=diff_lam_q1[l], diff_lam_k1=diff_lam_k1[l],
            diff_lam_q2=diff_lam_q2[l], diff_lam_k2=diff_lam_k2[l], diff_subln_g=diff_subln_g[l],
            swa_sinks=swa_sinks[l], fox_forget_b=fox_forget_b[l],
            nsa_cmp_pos=nsa_cmp_pos[l], nsa_cmp_w1=nsa_cmp_w1[l], nsa_cmp_b1=nsa_cmp_b1[l],
            nsa_cmp_w2=nsa_cmp_w2[l], router_w=router_w, router_b=router_b,
            exp_w_gate=exp_w_gate, exp_w_up=exp_w_up, exp_w_down=exp_w_down,
        )
        x = _layer(l, x, mod[l], p, consts, final=(l == depth - 1))
    return x
```

```python
import functools
import math

import numpy as np
import jax
import jax.numpy as jnp
from jax import lax
from jax.experimental import pallas as pl
from jax.experimental.pallas import tpu as pltpu

F32 = jnp.float32
BF16 = jnp.bfloat16

LANES = 128
HEAD_DIM = 64
N_MIXERS = 4
HEADS = 4
GROUP_WIDTH = HEADS * HEAD_DIM
DIFF_DK = HEAD_DIM // 2
SWA_WINDOW = 128
SWA_KV_HEADS = 2
CMP_BLOCK = 32
CMP_STRIDE = 16
CMP_HIDDEN = 2 * HEAD_DIM
SEL_BLOCK = 64
SEL_TOPK = 16
NSA_WINDOW = 512
FORCE_SCORE = 1e4
NEG_INF = -1e30
MASK_BIG = 2.0 ** 100
N_EXPERTS = 64
EXPERTS_PER_GROUP = 8
EPS = 1e-6
LOG2E = math.log2(math.e)
VMEM_LIMIT = 56 * 1024 * 1024
WIDE_TILES = 4

IN_SPLITS = (
    GROUP_WIDTH, GROUP_WIDTH, GROUP_WIDTH,
    GROUP_WIDTH, SWA_KV_HEADS * HEAD_DIM, SWA_KV_HEADS * HEAD_DIM,
    GROUP_WIDTH, GROUP_WIDTH, GROUP_WIDTH, HEADS,
    GROUP_WIDTH, HEAD_DIM, HEAD_DIM, HEAD_DIM, HEAD_DIM,
    HEAD_DIM, HEAD_DIM, 3 * HEADS,
)

_NT = (((1,), (1,)), ((), ()))


def _alibi_slopes():
    n = 3 * HEADS
    m = 2.0 ** (-8.0 * np.arange(1, n + 1) / n)
    return m.reshape(HEADS, 3).T


SLOPES = _alibi_slopes()


def _dot(a, b):
    return jnp.dot(a, b, preferred_element_type=F32)


def _dot_nt(a, b):
    return lax.dot_general(a, b, _NT, preferred_element_type=F32)


def _split_bf16(a):
    hi = a.astype(BF16)
    lo = (a - hi.astype(F32)).astype(BF16)
    return hi, lo


def _dot_f32(a, b, nt=False):
    d = _dot_nt if nt else _dot
    ah, al = _split_bf16(a)
    bh, bl = _split_bf16(b)
    return d(ah, bh) + (d(ah, bl) + d(al, bh))


def _cparams(sem):
    return pltpu.CompilerParams(dimension_semantics=sem, vmem_limit_bytes=VMEM_LIMIT)


def _mod_kernel(c_ref, w_ref, b_ref, o_ref):
    c = c_ref[...]
    cond = c * jax.nn.sigmoid(c)
    o_ref[0] = _dot_f32(cond, w_ref[0]) + b_ref[0]


def _modulation(c, ada_w, ada_b):
    depth, d, n = ada_w.shape
    b = c.shape[0]
    rows = 8
    cp = jnp.zeros((rows, d), F32).at[:b].set(c)
    tn = 1536
    out = pl.pallas_call(
        _mod_kernel,
        grid=(depth, n // tn),
        in_specs=[
            pl.BlockSpec((rows, d), lambda l, j: (0, 0)),
            pl.BlockSpec((1, d, tn), lambda l, j: (l, 0, j)),
            pl.BlockSpec((1, 1, tn), lambda l, j: (l, 0, j)),
        ],
        out_specs=pl.BlockSpec((1, rows, tn), lambda l, j: (l, 0, j)),
        out_shape=jax.ShapeDtypeStruct((depth, rows, n), F32),
        compiler_params=_cparams(("arbitrary", "arbitrary")),
        name="adaln_mod",
    )(cp, ada_w, ada_b.reshape(depth, 1, n))
    return out[:, :b].reshape(depth, b, 6, d)


W_A, W_B, W_C, W_D, W_G, W_CMP = 768, 768, 768, 768, 128, 128
W_OFFS = np.cumsum((0, W_A, W_B, W_C, W_D, W_G, W_CMP))
W_TOTAL = int(W_OFFS[-1])


def _prep_w_in(w):
    offs = np.cumsum((0,) + IN_SPLITS)
    col = lambda i: w[:, int(offs[i]):int(offs[i + 1])]
    dup = lambda t: jnp.concatenate([t, t], axis=1)
    kb, vb = col(4), col(5)
    gates = jnp.zeros((w.shape[0], W_G), F32).at[:, :3 * HEADS].set(col(17))
    qs64 = LOG2E * HEAD_DIM ** -0.5
    parts = [
        col(0) * (LOG2E * DIFF_DK ** -0.5), col(1), col(2),
        col(3) * qs64,
        dup(kb[:, :HEAD_DIM]), dup(kb[:, HEAD_DIM:]),
        dup(vb[:, :HEAD_DIM]), dup(vb[:, HEAD_DIM:]),
        col(6) * qs64, col(7), col(8),
        col(10) * qs64, dup(col(13)), dup(col(14)), dup(col(15)), dup(col(16)),
    ] + [gates, col(11), col(12)]
    wp = jnp.concatenate(parts, axis=1).astype(BF16)
    wf = jnp.zeros((8, w.shape[0]), F32).at[:HEADS].set(col(9).T).astype(BF16)
    return wp, wf


def _lane_cumsum(y):
    n = y.shape[1]
    lane = lax.broadcasted_iota(jnp.int32, y.shape, 1)
    sh = 1
    while sh < n:
        y = y + jnp.where(lane >= sh, pltpu.roll(y, sh, 1), 0.0)
        sh *= 2
    return y


def _inproj_kernel(x_ref, mod_ref, g_ref, w_ref, wf_ref, bf_ref,
                   oa, ob, oc, od, og, ocmp, ofc, carry):
    i = pl.program_id(1)
    x = x_ref[0]
    ms = jnp.mean(x * x, axis=-1, keepdims=True)
    y = x * lax.rsqrt(ms + EPS) * g_ref[...]
    h = y * (1.0 + mod_ref[1:2, :]) + mod_ref[0:1, :]
    hb = h.astype(BF16)
    o = W_OFFS
    oa[0] = _dot(hb, w_ref[:, o[0]:o[1]]).astype(BF16)
    ob[0] = _dot(hb, w_ref[:, o[1]:o[2]]).astype(BF16)
    oc[0] = _dot(hb, w_ref[:, o[2]:o[3]]).astype(BF16)
    od[0] = _dot(hb, w_ref[:, o[3]:o[4]]).astype(BF16)
    og[0] = jax.nn.sigmoid(_dot(hb, w_ref[:, o[4]:o[5]])).astype(BF16)
    ocmp[0] = _dot(hb, w_ref[:, o[5]:o[6]])
    fl = _dot_nt(wf_ref[...], hb) + bf_ref[:, 0:1]
    ls = jnp.minimum(fl, 0.0) - jnp.log1p(jnp.exp(-jnp.abs(fl)))

    @pl.when(i == 0)
    def _():
        carry[...] = jnp.zeros_like(carry)

    cs = _lane_cumsum(ls) + carry[:, 0:1]
    ofc[0] = cs
    carry[...] = jnp.broadcast_to(cs[:, cs.shape[1] - 1:], carry.shape)


def _in_projection(x, mod_l, g, wp, wf, bf, tm):
    b, s, d = x.shape
    bf8 = jnp.zeros((8, LANES), F32).at[:HEADS, :].set(bf[:, None])
    outs = pl.pallas_call(
        _inproj_kernel,
        grid=(b, s // tm),
        in_specs=[
            pl.BlockSpec((1, tm, d), lambda bi, i: (bi, i, 0)),
            pl.BlockSpec((None, 6, d), lambda bi, i: (bi, 0, 0)),
            pl.BlockSpec((1, d), lambda bi, i: (0, 0)),
            pl.BlockSpec((d, W_TOTAL), lambda bi, i: (0, 0)),
            pl.BlockSpec((8, d), lambda bi, i: (0, 0)),
            pl.BlockSpec((8, LANES), lambda bi, i: (0, 0)),
        ],
        out_specs=[
            pl.BlockSpec((1, tm, W_A), lambda bi, i: (bi, i, 0)),
            pl.BlockSpec((1, tm, W_B), lambda bi, i: (bi, i, 0)),
            pl.BlockSpec((1, tm, W_C), lambda bi, i: (bi, i, 0)),
            pl.BlockSpec((1, tm, W_D), lambda bi, i: (bi, i, 0)),
            pl.BlockSpec((1, tm, W_G), lambda bi, i: (bi, i, 0)),
            pl.BlockSpec((1, tm, W_CMP), lambda bi, i: (bi, i, 0)),
            pl.BlockSpec((1, 8, tm), lambda bi, i: (bi, 0, i)),
        ],
        out_shape=[
            jax.ShapeDtypeStruct((b, s, W_A), BF16),
            jax.ShapeDtypeStruct((b, s, W_B), BF16),
            jax.ShapeDtypeStruct((b, s, W_C), BF16),
            jax.ShapeDtypeStruct((b, s, W_D), BF16),
            jax.ShapeDtypeStruct((b, s, W_G), BF16),
            jax.ShapeDtypeStruct((b, s, W_CMP), F32),
            jax.ShapeDtypeStruct((b, 8, s), F32),
        ],
        scratch_shapes=[pltpu.VMEM((8, LANES), F32)],
        compiler_params=_cparams(("arbitrary", "arbitrary")),
        name="in_projection",
    )(x, mod_l, g.reshape(1, d), wp, wf, bf8)
    return outs


def _flash_kernel(kind, t, layer_idx, *refs):
    it = iter(refs)
    q_ref, k_ref, v_ref = next(it), next(it), next(it)
    fc_ref = next(it) if kind == "C" else None
    un_ref, use_ref = (next(it), next(it)) if kind == "Dsel" else (None, None)
    lam_ref, sg_ref = (next(it), next(it)) if kind == "A" else (None, None)
    o_ref, m_scr, acc_scr = next(it), next(it), next(it)

    p = pl.program_id(1)
    i = pl.program_id(2)
    nstream = 4 if kind == "A" else 2
    width = LANES // nstream
    mixer = {"A": 0, "Dsel": 2}.get(kind)

    q2 = q_ref[0]
    lane = lax.broadcasted_iota(jnp.int32, (t, LANES), 1)
    zero = jnp.zeros_like(q2)
    qms = [jnp.where((lane >= s * width) & (lane < (s + 1) * width), q2, zero)
           for s in range(nstream)]
    if kind == "Dsel":
        qms = [jnp.concatenate([qm, un_ref[0]], axis=1) for qm in qms]

    def head_of(s):
        return s // 2 if kind == "A" else s

    slopes = None
    if mixer is not None:
        sl = SLOPES[mixer] * LOG2E
        slopes = [jnp.where(p == 0, float(sl[head_of(s)]), float(sl[2 + head_of(s)]))
                  for s in range(nstream)]

    for s in range(nstream):
        m_scr[s] = jnp.full((t, LANES), NEG_INF, F32)
        acc_scr[s] = jnp.zeros((t, LANES), F32)

    def tile(first_key, nk, diag):
        start = pl.multiple_of(first_key, t)
        k2 = k_ref[0, pl.ds(start, nk), :]
        v2 = v_ref[0, pl.ds(start, nk), :]
        vlane = lax.broadcasted_iota(jnp.int32, (nk, LANES), 1)
        one = jnp.ones_like(v2)
        vhalf = [jnp.where(vlane < HEAD_DIM, v2, one), jnp.where(vlane < HEAD_DIM, one, v2)]
        col = lax.broadcasted_iota(jnp.int32, (1, nk), 1)
        rel = (start - i * t + col).astype(F32)
        if diag:
            mask = (lax.broadcasted_iota(jnp.int32, (t, t), 0)
                    >= lax.broadcasted_iota(jnp.int32, (t, t), 1))
        for s in range(nstream):
            sc = _dot_nt(qms[s], k2)
            if slopes is not None:
                sc = sc + slopes[s] * rel
            if kind == "C":
                sc = sc - LOG2E * fc_ref[0, pl.ds(2 * p + s, 1), pl.ds(start, nk)]
            if diag:
                sc = jnp.where(mask, sc, NEG_INF)
            m_prev = m_scr[s]
            m_next = jnp.maximum(m_prev, jnp.max(sc, axis=1, keepdims=True))
            alpha = jnp.exp2(m_prev - m_next)
            pexp = jnp.exp2((sc - jnp.tile(m_next, (1, nk // LANES))).astype(BF16))
            acc_scr[s] = alpha * acc_scr[s] + _dot(pexp, vhalf[head_of(s)])
            m_scr[s] = m_next

    tile(i * t, t, True)

    if kind == "Dsel":
        nq = pl.num_programs(2)
        base = (pl.program_id(0) * nq + i) * nq

        def body(j, carry):
            @pl.when(use_ref[base + j] != 0)
            def _():
                tile(j * t, t, False)
            return carry

        lax.fori_loop(0, i, body, 0)
    else:
        wide = WIDE_TILES * t

        def body(j, carry):
            tile(j * wide, wide, False)
            return carry

        lax.fori_loop(0, i // WIDE_TILES, body, 0)

        def rest(j, carry):
            tile(j * t, t, False)
            return carry

        lax.fori_loop(i // WIDE_TILES * WIDE_TILES, i, rest, 0)

    outs = [acc_scr[s] / pltpu.roll(acc_scr[s], HEAD_DIM, 1) for s in range(nstream)]
    if kind == "A":
        lam_init = 0.8 - 0.6 * math.exp(-0.3 * layer_idx)
        t1 = jnp.sum(lam_ref[0:1, :] * lam_ref[1:2, :], axis=1, keepdims=True)
        t2 = jnp.sum(lam_ref[2:3, :] * lam_ref[3:4, :], axis=1, keepdims=True)
        lam = jnp.exp(t1) - jnp.exp(t2) + lam_init
        d0 = outs[0] - lam * outs[1]
        d1 = outs[2] - lam * outs[3]
        lo_half = lane < HEAD_DIM
        o = jnp.where(lo_half, d0, d1)
        sq = o * o
        ss_lo = jnp.sum(jnp.where(lo_half, sq, 0.0), axis=1, keepdims=True)
        ss_hi = jnp.sum(jnp.where(lo_half, 0.0, sq), axis=1, keepdims=True)
        ms = jnp.where(lo_half, ss_lo, ss_hi) * (1.0 / HEAD_DIM)
        o = o * lax.rsqrt(ms + EPS) * sg_ref[...] * (1.0 - lam_init)
    else:
        o = jnp.where(lane < HEAD_DIM, outs[0], outs[1])
    o_ref[0] = o.astype(o_ref.dtype)


def _flash(kind, t, layer_idx, q_arr, qoff, k_arr, koff, v_arr, voff, extras):
    b, s, _ = q_arr.shape
    nstream = 4 if kind == "A" else 2
    shared_kv = kind == "Dsel"
    kw = k_arr.shape[-1] if shared_kv else LANES
    kv_idx = (lambda off: (lambda bi, p, i: (bi, 0, off))) if shared_kv else \
             (lambda off: (lambda bi, p, i: (bi, 0, off + p)))
    in_specs = [
        pl.BlockSpec((1, t, LANES), lambda bi, p, i: (bi, i, qoff + p)),
        pl.BlockSpec((1, s, kw), kv_idx(koff)),
        pl.BlockSpec((1, s, LANES), kv_idx(voff)),
    ]
    args = [q_arr, k_arr, v_arr]
    if kind == "C":
        fcum, = extras
        in_specs.append(pl.BlockSpec((1, 8, s), lambda bi, p, i: (bi, 0, 0)))
        args.append(fcum)
    elif kind == "Dsel":
        unsel, = extras
        nsel = unsel.shape[-1]
        nq, per = s // t, t // SEL_BLOCK
        used = (unsel.reshape(b, nq, t, nq, per) == 0).any(axis=(2, 4)).astype(jnp.int32).reshape(-1)
        in_specs.append(pl.BlockSpec((1, t, nsel), lambda bi, p, i: (bi, i, 0)))
        in_specs.append(pl.BlockSpec(memory_space=pltpu.SMEM))
        args += [unsel, used]
    elif kind == "A":
        lam8, sg = extras
        in_specs.append(pl.BlockSpec((8, LANES), lambda bi, p, i: (0, 0)))
        in_specs.append(pl.BlockSpec((1, LANES), lambda bi, p, i: (0, 0)))
        args += [lam8, sg]
    return pl.pallas_call(
        functools.partial(_flash_kernel, kind, t, layer_idx),
        grid=(b, 2, s // t),
        in_specs=in_specs,
        out_specs=pl.BlockSpec((1, t, LANES), lambda bi, p, i: (bi, i, p)),
        out_shape=jax.ShapeDtypeStruct((b, s, 2 * LANES), BF16),
        scratch_shapes=[pltpu.VMEM((nstream, t, LANES), F32)] * 2,
        compiler_params=_cparams(("arbitrary", "arbitrary", "arbitrary")),
        name="flash_" + kind,
    )(*args)


def _window_kernel(kind, t, wpad, *refs):
    it = iter(refs)
    q_ref, k_ref, v_ref = next(it), next(it), next(it)
    sink_ref = next(it) if kind == "B" else None
    o_ref = next(it)
    p = pl.program_id(1)
    i = pl.program_id(2)
    window = SWA_WINDOW if kind == "B" else NSA_WINDOW
    sl = SLOPES[1 if kind == "B" else 2] * LOG2E
    nk = wpad + t
    start = pl.multiple_of(jnp.maximum(i * t - wpad, 0), LANES)
    k2 = k_ref[0, pl.ds(start, nk), :]
    v2 = v_ref[0, pl.ds(start, nk), :]
    vlane = lax.broadcasted_iota(jnp.int32, (nk, LANES), 1)
    one = jnp.ones_like(v2)
    vhalf = [jnp.where(vlane < HEAD_DIM, v2, one), jnp.where(vlane < HEAD_DIM, one, v2)]
    q2 = q_ref[0]
    lane = lax.broadcasted_iota(jnp.int32, (t, LANES), 1)
    zero = jnp.zeros_like(q2)
    dist =((i * t - start) + lax.broadcasted_iota(jnp.int32, (t, nk), 0)
            - lax.broadcasted_iota(jnp.int32, (t, nk), 1))
    valid = (dist >= 0) & (dist < window)
    rel = (start - i * t + lax.broadcasted_iota(jnp.int32, (1, nk), 1)).astype(F32)
    qrow = lax.broadcasted_iota(jnp.int32, (t, 1), 0).astype(F32)
    outs = []
    for s in range(2):
        slope = jnp.where(p == 0, float(sl[s]), float(sl[2 + s]))
        qm = jnp.where((lane >= s * HEAD_DIM) & (lane < (s + 1) * HEAD_DIM), q2, zero)
        sc = jnp.where(valid, _dot_nt(qm, k2) + slope * rel, NEG_INF)
        m = jnp.max(sc, axis=1, keepdims=True)
        if kind == "B":
            sink = LOG2E * sink_ref[2 * p + s] + slope * qrow
            m = jnp.maximum(m, sink)
        pv = _dot(jnp.exp2((sc - m).astype(BF16)), vhalf[s])
        l = pltpu.roll(pv, HEAD_DIM, 1)
        if kind == "B":
            l = l + jnp.exp2(sink - m)
        outs.append(pv / l)
    o_ref[0] = jnp.where(lane < HEAD_DIM, outs[0], outs[1]).astype(o_ref.dtype)


def _window(kind, t, wpad, q_arr, qoff, k_arr, koff, v_arr, voff, extras):
    b, s, _ = q_arr.shape
    shared_kv = kind == "Dwin"
    kv_idx = (lambda off: (lambda bi, p, i: (bi, 0, off))) if shared_kv else \
             (lambda off: (lambda bi, p, i: (bi, 0, off + p)))
    in_specs = [
        pl.BlockSpec((1, t, LANES), lambda bi, p, i: (bi, i, qoff + p)),
        pl.BlockSpec((1, s, LANES), kv_idx(koff)),
        pl.BlockSpec((1, s, LANES), kv_idx(voff)),
    ]
    args = [q_arr, k_arr, v_arr]
    if kind == "B":
        in_specs.append(pl.BlockSpec(memory_space=pltpu.SMEM))
        args.append(extras[0])
    return pl.pallas_call(
        functools.partial(_window_kernel, kind, t, wpad),
        grid=(b, 2, s // t),
        in_specs=in_specs,
        out_specs=pl.BlockSpec((1, t, LANES), lambda bi, p, i: (bi, i, p)),
        out_shape=jax.ShapeDtypeStruct((b, s, 2 * LANES), BF16),
        compiler_params=_cparams(("arbitrary", "arbitrary", "arbitrary")),
        name="window_" + kind,
    )(*args)


def _gelu_tanh(x):
    return 0.5 * x * (1.0 + jnp.tanh(math.sqrt(2.0 / math.pi) * (x + 0.044715 * x * x * x)))


def _compress_kernel(t_ref, w1_ref, pos_ref, b1_ref, w2_ref, o_ref):
    tr = t_ref[0, 0]
    half = tr.shape[1]
    n = tr.shape[0]
    w1 = w1_ref[0]
    u = _dot_f32(tr, w1[:half])
    v = _dot_f32(tr, w1[half:])
    cpos = _dot_f32(pos_ref[0], w1)[0:1] + b1_ref[0]
    hid = u + pltpu.roll(v, n - 1, 0) + cpos
    o_ref[0, 0] = _dot_f32(_gelu_tanh(hid), w2_ref[0])


def _compress(kv_r, w1, pos8, b1, w2d):
    two, b, n, dd = kv_r.shape
    return pl.pallas_call(
        _compress_kernel,
        grid=(two, b),
        in_specs=[
            pl.BlockSpec((1, 1, n, dd), lambda c, bi: (c, bi, 0, 0)),
            pl.BlockSpec((1, 2 * dd, CMP_HIDDEN), lambda c, bi: (c, 0, 0)),
            pl.BlockSpec((1, 8, 2 * dd), lambda c, bi: (c, 0, 0)),
            pl.BlockSpec((1, 1, CMP_HIDDEN), lambda c, bi: (c, 0, 0)),
            pl.BlockSpec((1, CMP_HIDDEN, LANES), lambda c, bi: (c, 0, 0)),
        ],
        out_specs=pl.BlockSpec((1, 1, n, LANES), lambda c, bi: (c, bi, 0, 0)),
        out_shape=jax.ShapeDtypeStruct((two, b, n, LANES), F32),
        compiler_params=_cparams(("arbitrary", "arbitrary")),
        name="nsa_compress",
    )(kv_r, w1, pos8, b1, w2d)


def _cmp_kernel(t, q_ref, kc_ref, vc_ref, cov_ref, o_ref, un_ref):
    i = pl.program_id(1)
    ncp = kc_ref.shape[2]
    nsel = cov_ref.shape[0]
    kc = kc_ref[0, 0]
    vc = vc_ref[0, 0]
    kch, kcl = _split_bf16(kc)
    vcb = vc.astype(BF16)
    lane = lax.broadcasted_iota(jnp.int32, (t, LANES), 1)
    tq = i * t + lax.broadcasted_iota(jnp.int32, (t, ncp), 0)
    nidx = lax.broadcasted_iota(jnp.int32, (t, ncp), 1)
    cmp_end = nidx * CMP_STRIDE + (CMP_BLOCK - 1)
    valid = (tq >= cmp_end) & (nidx < ncp - 1)
    end_row = cmp_end[0:1, :].astype(F32)
    psum = jnp.zeros((t, ncp), F32)
    outs = []
    for pair in range(2):
        q2 = q_ref[0, :, pair * LANES:(pair + 1) * LANES]
        for g in range(2):
            h = 2 * pair + g
            qm = jnp.where((lane >= g * HEAD_DIM) & (lane < (g + 1) * HEAD_DIM), q2, jnp.zeros_like(q2))
            sc = _dot_nt(qm, kch) + _dot_nt(qm, kcl) + float(SLOPES[2][h] * LOG2E) * end_row
            sc = jnp.where(valid, sc, NEG_INF)
            m = jnp.max(sc, axis=1, keepdims=True)
            pe = jnp.exp2(sc - m)
            l = jnp.sum(pe, axis=1, keepdims=True)
            pc = pe * jnp.where(m > 0.5 * NEG_INF, 1.0 / l, 0.0)
            psum = psum + pc
            outs.append(_dot(pc.astype(BF16), vcb))
    lo_half = lane < HEAD_DIM
    o_ref[0, :, 0:LANES] = jnp.where(lo_half, outs[0], outs[1]).astype(o_ref.dtype)
    o_ref[0, :, LANES:2 * LANES] = jnp.where(lo_half, outs[2], outs[3]).astype(o_ref.dtype)
    ph, plo = _split_bf16(psum)
    cov = cov_ref[...]
    imp = _dot_nt(cov, ph) + _dot_nt(cov, plo)
    blk = lax.broadcasted_iota(jnp.int32, (nsel, t), 0)
    cur = (i * t + lax.broadcasted_iota(jnp.int32, (nsel, t), 1)) // SEL_BLOCK
    forced = (blk == 0) | (blk == cur) | (blk == cur - 1)
    score = jnp.where(forced, FORCE_SCORE, jnp.where(blk <= cur, imp, -1.0))
    unsel = jnp.ones((nsel, t), F32)
    for _ in range(min(SEL_TOPK, nsel)):
        mx = jnp.max(score, axis=0, keepdims=True)
        idx = jnp.min(jnp.where(score == mx, blk, nsel), axis=0, keepdims=True)
        hit = blk == idx
        unsel = jnp.where(hit, 0.0, unsel)
        score = jnp.where(hit, -3.0, score)
    un_ref[0] = unsel.T.astype(un_ref.dtype)


def _cmp_attention(t, qd, kvc, cover_t):
    b, s, _ = qd.shape
    ncp = kvc.shape[2]
    nsel = cover_t.shape[0]
    return pl.pallas_call(
        functools.partial(_cmp_kernel, t),
        grid=(b, s // t),
        in_specs=[
            pl.BlockSpec((1, t, 2 * LANES), lambda bi, i: (bi, i, 0)),
            pl.BlockSpec((1, 1, ncp, LANES), lambda bi, i: (0, bi, 0, 0)),
            pl.BlockSpec((1, 1, ncp, LANES), lambda bi, i: (1, bi, 0, 0)),
            pl.BlockSpec((nsel, ncp), lambda bi, i: (0, 0)),
        ],
        out_specs=[
            pl.BlockSpec((1, t, 2 * LANES), lambda bi, i: (bi, i, 0)),
            pl.BlockSpec((1, t, nsel), lambda bi, i: (bi, i, 0)),
        ],
        out_shape=[
            jax.ShapeDtypeStruct((b, s, 2 * LANES), BF16),
            jax.ShapeDtypeStruct((b, s, nsel), BF16),
        ],
        compiler_params=_cparams(("arbitrary", "arbitrary")),
        name="nsa_cmp_select",
    )(qd, kvc, kvc, cover_t)


def _cover_matrix(s):
    ncp, nsel = s // CMP_STRIDE, s // SEL_BLOCK
    cs = np.arange(ncp)[:, None] * CMP_STRIDE
    ss = np.arange(nsel)[None, :] * SEL_BLOCK
    cov = np.clip(np.minimum(cs + CMP_BLOCK, ss + SEL_BLOCK) - np.maximum(cs, ss), 0, None) / CMP_BLOCK
    cov[ncp - 1] = 0.0
    return jnp.asarray(cov.T, BF16)


def _expand_neg(s):
    nsel = s // SEL_BLOCK
    e = (np.arange(s)[:, None] // SEL_BLOCK) == np.arange(nsel)[None, :]
    return jnp.asarray(np.where(e, -MASK_BIG, 0.0), BF16)


def _outproj_kernel(ya, yb, yc, ocmp, osel, owin, gt, gx_ref, x_ref, mod_ref, wo_ref, g_ref,
                    rwt_ref, rb_ref, xo_ref, h_ref, ri_ref, rg_ref, cnt_ref, carry):
    first = (pl.program_id(0) == 0) & (pl.program_id(1) == 0)
    gw = GROUP_WIDTH
    g = _dot(gt[0], gx_ref[...])
    yd = (g[:, 0:gw] * ocmp[0].astype(F32) + g[:, gw:2 * gw] * osel[0].astype(F32)
          + g[:, 2 * gw:3 * gw] * owin[0].astype(F32))
    ycat = jnp.concatenate([ya[0], yb[0], yc[0], yd.astype(BF16)], axis=1)
    y = _dot(ycat, wo_ref[...])
    x = x_ref[0] + mod_ref[2:3, :] * y
    xo_ref[0] = x
    ms = jnp.mean(x * x, axis=-1, keepdims=True)
    h = x * lax.rsqrt(ms + EPS) * g_ref[...]
    h = h * (1.0 + mod_ref[4:5, :]) + mod_ref[3:4, :]
    h_ref[0] = h
    tm = h.shape[0]
    ng = LANES // EXPERTS_PER_GROUP
    big = 4 * LANES
    logits = _dot_f32(rwt_ref[...], h, nt=True)
    eidx = lax.broadcasted_iota(jnp.int32, (LANES, tm), 0)
    aff = jax.nn.sigmoid(logits)
    sel = jnp.where(eidx < N_EXPERTS, aff + rb_ref[:, 0:1], NEG_INF)
    sel3 = sel.reshape(ng, EXPERTS_PER_GROUP, tm)
    e3 = eidx.reshape(ng, EXPERTS_PER_GROUP, tm)
    g1 = jnp.max(sel3, axis=1, keepdims=True)
    i1 = jnp.min(jnp.where(sel3 == g1, e3, big), axis=1, keepdims=True)
    sel_b = jnp.where(e3 == i1, NEG_INF, sel3)
    g2 = jnp.max(sel_b, axis=1, keepdims=True)
    i2 = jnp.min(jnp.where(sel_b == g2, e3, big), axis=1, keepdims=True)
    gs = g1 + g2
    gidx = lax.broadcasted_iota(jnp.int32, (ng, 1, tm), 0)
    gm = jnp.max(gs, axis=0, keepdims=True)
    best = gidx == jnp.min(jnp.where(gs == gm, gidx, big), axis=0, keepdims=True)
    e1 = jnp.min(jnp.where(best, i1, big), axis=0)
    e2 = jnp.min(jnp.where(best, i2, big), axis=0)
    oh1 = eidx == e1
    oh2 = eidx == e2
    a1 = jnp.sum(jnp.where(oh1, aff, 0.0), axis=0, keepdims=True)
    a2 = jnp.sum(jnp.where(oh2, aff, 0.0), axis=0, keepdims=True)
    inv = 1.0 / (a1 + a2)
    @pl.when(first)
    def _():
        carry[...] = jnp.zeros_like(carry)

    ohs = jnp.where(oh1 | oh2, 1.0, 0.0)
    rr = lax.broadcasted_iota(jnp.int32, (tm, tm), 0)
    cc = lax.broadcasted_iota(jnp.int32, (tm, tm), 1)
    earlier = jnp.where(rr < cc, 1.0, 0.0).astype(BF16)
    before = _dot(ohs.astype(BF16), earlier) + carry[:, 0:1]
    r1 = jnp.sum(jnp.where(oh1, before, 0.0), axis=0, keepdims=True)
    r2 = jnp.sum(jnp.where(oh2, before, 0.0), axis=0, keepdims=True)
    total = carry[:, 0:1] + jnp.sum(ohs, axis=1, keepdims=True)
    carry[...] = jnp.broadcast_to(total, carry.shape)
    cnt_ref[...] = jnp.broadcast_to(total, cnt_ref.shape).astype(jnp.int32)
    row = lax.broadcasted_iota(jnp.int32, (8, tm), 0)
    ri_ref[0] = jnp.where(row == 0, e1, jnp.where(row == 1, e2, jnp.where(
        row == 2, r1.astype(jnp.int32), jnp.where(row == 3, r2.astype(jnp.int32), 0))))
    rg_ref[0] = jnp.where(row == 0, a1 * inv, jnp.where(row == 1, a2 * inv, 0.0))


def _gate_expander():
    e = np.zeros((W_G, 3 * GROUP_WIDTH), np.float32)
    for h in range(HEADS):
        for r in range(3):
            e[h * 3 + r, r * GROUP_WIDTH + h * HEAD_DIM:r * GROUP_WIDTH + (h + 1) * HEAD_DIM] = 1.0
    return jnp.asarray(e, BF16)


def _out_projection(ya, yb, yc, ocmp, osel, owin, gates, x, mod_l, wo, g, rwt, rb, tm):
    b, s, d = x.shape
    gw = GROUP_WIDTH
    tok = lambda w: pl.BlockSpec((1, tm, w), lambda bi, i: (bi, i, 0))
    full = lambda shp: pl.BlockSpec(shp, lambda bi, i: (0,) * len(shp))
    rowblk = pl.BlockSpec((1, 8, tm), lambda bi, i: (bi, 0, i))
    return pl.pallas_call(
        _outproj_kernel,
        grid=(b, s // tm),
        in_specs=[tok(gw)] * 6 + [tok(W_G), full((W_G, 3 * gw)), tok(d),
                                  pl.BlockSpec((None, 6, d), lambda bi, i: (bi, 0, 0)),
                                  full((d, d)), full((1, d)), full((LANES, d)), full((LANES, 1))],
        out_specs=[tok(d), tok(d), rowblk, rowblk, full((LANES, LANES))],
        out_shape=[
            jax.ShapeDtypeStruct((b, s, d), F32),
            jax.ShapeDtypeStruct((b, s, d), F32),
            jax.ShapeDtypeStruct((b, 8, s), jnp.int32),
            jax.ShapeDtypeStruct((b, 8, s), F32),
            jax.ShapeDtypeStruct((LANES, LANES), jnp.int32),
        ],
        scratch_shapes=[pltpu.VMEM((LANES, LANES), F32)],
        compiler_params=_cparams(("arbitrary", "arbitrary")),
        name="out_projection_router",
    )(ya, yb, yc, ocmp, osel, owin, gates, _gate_expander(), x, mod_l, wo, g.reshape(1, d), rwt, rb)


def _dispatch_kernel(p0_ref, p1_ref, h_ref, xs_ref, sem):
    tm = h_ref.shape[0]

    def issue(r, c):
        for pos_ref in (p0_ref, p1_ref):
            pltpu.make_async_copy(h_ref.at[pl.ds(r, 1), :],
                                  xs_ref.at[pl.ds(pos_ref[r], 1), :], sem).start()
        return c

    lax.fori_loop(0, tm, issue, 0, unroll=4)
    for _ in range(2):
        pltpu.make_async_copy(h_ref, xs_ref.at[pl.ds(0, tm), :], sem).wait()


def _dispatch(h, pos0, pos1, tm):
    t, d = h.shape
    return pl.pallas_call(
        _dispatch_kernel,
        grid=(t // tm,),
        in_specs=[
            pl.BlockSpec((tm,), lambda i: (i,), memory_space=pltpu.SMEM),
            pl.BlockSpec((tm,), lambda i: (i,), memory_space=pltpu.SMEM),
            pl.BlockSpec((tm, d), lambda i: (i, 0)),
        ],
        out_specs=pl.BlockSpec(memory_space=pl.ANY),
        out_shape=jax.ShapeDtypeStruct((2 * t, d), F32),
        scratch_shapes=[pltpu.SemaphoreType.DMA(())],
        compiler_params=_cparams(("arbitrary",)),
        name="moe_dispatch",
    )(pos0, pos1, h)


def _expert_kernel(vb_ref, ve_ref, vlo_ref, vhi_ref, nv_ref,
                   xs_ref, wg_ref, wu_ref, wd_ref, ys_ref, wgb, wub, wdb):
    v = pl.program_id(0)
    rows = xs_ref.shape[0]
    prev = jnp.maximum(v - 1, 0)
    new_expert = (v == 0) | (ve_ref[v] != ve_ref[prev])
    new_block = (v == 0) | (vb_ref[v] != vb_ref[prev])

    @pl.when(v < nv_ref[0])
    def _():
        @pl.when(new_expert)
        def _():
            wgb[...] = wg_ref[0, 0].astype(BF16)
            wub[...] = wu_ref[0, 0].astype(BF16)
            wdb[...] = wd_ref[0, 0].astype(BF16)

        x = xs_ref[...].astype(BF16)
        hg = _dot(x, wgb[...])
        hu = _dot(x, wub[...])
        hm = (hg * jax.nn.sigmoid(hg) * hu).astype(BF16)
        y = _dot(hm, wdb[...])
        r = vb_ref[v] * rows + lax.broadcasted_iota(jnp.int32, (rows, 1), 0)
        mine = (r >= vlo_ref[v]) & (r < vhi_ref[v])

        @pl.when(new_block)
        def _():
            ys_ref[...] = jnp.where(mine, y, 0.0)

        @pl.when(jnp.logical_not(new_block))
        def _():
            ys_ref[...] = jnp.where(mine, y, ys_ref[...])


def _experts(xs, meta, layer, wg, wu, wd, rows, nvmax):
    n, d = xs.shape
    de = wg.shape[3]
    grid_spec = pltpu.PrefetchScalarGridSpec(
        num_scalar_prefetch=5,
        grid=(nvmax,),
        in_specs=[
            pl.BlockSpec((rows, d), lambda v, vb, ve, lo, hi, nv: (vb[v], 0)),
            pl.BlockSpec((1, 1, d, de), lambda v, vb, ve, lo, hi, nv: (layer, ve[v], 0, 0)),
            pl.BlockSpec((1, 1, d, de), lambda v, vb, ve, lo, hi, nv: (layer, ve[v], 0, 0)),
            pl.BlockSpec((1, 1, de, d), lambda v, vb, ve, lo, hi, nv: (layer, ve[v], 0, 0)),
        ],
        out_specs=pl.BlockSpec((rows, d), lambda v, vb, ve, lo, hi, nv: (vb[v], 0)),
        scratch_shapes=[pltpu.VMEM((d, de), BF16), pltpu.VMEM((d, de), BF16), pltpu.VMEM((de, d), BF16)],
    )
    return pl.pallas_call(
        _expert_kernel,
        grid_spec=grid_spec,
        out_shape=jax.ShapeDtypeStruct((n, d), F32),
        compiler_params=_cparams(("arbitrary",)),
        name="moe_experts",
    )(*meta, xs, wg, wu, wd)


def _combine_kernel(final, p0_ref, p1_ref, x_ref, rg_ref, mod_ref, g_ref, ys_ref, o_ref, buf, sem):
    tm = x_ref.shape[0]

    def issue(r, c):
        for k, pos_ref in enumerate((p0_ref, p1_ref)):
            pltpu.make_async_copy(ys_ref.at[pl.ds(pos_ref[r], 1), :],
                                  buf.at[k, pl.ds(r, 1), :], sem).start()
        return c

    lax.fori_loop(0, tm, issue, 0, unroll=4)
    for k in range(2):
        pltpu.make_async_copy(ys_ref.at[pl.ds(0, tm), :], buf.at[k], sem).wait()
    rg = rg_ref[...]
    y = rg[:, 0:1] * buf[0] + rg[:, 1:2] * buf[1]
    x = x_ref[...] + mod_ref[5:6, :] * y
    if final:
        ms = jnp.mean(x * x, axis=-1, keepdims=True)
        x = x * lax.rsqrt(ms + EPS) * g_ref[...]
    o_ref[...] = x


def _combine(final, pos0, pos1, x, rg, mod_l, g, ys, tm):
    t, d = x.shape
    per_b = t // mod_l.shape[0] // tm
    return pl.pallas_call(
        functools.partial(_combine_kernel, final),
        grid=(t // tm,),
        in_specs=[
            pl.BlockSpec((tm,), lambda i: (i,), memory_space=pltpu.SMEM),
            pl.BlockSpec((tm,), lambda i: (i,), memory_space=pltpu.SMEM),
            pl.BlockSpec((tm, d), lambda i: (i, 0)),
            pl.BlockSpec((tm, 8), lambda i: (i, 0)),
            pl.BlockSpec((None, 6, d), lambda i: (i // per_b, 0, 0)),
            pl.BlockSpec((1, d), lambda i: (0, 0)),
            pl.BlockSpec(memory_space=pl.ANY),
        ],
        out_specs=pl.BlockSpec((tm, d), lambda i: (i, 0)),
        out_shape=jax.ShapeDtypeStruct((t, d), F32),
        scratch_shapes=[pltpu.VMEM((2, tm, d), F32), pltpu.SemaphoreType.DMA(())],
        compiler_params=_cparams(("arbitrary",)),
        name="moe_combine",
    )(pos0, pos1, x, rg, mod_l, g.reshape(1, d), ys)


def _visit_plan(counts, rows, nblocks):
    ne = counts.shape[0]
    nvmax = nblocks + ne - 1
    ends = jnp.cumsum(counts)
    offs = ends - counts
    b_lo = offs // rows
    b_hi = jnp.maximum(ends - 1, 0) // rows
    nvis = jnp.where(counts > 0, b_hi - b_lo + 1, 0)
    vend = jnp.cumsum(nvis)
    vstart = vend - nvis
    nv = vend[-1]
    v = jnp.minimum(jnp.arange(nvmax), nv - 1)
    e = jnp.sum((v[:, None] >= vend[None, :]).astype(jnp.int32), axis=1)
    onehot = e[:, None] == jnp.arange(ne)[None, :]
    pick = lambda a: jnp.sum(jnp.where(onehot, a[None, :], 0), axis=1)
    blk = pick(b_lo) + v - pick(vstart)
    lo = jnp.maximum(pick(offs), blk * rows)
    hi = jnp.minimum(pick(ends), (blk + 1) * rows)
    i32 = lambda a: a.astype(jnp.int32)
    return (i32(blk), i32(e), i32(lo), i32(hi), i32(nv).reshape(1)), offs, nvmax


def _tiles(s):
    tm = min(512, s)
    return dict(tm=tm, t_full=min(512, s), t_win=min(256, s), t_cmp=min(256, s),
                tm_moe=min(512, s), rows=256)


def _layer(l, x, mod_l, p, consts, final):
    b, s, d = x.shape
    tl = _tiles(s)
    wp, wf = _prep_w_in(p["w_in"])
    qa, qb, qc, qd, gates, kvcmp, fcum = _in_projection(
        x, mod_l, p["norm_attn_g"], wp, wf, p["fox_forget_b"], tl["tm"])
    lam8 = jnp.zeros((8, LANES), F32).at[:4, :DIFF_DK].set(
        jnp.stack([p["diff_lam_q1"], p["diff_lam_k1"], p["diff_lam_q2"], p["diff_lam_k2"]]))
    sg = jnp.tile(p["diff_subln_g"], 2).reshape(1, LANES)
    ya = _flash("A", tl["t_full"], l, qa, 0, qa, 2, qa, 4, (lam8, sg))
    yb = _window("B", tl["t_win"], SWA_WINDOW, qb, 0, qb, 2, qb, 4, (p["swa_sinks"],))
    yc = _flash("C", tl["t_full"], l, qc, 0, qc, 2, qc, 4, (fcum,))
    n16 = s // CMP_STRIDE
    kv_r = jnp.stack([kvcmp[..., :HEAD_DIM].reshape(b, n16, CMP_STRIDE * HEAD_DIM),
                      kvcmp[..., HEAD_DIM:].reshape(b, n16, CMP_STRIDE * HEAD_DIM)])
    pos8 = jnp.zeros((2, 8, CMP_BLOCK * HEAD_DIM), F32).at[:, 0].set(
        p["nsa_cmp_pos"].reshape(2, CMP_BLOCK * HEAD_DIM))
    w2d = jnp.concatenate([p["nsa_cmp_w2"], p["nsa_cmp_w2"]], axis=-1)
    kvc = _compress(kv_r, p["nsa_cmp_w1"], pos8, p["nsa_cmp_b1"].reshape(2, 1, CMP_HIDDEN), w2d)
    ocmp, unsel = _cmp_attention(tl["t_cmp"], qd, kvc, consts["cover"])
    ksel = jnp.concatenate([qd[..., 2 * LANES:3 * LANES],
                            jnp.broadcast_to(consts["eneg"][None], (b,) + consts["eneg"].shape)], axis=-1)
    osel = _flash("Dsel", tl["t_full"], l, qd, 0, ksel, 0, qd, 3, (unsel,))
    owin = _window("Dwin", tl["t_win"], NSA_WINDOW, qd, 0, qd, 4, qd, 5, ())
    rwt = jnp.zeros((LANES, d), F32).at[:N_EXPERTS].set(p["router_w"].T)
    rb = jnp.zeros((LANES, 1), F32).at[:N_EXPERTS, 0].set(p["router_b"])
    xm, h2, ri, rg, cnt = _out_projection(
        ya, yb, yc, ocmp, osel, owin, gates, x, mod_l, p["w_out"].astype(BF16),
        p["norm_moe_g"], rwt, rb, tl["tm"])
    t = b * s
    counts = cnt[:N_EXPERTS, 0]
    rows = tl["rows"]
    meta, offs, nvmax = _visit_plan(counts, rows, 2 * t // rows)
    onehot = ri[:, 0:2, :, None] == jnp.arange(N_EXPERTS)[None, None, None, :]
    pos = (jnp.sum(jnp.where(onehot, offs[None, None, None, :], 0), axis=-1) + ri[:, 2:4]).astype(jnp.int32)
    pos0, pos1 = pos[:, 0].reshape(t), pos[:, 1].reshape(t)
    rgt = rg.transpose(0, 2, 1).reshape(t, 8)
    xs = _dispatch(h2.reshape(t, d), pos0, pos1, tl["tm_moe"])
    ys = _experts(xs, meta, l, p["exp_w_gate"], p["exp_w_up"], p["exp_w_down"], rows, nvmax)
    out = _combine(final, pos0, pos1, xm.reshape(t, d), rgt, mod_l, p["norm_final_g"], ys, tl["tm_moe"])
    return out.reshape(b, s, d)


def kernel(x, c, ada_w, ada_b, norm_attn_g, norm_moe_g, norm_final_g, w_in, w_out, diff_lam_q1, diff_lam_k1, diff_lam_q2, diff_lam_k2, diff_subln_g, swa_sinks, fox_forget_b, nsa_cmp_pos, nsa_cmp_w1, nsa_cmp_b1, nsa_cmp_w2, router_w, router_b, exp_w_gate, exp_w_up, exp_w_down):
    depth = ada_w.shape[0]
    s = x.shape[1]
    mod = _modulation(c, ada_w, ada_b)
    consts = dict(cover=_cover_matrix(s), eneg=_expand_neg(s))
    for l in range(depth):
        p = dict(
            norm_attn_g=norm_attn_g[l], norm_moe_g=norm_moe_g[l], norm_final_g=norm_final_g,
            w_in=w_in[l], w_out=w_out[l],
            diff_lam_q1=diff_lam_q1[l], diff_lam_k1=diff_lam_k1[l],
            diff_lam_q2=diff_lam_q2[l], diff_lam_k2=diff_lam_k2[l], diff_subln_g=diff_subln_g[l],
            swa_sinks=swa_sinks[l], fox_forget_b=fox_forget_b[l],
            nsa_cmp_pos=nsa_cmp_pos[l], nsa_cmp_w1=nsa_cmp_w1[l], nsa_cmp_b1=nsa_cmp_b1[l],
            nsa_cmp_w2=nsa_cmp_w2[l], router_w=router_w, router_b=router_b,
            exp_w_gate=exp_w_gate, exp_w_up=exp_w_up, exp_w_down=exp_w_down,
        )
        x = _layer(l, x, mod[l], p, consts, final=(l == depth - 1))
    return x
```

```python
import functools
import math

import numpy as np
import jax
import jax.numpy as jnp
from jax import lax
from jax.experimental import pallas as pl
from jax.experimental.pallas import tpu as pltpu

F32 = jnp.float32
BF16 = jnp.bfloat16

LANES = 128
HEAD_DIM = 64
N_MIXERS = 4
HEADS = 4
GROUP_WIDTH = HEADS * HEAD_DIM
DIFF_DK = HEAD_DIM // 2
SWA_WINDOW = 128
SWA_KV_HEADS = 2
CMP_BLOCK = 32
CMP_STRIDE = 16
CMP_HIDDEN = 2 * HEAD_DIM
SEL_BLOCK = 64
SEL_TOPK = 16
NSA_WINDOW = 512
FORCE_SCORE = 1e4
NEG_INF = -1e30
MASK_BIG = 2.0 ** 100
N_EXPERTS = 64
EXPERTS_PER_GROUP = 8
EPS = 1e-6
LOG2E = math.log2(math.e)
VMEM_LIMIT = 56 * 1024 * 1024
WIDE_TILES = 4

IN_SPLITS = (
    GROUP_WIDTH, GROUP_WIDTH, GROUP_WIDTH,
    GROUP_WIDTH, SWA_KV_HEADS * HEAD_DIM, SWA_KV_HEADS * HEAD_DIM,
    GROUP_WIDTH, GROUP_WIDTH, GROUP_WIDTH, HEADS,
    GROUP_WIDTH, HEAD_DIM, HEAD_DIM, HEAD_DIM, HEAD_DIM,
    HEAD_DIM, HEAD_DIM, 3 * HEADS,
)

_NT = (((1,), (1,)), ((), ()))


def _alibi_slopes():
    n = 3 * HEADS
    m = 2.0 ** (-8.0 * np.arange(1, n + 1) / n)
    return m.reshape(HEADS, 3).T


SLOPES = _alibi_slopes()


def _dot(a, b):
    return jnp.dot(a, b, preferred_element_type=F32)


def _dot_nt(a, b):
    return lax.dot_general(a, b, _NT, preferred_element_type=F32)


def _split_bf16(a):
    hi = a.astype(BF16)
    lo = (a - hi.astype(F32)).astype(BF16)
    return hi, lo


def _dot_f32(a, b, nt=False):
    d = _dot_nt if nt else _dot
    ah, al = _split_bf16(a)
    bh, bl = _split_bf16(b)
    return d(ah, bh) + (d(ah, bl) + d(al, bh))


def _cparams(sem):
    return pltpu.CompilerParams(dimension_semantics=sem, vmem_limit_bytes=VMEM_LIMIT)


def _mod_kernel(c_ref, w_ref, b_ref, o_ref):
    c = c_ref[...]
    cond = c * jax.nn.sigmoid(c)
    o_ref[0] = _dot_f32(cond, w_ref[0]) + b_ref[0]


def _modulation(c, ada_w, ada_b):
    depth, d, n = ada_w.shape
    b = c.shape[0]
    rows = 8
    cp = jnp.zeros((rows, d), F32).at[:b].set(c)
    tn = 1536
    out = pl.pallas_call(
        _mod_kernel,
        grid=(depth, n // tn),
        in_specs=[
            pl.BlockSpec((rows, d), lambda l, j: (0, 0)),
            pl.BlockSpec((1, d, tn), lambda l, j: (l, 0, j)),
            pl.BlockSpec((1, 1, tn), lambda l, j: (l, 0, j)),
        ],
        out_specs=pl.BlockSpec((1, rows, tn), lambda l, j: (l, 0, j)),
        out_shape=jax.ShapeDtypeStruct((depth, rows, n), F32),
        compiler_params=_cparams(("arbitrary", "arbitrary")),
        name="adaln_mod",
    )(cp, ada_w, ada_b.reshape(depth, 1, n))
    return out[:, :b].reshape(depth, b, 6, d)


W_A, W_B, W_C, W_D, W_G, W_CMP = 768, 768, 768, 768, 128, 128
W_OFFS = np.cumsum((0, W_A, W_B, W_C, W_D, W_G, W_CMP))
W_TOTAL = int(W_OFFS[-1])


def _prep_w_in(w):
    offs = np.cumsum((0,) + IN_SPLITS)
    col = lambda i: w[:, int(offs[i]):int(offs[i + 1])]
    dup = lambda t: jnp.concatenate([t, t], axis=1)
    kb, vb = col(4), col(5)
    gates = jnp.zeros((w.shape[0], W_G), F32).at[:, :3 * HEADS].set(col(17))
    qs64 = LOG2E * HEAD_DIM ** -0.5
    parts = [
        col(0) * (LOG2E * DIFF_DK ** -0.5), col(1), col(2),
        col(3) * qs64,
        dup(kb[:, :HEAD_DIM]), dup(kb[:, HEAD_DIM:]),
        dup(vb[:, :HEAD_DIM]), dup(vb[:, HEAD_DIM:]),
        col(6) * qs64, col(7), col(8),
        col(10) * qs64, dup(col(13)), dup(col(14)), dup(col(15)), dup(col(16)),
    ] + [gates, col(11), col(12)]
    wp = jnp.concatenate(parts, axis=1).astype(BF16)
    wf = jnp.zeros((8, w.shape[0]), F32).at[:HEADS].set(col(9).T).astype(BF16)
    return wp, wf


def _lane_cumsum(y):
    n = y.shape[1]
    lane = lax.broadcasted_iota(jnp.int32, y.shape, 1)
    sh = 1
    while sh < n:
        y = y + jnp.where(lane >= sh, pltpu.roll(y, sh, 1), 0.0)
        sh *= 2
    return y


def _inproj_kernel(x_ref, mod_ref, g_ref, w_ref, wf_ref, bf_ref,
                   oa, ob, oc, od, og, ocmp, ofc, carry):
    i = pl.program_id(1)
    x = x_ref[0]
    ms = jnp.mean(x * x, axis=-1, keepdims=True)
    y = x * lax.rsqrt(ms + EPS) * g_ref[...]
    h = y * (1.0 + mod_ref[1:2, :]) + mod_ref[0:1, :]
    hb = h.astype(BF16)
    o = W_OFFS
    oa[0] = _dot(hb, w_ref[:, o[0]:o[1]]).astype(BF16)
    ob[0] = _dot(hb, w_ref[:, o[1]:o[2]]).astype(BF16)
    oc[0] = _dot(hb, w_ref[:, o[2]:o[3]]).astype(BF16)
    od[0] = _dot(hb, w_ref[:, o[3]:o[4]]).astype(BF16)
    og[0] = jax.nn.sigmoid(_dot(hb, w_ref[:, o[4]:o[5]])).astype(BF16)
    ocmp[0] = _dot(hb, w_ref[:, o[5]:o[6]])
    fl = _dot_nt(wf_ref[...], hb) + bf_ref[:, 0:1]
    ls = jnp.minimum(fl, 0.0) - jnp.log1p(jnp.exp(-jnp.abs(fl)))

    @pl.when(i == 0)
    def _():
        carry[...] = jnp.zeros_like(carry)

    cs = _lane_cumsum(ls) + carry[:, 0:1]
    ofc[0] = cs
    carry[...] = jnp.broadcast_to(cs[:, cs.shape[1] - 1:], carry.shape)


def _in_projection(x, mod_l, g, wp, wf, bf, tm):
    b, s, d = x.shape
    bf8 = jnp.zeros((8, LANES), F32).at[:HEADS, :].set(bf[:, None])
    outs = pl.pallas_call(
        _inproj_kernel,
        grid=(b, s // tm),
        in_specs=[
            pl.BlockSpec((1, tm, d), lambda bi, i: (bi, i, 0)),
            pl.BlockSpec((None, 6, d), lambda bi, i: (bi, 0, 0)),
            pl.BlockSpec((1, d), lambda bi, i: (0, 0)),
            pl.BlockSpec((d, W_TOTAL), lambda bi, i: (0, 0)),
            pl.BlockSpec((8, d), lambda bi, i: (0, 0)),
            pl.BlockSpec((8, LANES), lambda bi, i: (0, 0)),
        ],
        out_specs=[
            pl.BlockSpec((1, tm, W_A), lambda bi, i: (bi, i, 0)),
            pl.BlockSpec((1, tm, W_B), lambda bi, i: (bi, i, 0)),
            pl.BlockSpec((1, tm, W_C), lambda bi, i: (bi, i, 0)),
            pl.BlockSpec((1, tm, W_D), lambda bi, i: (bi, i, 0)),
            pl.BlockSpec((1, tm, W_G), lambda bi, i: (bi, i, 0)),
            pl.BlockSpec((1, tm, W_CMP), lambda bi, i: (bi, i, 0)),
            pl.BlockSpec((1, 8, tm), lambda bi, i: (bi, 0, i)),
        ],
        out_shape=[
            jax.ShapeDtypeStruct((b, s, W_A), BF16),
            jax.ShapeDtypeStruct((b, s, W_B), BF16),
            jax.ShapeDtypeStruct((b, s, W_C), BF16),
            jax.ShapeDtypeStruct((b, s, W_D), BF16),
            jax.ShapeDtypeStruct((b, s, W_G), BF16),
            jax.ShapeDtypeStruct((b, s, W_CMP), F32),
            jax.ShapeDtypeStruct((b, 8, s), F32),
        ],
        scratch_shapes=[pltpu.VMEM((8, LANES), F32)],
        compiler_params=_cparams(("arbitrary", "arbitrary")),
        name="in_projection",
    )(x, mod_l, g.reshape(1, d), wp, wf, bf8)
    return outs


def _flash_kernel(kind, t, layer_idx, *refs):
    it = iter(refs)
    q_ref, k_ref, v_ref = next(it), next(it), next(it)
    fc_ref = next(it) if kind == "C" else None
    un_ref, use_ref = (next(it), next(it)) if kind == "Dsel" else (None, None)
    lam_ref, sg_ref = (next(it), next(it)) if kind == "A" else (None, None)
    o_ref, m_scr, acc_scr = next(it), next(it), next(it)

    p = pl.program_id(1)
    i = pl.program_id(2)
    nstream = 4 if kind == "A" else 2
    width = LANES // nstream
    mixer = {"A": 0, "Dsel": 2}.get(kind)

    q2 = q_ref[0]
    lane = lax.broadcasted_iota(jnp.int32, (t, LANES), 1)
    zero = jnp.zeros_like(q2)
    qms = [jnp.where((lane >= s * width) & (lane < (s + 1) * width), q2, zero)
           for s in range(nstream)]
    if kind == "Dsel":
        qms = [jnp.concatenate([qm, un_ref[0]], axis=1) for qm in qms]
    qcat = jnp.concatenate(qms, axis=0)

    def head_of(s):
        return s // 2 if kind == "A" else s

    slopes = None
    if mixer is not None:
        sl = SLOPES[mixer] * LOG2E
        slopes = [jnp.where(p == 0, float(sl[head_of(s)]), float(sl[2 + head_of(s)]))
                  for s in range(nstream)]

    for s in range(nstream):
        m_scr[s] = jnp.full((t, LANES), NEG_INF, F32)
        acc_scr[s] = jnp.zeros((t, LANES), F32)

    def tile(first_key, nk, diag):
        start = pl.multiple_of(first_key, t)
        k2 = k_ref[0, pl.ds(start, nk), :]
        v2 = v_ref[0, pl.ds(start, nk), :]
        vlane = lax.broadcasted_iota(jnp.int32, (nk, LANES), 1)
        one = jnp.ones_like(v2)
        vhalf = [jnp.where(vlane < HEAD_DIM, v2, one), jnp.where(vlane < HEAD_DIM, one, v2)]
        col = lax.broadcasted_iota(jnp.int32, (1, nk), 1)
        rel = (start - i * t + col).astype(F32)
        if diag:
            mask = (lax.broadcasted_iota(jnp.int32, (t, t), 0)
                    >= lax.broadcasted_iota(jnp.int32, (t, t), 1))
        sc_all = _dot_nt(qcat, k2)
        for s in range(nstream):
            sc = sc_all[s * t:(s + 1) * t]
            if slopes is not None:
                sc = sc + slopes[s] * rel
            if kind == "C":
                sc = sc - LOG2E * fc_ref[0, pl.ds(2 * p + s, 1), pl.ds(start, nk)]
            if diag:
                sc = jnp.where(mask, sc, NEG_INF)
            m_prev = m_scr[s]
            m_next = jnp.maximum(m_prev, jnp.max(sc, axis=1, keepdims=True))
            alpha = jnp.exp2(m_prev - m_next)
            pexp = jnp.exp2((sc - jnp.tile(m_next, (1, nk // LANES))).astype(BF16))
            acc_scr[s] = alpha * acc_scr[s] + _dot(pexp, vhalf[head_of(s)])
            m_scr[s] = m_next

    tile(i * t, t, True)

    if kind == "Dsel":
        nq = pl.num_programs(2)
        base = (pl.program_id(0) * nq + i) * nq

        def body(j, carry):
            @pl.when(use_ref[base + j] != 0)
            def _():
                tile(j * t, t, False)
            return carry

        lax.fori_loop(0, i, body, 0)
    else:
        wide = WIDE_TILES * t

        def body(j, carry):
            tile(j * wide, wide, False)
            return carry

        lax.fori_loop(0, i // WIDE_TILES, body, 0)

        def rest(j, carry):
            tile(j * t, t, False)
            return carry

        lax.fori_loop(i // WIDE_TILES * WIDE_TILES, i, rest, 0)

    outs = [acc_scr[s] / pltpu.roll(acc_scr[s], HEAD_DIM, 1) for s in range(nstream)]
    if kind == "A":
        lam_init = 0.8 - 0.6 * math.exp(-0.3 * layer_idx)
        t1 = jnp.sum(lam_ref[0:1, :] * lam_ref[1:2, :], axis=1, keepdims=True)
        t2 = jnp.sum(lam_ref[2:3, :] * lam_ref[3:4, :], axis=1, keepdims=True)
        lam = jnp.exp(t1) - jnp.exp(t2) + lam_init
        d0 = outs[0] - lam * outs[1]
        d1 = outs[2] - lam * outs[3]
        lo_half = lane < HEAD_DIM
        o = jnp.where(lo_half, d0, d1)
        sq = o * o
        ss_lo = jnp.sum(jnp.where(lo_half, sq, 0.0), axis=1, keepdims=True)
        ss_hi = jnp.sum(jnp.where(lo_half, 0.0, sq), axis=1, keepdims=True)
        ms = jnp.where(lo_half, ss_lo, ss_hi) * (1.0 / HEAD_DIM)
        o = o * lax.rsqrt(ms + EPS) * sg_ref[...] * (1.0 - lam_init)
    else:
        o = jnp.where(lane < HEAD_DIM, outs[0], outs[1])
    o_ref[0] = o.astype(o_ref.dtype)


def _flash(kind, t, layer_idx, q_arr, qoff, k_arr, koff, v_arr, voff, extras):
    b, s, _ = q_arr.shape
    nstream = 4 if kind == "A" else 2
    shared_kv = kind == "Dsel"
    kw = k_arr.shape[-1] if shared_kv else LANES
    kv_idx = (lambda off: (lambda bi, p, i: (bi, 0, off))) if shared_kv else \
             (lambda off: (lambda bi, p, i: (bi, 0, off + p)))
    in_specs = [
        pl.BlockSpec((1, t, LANES), lambda bi, p, i: (bi, i, qoff + p)),
        pl.BlockSpec((1, s, kw), kv_idx(koff)),
        pl.BlockSpec((1, s, LANES), kv_idx(voff)),
    ]
    args = [q_arr, k_arr, v_arr]
    if kind == "C":
        fcum, = extras
        in_specs.append(pl.BlockSpec((1, 8, s), lambda bi, p, i: (bi, 0, 0)))
        args.append(fcum)
    elif kind == "Dsel":
        unsel, = extras
        nsel = unsel.shape[-1]
        nq, per = s // t, t // SEL_BLOCK
        used = (unsel.reshape(b, nq, t, nq, per) == 0).any(axis=(2, 4)).astype(jnp.int32).reshape(-1)
        in_specs.append(pl.BlockSpec((1, t, nsel), lambda bi, p, i: (bi, i, 0)))
        in_specs.append(pl.BlockSpec(memory_space=pltpu.SMEM))
        args += [unsel, used]
    elif kind == "A":
        lam8, sg = extras
        in_specs.append(pl.BlockSpec((8, LANES), lambda bi, p, i: (0, 0)))
        in_specs.append(pl.BlockSpec((1, LANES), lambda bi, p, i: (0, 0)))
        args += [lam8, sg]
    return pl.pallas_call(
        functools.partial(_flash_kernel, kind, t, layer_idx),
        grid=(b, 2, s // t),
        in_specs=in_specs,
        out_specs=pl.BlockSpec((1, t, LANES), lambda bi, p, i: (bi, i, p)),
        out_shape=jax.ShapeDtypeStruct((b, s, 2 * LANES), BF16),
        scratch_shapes=[pltpu.VMEM((nstream, t, LANES), F32)] * 2,
        compiler_params=_cparams(("arbitrary", "arbitrary", "arbitrary")),
        name="flash_" + kind,
    )(*args)


def _window_kernel(kind, t, wpad, *refs):
    it = iter(refs)
    q_ref, k_ref, v_ref = next(it), next(it), next(it)
    sink_ref = next(it) if kind == "B" else None
    o_ref = next(it)
    i = pl.program_id(1)
    window = SWA_WINDOW if kind == "B" else NSA_WINDOW
    sl = SLOPES[1 if kind == "B" else 2] * LOG2E
    nk = wpad + t
    start = pl.multiple_of(jnp.maximum(i * t - wpad, 0), LANES)
    lane = lax.broadcasted_iota(jnp.int32, (t, LANES), 1)
    vlane = lax.broadcasted_iota(jnp.int32, (nk, LANES), 1)
    dist = ((i * t - start) + lax.broadcasted_iota(jnp.int32, (t, nk), 0)
            - lax.broadcasted_iota(jnp.int32, (t, nk), 1))
    valid = (dist >= 0) & (dist < window)
    rel = (start - i * t + lax.broadcasted_iota(jnp.int32, (1, nk), 1)).astype(F32)
    qrow = lax.broadcasted_iota(jnp.int32, (t, 1), 0).astype(F32)

    def masked_q(pair, g):
        q2 = q_ref[0, :, pair * LANES:(pair + 1) * LANES]
        return jnp.where((lane >= g * HEAD_DIM) & (lane < (g + 1) * HEAD_DIM), q2, jnp.zeros_like(q2))

    def kv_tiles(pair):
        cols = slice(pair * LANES, (pair + 1) * LANES) if kind == "B" else slice(0, LANES)
        k2 = k_ref[0, pl.ds(start, nk), cols]
        v2 = v_ref[0, pl.ds(start, nk), cols]
        one = jnp.ones_like(v2)
        return k2, [jnp.where(vlane < HEAD_DIM, v2, one), jnp.where(vlane < HEAD_DIM, one, v2)]

    groups = [[(0, 0), (0, 1)], [(1, 0), (1, 1)]] if kind == "B" else [[(0, 0), (0, 1), (1, 0), (1, 1)]]
    outs = {}
    for group in groups:
        k2, vhalf = kv_tiles(group[0][0])
        sc_all = _dot_nt(jnp.concatenate([masked_q(pair, g) for pair, g in group], axis=0), k2)
        for n, (pair, g) in enumerate(group):
            h = 2 * pair + g
            slope = float(sl[h])
            sc = jnp.where(valid, sc_all[n * t:(n + 1) * t] + slope * rel, NEG_INF)
            m = jnp.max(sc, axis=1, keepdims=True)
            if kind == "B":
                sink = LOG2E * sink_ref[h] + slope * qrow
                m = jnp.maximum(m, sink)
            pv = _dot(jnp.exp2((sc - m).astype(BF16)), vhalf[g])
            l = pltpu.roll(pv, HEAD_DIM, 1)
            if kind == "B":
                l = l + jnp.exp2(sink - m)
            outs[h] = pv / l
    for pair in range(2):
        o_ref[0, :, pair * LANES:(pair + 1) * LANES] = jnp.where(
            lane < HEAD_DIM, outs[2 * pair], outs[2 * pair + 1]).astype(o_ref.dtype)


def _window(kind, t, wpad, q_arr, k_arr, koff, v_arr, voff, extras):
    b, s, _ = q_arr.shape
    kw = 2 * LANES if kind == "B" else LANES
    in_specs = [
        pl.BlockSpec((1, t, 2 * LANES), lambda bi, i: (bi, i, 0)),
        pl.BlockSpec((1, s, kw), lambda bi, i: (bi, 0, koff)),
        pl.BlockSpec((1, s, kw), lambda bi, i: (bi, 0, voff)),
    ]
    args = [q_arr, k_arr, v_arr]
    if kind == "B":
        in_specs.append(pl.BlockSpec(memory_space=pltpu.SMEM))
        args.append(extras[0])
    return pl.pallas_call(
        functools.partial(_window_kernel, kind, t, wpad),
        grid=(b, s // t),
        in_specs=in_specs,
        out_specs=pl.BlockSpec((1, t, 2 * LANES), lambda bi, i: (bi, i, 0)),
        out_shape=jax.ShapeDtypeStruct((b, s, 2 * LANES), BF16),
        compiler_params=_cparams(("arbitrary", "arbitrary")),
        name="window_" + kind,
    )(*args)


def _gelu_tanh(x):
    return 0.5 * x * (1.0 + jnp.tanh(math.sqrt(2.0 / math.pi) * (x + 0.044715 * x * x * x)))


def _compress_kernel(t_ref, w1_ref, pos_ref, b1_ref, w2_ref, o_ref):
    tr = t_ref[0, 0]
    half = tr.shape[1]
    n = tr.shape[0]
    w1 = w1_ref[0]
    u = _dot_f32(tr, w1[:half])
    v = _dot_f32(tr, w1[half:])
    cpos = _dot_f32(pos_ref[0], w1)[0:1] + b1_ref[0]
    hid = u + pltpu.roll(v, n - 1, 0) + cpos
    o_ref[0, 0] = _dot_f32(_gelu_tanh(hid), w2_ref[0])


def _compress(kv_r, w1, pos8, b1, w2d):
    two, b, n, dd = kv_r.shape
    return pl.pallas_call(
        _compress_kernel,
        grid=(two, b),
        in_specs=[
            pl.BlockSpec((1, 1, n, dd), lambda c, bi: (c, bi, 0, 0)),
            pl.BlockSpec((1, 2 * dd, CMP_HIDDEN), lambda c, bi: (c, 0, 0)),
            pl.BlockSpec((1, 8, 2 * dd), lambda c, bi: (c, 0, 0)),
            pl.BlockSpec((1, 1, CMP_HIDDEN), lambda c, bi: (c, 0, 0)),
            pl.BlockSpec((1, CMP_HIDDEN, LANES), lambda c, bi: (c, 0, 0)),
        ],
        out_specs=pl.BlockSpec((1, 1, n, LANES), lambda c, bi: (c, bi, 0, 0)),
        out_shape=jax.ShapeDtypeStruct((two, b, n, LANES), F32),
        compiler_params=_cparams(("arbitrary", "arbitrary")),
        name="nsa_compress",
    )(kv_r, w1, pos8, b1, w2d)


def _cmp_kernel(t, q_ref, kc_ref, vc_ref, cov_ref, o_ref, un_ref):
    i = pl.program_id(1)
    ncp = kc_ref.shape[2]
    nsel = cov_ref.shape[0]
    kc = kc_ref[0, 0]
    vc = vc_ref[0, 0]
    kch, kcl = _split_bf16(kc)
    vcb = vc.astype(BF16)
    lane = lax.broadcasted_iota(jnp.int32, (t, LANES), 1)
    tq = i * t + lax.broadcasted_iota(jnp.int32, (t, ncp), 0)
    nidx = lax.broadcasted_iota(jnp.int32, (t, ncp), 1)
    cmp_end = nidx * CMP_STRIDE + (CMP_BLOCK - 1)
    valid = (tq >= cmp_end) & (nidx < ncp - 1)
    end_row = cmp_end[0:1, :].astype(F32)
    psum = jnp.zeros((t, ncp), F32)
    outs = []
    for pair in range(2):
        q2 = q_ref[0, :, pair * LANES:(pair + 1) * LANES]
        for g in range(2):
            h = 2 * pair + g
            qm = jnp.where((lane >= g * HEAD_DIM) & (lane < (g + 1) * HEAD_DIM), q2, jnp.zeros_like(q2))
            sc = _dot_nt(qm, kch) + _dot_nt(qm, kcl) + float(SLOPES[2][h] * LOG2E) * end_row
            sc = jnp.where(valid, sc, NEG_INF)
            m = jnp.max(sc, axis=1, keepdims=True)
            pe = jnp.exp2(sc - m)
            l = jnp.sum(pe, axis=1, keepdims=True)
            pc = pe * jnp.where(m > 0.5 * NEG_INF, 1.0 / l, 0.0)
            psum = psum + pc
            outs.append(_dot(pc.astype(BF16), vcb))
    lo_half = lane < HEAD_DIM
    o_ref[0, :, 0:LANES] = jnp.where(lo_half, outs[0], outs[1]).astype(o_ref.dtype)
    o_ref[0, :, LANES:2 * LANES] = jnp.where(lo_half, outs[2], outs[3]).astype(o_ref.dtype)
    ph, plo = _split_bf16(psum)
    cov = cov_ref[...]
    imp = _dot_nt(cov, ph) + _dot_nt(cov, plo)
    blk = lax.broadcasted_iota(jnp.int32, (nsel, t), 0)
    cur = (i * t + lax.broadcasted_iota(jnp.int32, (nsel, t), 1)) // SEL_BLOCK
    forced = (blk == 0) | (blk == cur) | (blk == cur - 1)
    score = jnp.where(forced, FORCE_SCORE, jnp.where(blk <= cur, imp, -1.0))
    unsel = jnp.ones((nsel, t), F32)
    for _ in range(min(SEL_TOPK, nsel)):
        mx = jnp.max(score, axis=0, keepdims=True)
        idx = jnp.min(jnp.where(score == mx, blk, nsel), axis=0, keepdims=True)
        hit = blk == idx
        unsel = jnp.where(hit, 0.0, unsel)
        score = jnp.where(hit, -3.0, score)
    un_ref[0] = unsel.T.astype(un_ref.dtype)


def _cmp_attention(t, qd, kvc, cover_t):
    b, s, _ = qd.shape
    ncp = kvc.shape[2]
    nsel = cover_t.shape[0]
    return pl.pallas_call(
        functools.partial(_cmp_kernel, t),
        grid=(b, s // t),
        in_specs=[
            pl.BlockSpec((1, t, 2 * LANES), lambda bi, i: (bi, i, 0)),
            pl.BlockSpec((1, 1, ncp, LANES), lambda bi, i: (0, bi, 0, 0)),
            pl.BlockSpec((1, 1, ncp, LANES), lambda bi, i: (1, bi, 0, 0)),
            pl.BlockSpec((nsel, ncp), lambda bi, i: (0, 0)),
        ],
        out_specs=[
            pl.BlockSpec((1, t, 2 * LANES), lambda bi, i: (bi, i, 0)),
            pl.BlockSpec((1, t, nsel), lambda bi, i: (bi, i, 0)),
        ],
        out_shape=[
            jax.ShapeDtypeStruct((b, s, 2 * LANES), BF16),
            jax.ShapeDtypeStruct((b, s, nsel), BF16),
        ],
        compiler_params=_cparams(("arbitrary", "arbitrary")),
        name="nsa_cmp_select",
    )(qd, kvc, kvc, cover_t)


def _cover_matrix(s):
    ncp, nsel = s // CMP_STRIDE, s // SEL_BLOCK
    cs = np.arange(ncp)[:, None] * CMP_STRIDE
    ss = np.arange(nsel)[None, :] * SEL_BLOCK
    cov = np.clip(np.minimum(cs + CMP_BLOCK, ss + SEL_BLOCK) - np.maximum(cs, ss), 0, None) / CMP_BLOCK
    cov[ncp - 1] = 0.0
    return jnp.asarray(cov.T, BF16)


def _expand_neg(s):
    nsel = s // SEL_BLOCK
    e = (np.arange(s)[:, None] // SEL_BLOCK) == np.arange(nsel)[None, :]
    return jnp.asarray(np.where(e, -MASK_BIG, 0.0), BF16)


def _outproj_kernel(ya, yb, yc, ocmp, osel, owin, gt, gx_ref, x_ref, mod_ref, wo_ref, g_ref,
                    rwt_ref, rb_ref, xo_ref, h_ref, ri_ref, rg_ref, cnt_ref, carry):
    first = (pl.program_id(0) == 0) & (pl.program_id(1) == 0)
    gw = GROUP_WIDTH
    g = _dot(gt[0], gx_ref[...])
    yd = (g[:, 0:gw] * ocmp[0].astype(F32) + g[:, gw:2 * gw] * osel[0].astype(F32)
          + g[:, 2 * gw:3 * gw] * owin[0].astype(F32))
    ycat = jnp.concatenate([ya[0], yb[0], yc[0], yd.astype(BF16)], axis=1)
    y = _dot(ycat, wo_ref[...])
    x = x_ref[0] + mod_ref[2:3, :] * y
    xo_ref[0] = x
    ms = jnp.mean(x * x, axis=-1, keepdims=True)
    h = x * lax.rsqrt(ms + EPS) * g_ref[...]
    h = h * (1.0 + mod_ref[4:5, :]) + mod_ref[3:4, :]
    h_ref[0] = h
    tm = h.shape[0]
    ng = LANES // EXPERTS_PER_GROUP
    big = 4 * LANES
    logits = _dot_f32(rwt_ref[...], h, nt=True)
    eidx = lax.broadcasted_iota(jnp.int32, (LANES, tm), 0)
    aff = jax.nn.sigmoid(logits)
    sel = jnp.where(eidx < N_EXPERTS, aff + rb_ref[:, 0:1], NEG_INF)
    sel3 = sel.reshape(ng, EXPERTS_PER_GROUP, tm)
    e3 = eidx.reshape(ng, EXPERTS_PER_GROUP, tm)
    g1 = jnp.max(sel3, axis=1, keepdims=True)
    i1 = jnp.min(jnp.where(sel3 == g1, e3, big), axis=1, keepdims=True)
    sel_b = jnp.where(e3 == i1, NEG_INF, sel3)
    g2 = jnp.max(sel_b, axis=1, keepdims=True)
    i2 = jnp.min(jnp.where(sel_b == g2, e3, big), axis=1, keepdims=True)
    gs = g1 + g2
    gidx = lax.broadcasted_iota(jnp.int32, (ng, 1, tm), 0)
    gm = jnp.max(gs, axis=0, keepdims=True)
    best = gidx == jnp.min(jnp.where(gs == gm, gidx, big), axis=0, keepdims=True)
    e1 = jnp.min(jnp.where(best, i1, big), axis=0)
    e2 = jnp.min(jnp.where(best, i2, big), axis=0)
    oh1 = eidx == e1
    oh2 = eidx == e2
    a1 = jnp.sum(jnp.where(oh1, aff, 0.0), axis=0, keepdims=True)
    a2 = jnp.sum(jnp.where(oh2, aff, 0.0), axis=0, keepdims=True)
    inv = 1.0 / (a1 + a2)
    @pl.when(first)
    def _():
        carry[...] = jnp.zeros_like(carry)

    ohs = jnp.where(oh1 | oh2, 1.0, 0.0)
    rr = lax.broadcasted_iota(jnp.int32, (tm, tm), 0)
    cc = lax.broadcasted_iota(jnp.int32, (tm, tm), 1)
    earlier = jnp.where(rr < cc, 1.0, 0.0).astype(BF16)
    before = _dot(ohs.astype(BF16), earlier) + carry[:, 0:1]
    r1 = jnp.sum(jnp.where(oh1, before, 0.0), axis=0, keepdims=True)
    r2 = jnp.sum(jnp.where(oh2, before, 0.0), axis=0, keepdims=True)
    total = carry[:, 0:1] + jnp.sum(ohs, axis=1, keepdims=True)
    carry[...] = jnp.broadcast_to(total, carry.shape)
    cnt_ref[...] = jnp.broadcast_to(total, cnt_ref.shape).astype(jnp.int32)
    row = lax.broadcasted_iota(jnp.int32, (8, tm), 0)
    ri_ref[0] = jnp.where(row == 0, e1, jnp.where(row == 1, e2, jnp.where(
        row == 2, r1.astype(jnp.int32), jnp.where(row == 3, r2.astype(jnp.int32), 0))))
    rg_ref[0] = jnp.where(row == 0, a1 * inv, jnp.where(row == 1, a2 * inv, 0.0))


def _gate_expander():
    e = np.zeros((W_G, 3 * GROUP_WIDTH), np.float32)
    for h in range(HEADS):
        for r in range(3):
            e[h * 3 + r, r * GROUP_WIDTH + h * HEAD_DIM:r * GROUP_WIDTH + (h + 1) * HEAD_DIM] = 1.0
    return jnp.asarray(e, BF16)


def _out_projection(ya, yb, yc, ocmp, osel, owin, gates, x, mod_l, wo, g, rwt, rb, tm):
    b, s, d = x.shape
    gw = GROUP_WIDTH
    tok = lambda w: pl.BlockSpec((1, tm, w), lambda bi, i: (bi, i, 0))
    full = lambda shp: pl.BlockSpec(shp, lambda bi, i: (0,) * len(shp))
    rowblk = pl.BlockSpec((1, 8, tm), lambda bi, i: (bi, 0, i))
    return pl.pallas_call(
        _outproj_kernel,
        grid=(b, s // tm),
        in_specs=[tok(gw)] * 6 + [tok(W_G), full((W_G, 3 * gw)), tok(d),
                                  pl.BlockSpec((None, 6, d), lambda bi, i: (bi, 0, 0)),
                                  full((d, d)), full((1, d)), full((LANES, d)), full((LANES, 1))],
        out_specs=[tok(d), tok(d), rowblk, rowblk, full((LANES, LANES))],
        out_shape=[
            jax.ShapeDtypeStruct((b, s, d), F32),
            jax.ShapeDtypeStruct((b, s, d), F32),
            jax.ShapeDtypeStruct((b, 8, s), jnp.int32),
            jax.ShapeDtypeStruct((b, 8, s), F32),
            jax.ShapeDtypeStruct((LANES, LANES), jnp.int32),
        ],
        scratch_shapes=[pltpu.VMEM((LANES, LANES), F32)],
        compiler_params=_cparams(("arbitrary", "arbitrary")),
        name="out_projection_router",
    )(ya, yb, yc, ocmp, osel, owin, gates, _gate_expander(), x, mod_l, wo, g.reshape(1, d), rwt, rb)


def _dispatch_kernel(p0_ref, p1_ref, h_ref, xs_ref, sem):
    tm = h_ref.shape[0]

    def issue(r, c):
        for pos_ref in (p0_ref, p1_ref):
            pltpu.make_async_copy(h_ref.at[pl.ds(r, 1), :],
                                  xs_ref.at[pl.ds(pos_ref[r], 1), :], sem).start()
        return c

    lax.fori_loop(0, tm, issue, 0, unroll=4)
    for _ in range(2):
        pltpu.make_async_copy(h_ref, xs_ref.at[pl.ds(0, tm), :], sem).wait()


def _dispatch(h, pos0, pos1, tm):
    t, d = h.shape
    return pl.pallas_call(
        _dispatch_kernel,
        grid=(t // tm,),
        in_specs=[
            pl.BlockSpec((tm,), lambda i: (i,), memory_space=pltpu.SMEM),
            pl.BlockSpec((tm,), lambda i: (i,), memory_space=pltpu.SMEM),
            pl.BlockSpec((tm, d), lambda i: (i, 0)),
        ],
        out_specs=pl.BlockSpec(memory_space=pl.ANY),
        out_shape=jax.ShapeDtypeStruct((2 * t, d), F32),
        scratch_shapes=[pltpu.SemaphoreType.DMA(())],
        compiler_params=_cparams(("arbitrary",)),
        name="moe_dispatch",
    )(pos0, pos1, h)


def _expert_kernel(layer, vb_ref, ve_ref, vlo_ref, vhi_ref, vord_ref, vnext_ref, nv_ref,
                   xs_ref, wg_hbm, wu_hbm, wd_hbm, ys_ref, wgf, wuf, wdf, wgb, wub, wdb, sem):
    v = pl.program_id(0)
    rows = xs_ref.shape[0]
    prev = jnp.maximum(v - 1, 0)
    new_expert = (v == 0) | (ve_ref[v] != ve_ref[prev])
    new_block = (v == 0) | (vb_ref[v] != vb_ref[prev])

    def weight_copies(e, slot):
        return [pltpu.make_async_copy(src.at[layer, e], dst.at[slot], sem.at[slot, n])
                for n, (src, dst) in enumerate(((wg_hbm, wgf), (wu_hbm, wuf), (wd_hbm, wdf)))]

    @pl.when(v < nv_ref[0])
    def _():
        @pl.when(new_expert)
        def _():
            slot = vord_ref[v] % 2

            @pl.when(v == 0)
            def _():
                for c in weight_copies(ve_ref[v], slot):
                    c.start()

            @pl.when(vnext_ref[v] >= 0)
            def _():
                for c in weight_copies(vnext_ref[v], 1 - slot):
                    c.start()

            for c in weight_copies(ve_ref[v], slot):
                c.wait()
            wgb[...] = wgf[slot].astype(BF16)
            wub[...] = wuf[slot].astype(BF16)
            wdb[...] = wdf[slot].astype(BF16)

        x = xs_ref[...].astype(BF16)
        hg = _dot(x, wgb[...])
        hu = _dot(x, wub[...])
        hm = (hg * jax.nn.sigmoid(hg) * hu).astype(BF16)
        y = _dot(hm, wdb[...])
        r = vb_ref[v] * rows + lax.broadcasted_iota(jnp.int32, (rows, 1), 0)
        mine = (r >= vlo_ref[v]) & (r < vhi_ref[v])

        @pl.when(new_block)
        def _():
            ys_ref[...] = jnp.where(mine, y, 0.0)

        @pl.when(jnp.logical_not(new_block))
        def _():
            ys_ref[...] = jnp.where(mine, y, ys_ref[...])


def _experts(xs, meta, layer, wg, wu, wd, rows, nvmax):
    n, d = xs.shape
    de = wg.shape[3]
    blk = lambda v, vb, ve, lo, hi, vo, vn, nv: (vb[v], 0)
    grid_spec = pltpu.PrefetchScalarGridSpec(
        num_scalar_prefetch=7,
        grid=(nvmax,),
        in_specs=[
            pl.BlockSpec((rows, d), blk),
            pl.BlockSpec(memory_space=pl.ANY),
            pl.BlockSpec(memory_space=pl.ANY),
            pl.BlockSpec(memory_space=pl.ANY),
        ],
        out_specs=pl.BlockSpec((rows, d), blk),
        scratch_shapes=[pltpu.VMEM((2, d, de), F32), pltpu.VMEM((2, d, de), F32), pltpu.VMEM((2, de, d), F32),
                        pltpu.VMEM((d, de), BF16), pltpu.VMEM((d, de), BF16), pltpu.VMEM((de, d), BF16),
                        pltpu.SemaphoreType.DMA((2, 3))],
    )
    return pl.pallas_call(
        functools.partial(_expert_kernel, layer),
        grid_spec=grid_spec,
        out_shape=jax.ShapeDtypeStruct((n, d), F32),
        compiler_params=_cparams(("arbitrary",)),
        name="moe_experts",
    )(*meta, xs, wg, wu, wd)


def _combine_kernel(final, p0_ref, p1_ref, x_ref, rg_ref, mod_ref, g_ref, ys_ref, o_ref, buf, sem):
    tm = x_ref.shape[0]

    def issue(r, c):
        for k, pos_ref in enumerate((p0_ref, p1_ref)):
            pltpu.make_async_copy(ys_ref.at[pl.ds(pos_ref[r], 1), :],
                                  buf.at[k, pl.ds(r, 1), :], sem).start()
        return c

    lax.fori_loop(0, tm, issue, 0, unroll=4)
    for k in range(2):
        pltpu.make_async_copy(ys_ref.at[pl.ds(0, tm), :], buf.at[k], sem).wait()
    rg = rg_ref[...]
    y = rg[:, 0:1] * buf[0] + rg[:, 1:2] * buf[1]
    x = x_ref[...] + mod_ref[5:6, :] * y
    if final:
        ms = jnp.mean(x * x, axis=-1, keepdims=True)
        x = x * lax.rsqrt(ms + EPS) * g_ref[...]
    o_ref[...] = x


def _combine(final, pos0, pos1, x, rg, mod_l, g, ys, tm):
    t, d = x.shape
    per_b = t // mod_l.shape[0] // tm
    return pl.pallas_call(
        functools.partial(_combine_kernel, final),
        grid=(t // tm,),
        in_specs=[
            pl.BlockSpec((tm,), lambda i: (i,), memory_space=pltpu.SMEM),
            pl.BlockSpec((tm,), lambda i: (i,), memory_space=pltpu.SMEM),
            pl.BlockSpec((tm, d), lambda i: (i, 0)),
            pl.BlockSpec((tm, 8), lambda i: (i, 0)),
            pl.BlockSpec((None, 6, d), lambda i: (i // per_b, 0, 0)),
            pl.BlockSpec((1, d), lambda i: (0, 0)),
            pl.BlockSpec(memory_space=pl.ANY),
        ],
        out_specs=pl.BlockSpec((tm, d), lambda i: (i, 0)),
        out_shape=jax.ShapeDtypeStruct((t, d), F32),
        scratch_shapes=[pltpu.VMEM((2, tm, d), F32), pltpu.SemaphoreType.DMA(())],
        compiler_params=_cparams(("arbitrary",)),
        name="moe_combine",
    )(pos0, pos1, x, rg, mod_l, g.reshape(1, d), ys)


def _visit_plan(counts, rows, nblocks):
    ne = counts.shape[0]
    nvmax = nblocks + ne - 1
    ends = jnp.cumsum(counts)
    offs = ends - counts
    b_lo = offs // rows
    b_hi = jnp.maximum(ends - 1, 0) // rows
    nvis = jnp.where(counts > 0, b_hi - b_lo + 1, 0)
    vend = jnp.cumsum(nvis)
    vstart = vend - nvis
    nv = vend[-1]
    v = jnp.minimum(jnp.arange(nvmax), nv - 1)
    e = jnp.sum((v[:, None] >= vend[None, :]).astype(jnp.int32), axis=1)
    onehot = e[:, None] == jnp.arange(ne)[None, :]
    pick = lambda a: jnp.sum(jnp.where(onehot, a[None, :], 0), axis=1)
    blk = pick(b_lo) + v - pick(vstart)
    lo = jnp.maximum(pick(offs), blk * rows)
    hi = jnp.minimum(pick(ends), (blk + 1) * rows)
    order = jnp.cumsum((counts > 0).astype(jnp.int32)) - 1
    ids = jnp.where(counts > 0, jnp.arange(ne), ne)
    later = jnp.concatenate([jnp.flip(lax.cummin(jnp.flip(ids)))[1:], jnp.full((1,), ne, ids.dtype)])
    nxt = jnp.where(later < ne, later, -1)
    i32 = lambda a: a.astype(jnp.int32)
    return (i32(blk), i32(e), i32(lo), i32(hi), i32(pick(order)), i32(pick(nxt)),
            i32(nv).reshape(1)), offs, nvmax


def _tiles(s):
    tm = min(512, s)
    return dict(tm=tm, t_full=min(512, s), t_win=min(256, s), t_cmp=min(256, s),
                tm_moe=min(512, s), rows=256)


def _layer(l, x, mod_l, p, consts, final):
    b, s, d = x.shape
    tl = _tiles(s)
    wp, wf = _prep_w_in(p["w_in"])
    qa, qb, qc, qd, gates, kvcmp, fcum = _in_projection(
        x, mod_l, p["norm_attn_g"], wp, wf, p["fox_forget_b"], tl["tm"])
    lam8 = jnp.zeros((8, LANES), F32).at[:4, :DIFF_DK].set(
        jnp.stack([p["diff_lam_q1"], p["diff_lam_k1"], p["diff_lam_q2"], p["diff_lam_k2"]]))
    sg = jnp.tile(p["diff_subln_g"], 2).reshape(1, LANES)
    ya = _flash("A", tl["t_full"], l, qa, 0, qa, 2, qa, 4, (lam8, sg))
    yb = _window("B", tl["t_win"], SWA_WINDOW, qb, qb, 1, qb, 2, (p["swa_sinks"],))
    yc = _flash("C", tl["t_full"], l, qc, 0, qc, 2, qc, 4, (fcum,))
    n16 = s // CMP_STRIDE
    kv_r = jnp.stack([kvcmp[..., :HEAD_DIM].reshape(b, n16, CMP_STRIDE * HEAD_DIM),
                      kvcmp[..., HEAD_DIM:].reshape(b, n16, CMP_STRIDE * HEAD_DIM)])
    pos8 = jnp.zeros((2, 8, CMP_BLOCK * HEAD_DIM), F32).at[:, 0].set(
        p["nsa_cmp_pos"].reshape(2, CMP_BLOCK * HEAD_DIM))
    w2d = jnp.concatenate([p["nsa_cmp_w2"], p["nsa_cmp_w2"]], axis=-1)
    kvc = _compress(kv_r, p["nsa_cmp_w1"], pos8, p["nsa_cmp_b1"].reshape(2, 1, CMP_HIDDEN), w2d)
    ocmp, unsel = _cmp_attention(tl["t_cmp"], qd, kvc, consts["cover"])
    ksel = jnp.concatenate([qd[..., 2 * LANES:3 * LANES],
                            jnp.broadcast_to(consts["eneg"][None], (b,) + consts["eneg"].shape)], axis=-1)
    osel = _flash("Dsel", tl["t_full"], l, qd, 0, ksel, 0, qd, 3, (unsel,))
    owin = _window("Dwin", tl["t_win"], NSA_WINDOW, qd, qd, 4, qd, 5, ())
    rwt = jnp.zeros((LANES, d), F32).at[:N_EXPERTS].set(p["router_w"].T)
    rb = jnp.zeros((LANES, 1), F32).at[:N_EXPERTS, 0].set(p["router_b"])
    xm, h2, ri, rg, cnt = _out_projection(
        ya, yb, yc, ocmp, osel, owin, gates, x, mod_l, p["w_out"].astype(BF16),
        p["norm_moe_g"], rwt, rb, tl["tm"])
    t = b * s
    counts = cnt[:N_EXPERTS, 0]
    rows = tl["rows"]
    meta, offs, nvmax = _visit_plan(counts, rows, 2 * t // rows)
    onehot = ri[:, 0:2, :, None] == jnp.arange(N_EXPERTS)[None, None, None, :]
    pos = (jnp.sum(jnp.where(onehot, offs[None, None, None, :], 0), axis=-1) + ri[:, 2:4]).astype(jnp.int32)
    pos0, pos1 = pos[:, 0].reshape(t), pos[:, 1].reshape(t)
    rgt = rg.transpose(0, 2, 1).reshape(t, 8)
    xs = _dispatch(h2.reshape(t, d), pos0, pos1, tl["tm_moe"])
    ys = _experts(xs, meta, l, p["exp_w_gate"], p["exp_w_up"], p["exp_w_down"], rows, nvmax)
    out = _combine(final, pos0, pos1, xm.reshape(t, d), rgt, mod_l, p["norm_final_g"], ys, tl["tm_moe"])
    return out.reshape(b, s, d)


def kernel(x, c, ada_w, ada_b, norm_attn_g, norm_moe_g, norm_final_g, w_in, w_out, diff_lam_q1, diff_lam_k1, diff_lam_q2, diff_lam_k2, diff_subln_g, swa_sinks, fox_forget_b, nsa_cmp_pos, nsa_cmp_w1, nsa_cmp_b1, nsa_cmp_w2, router_w, router_b, exp_w_gate, exp_w_up, exp_w_down):
    depth = ada_w.shape[0]
    s = x.shape[1]
    mod = _modulation(c, ada_w, ada_b)
    consts = dict(cover=_cover_matrix(s), eneg=_expand_neg(s))
    for l in range(depth):
        p = dict(
            norm_attn_g=norm_attn_g[l], norm_moe_g=norm_moe_g[l], norm_final_g=norm_final_g,
            w_in=w_in[l], w_out=w_out[l],
            diff_lam_q1=diff_lam_q1[l], diff_lam_k1=diff_lam_k1[l],
            diff_lam_q2=diff_lam_q2[l], diff_lam_k2=diff_lam_k2[l], diff_subln_g=diff_subln_g[l],
            swa_sinks=swa_sinks[l], fox_forget_b=fox_forget_b[l],
            nsa_cmp_pos=nsa_cmp_pos[l], nsa_cmp_w1=nsa_cmp_w1[l], nsa_cmp_b1=nsa_cmp_b1[l],
            nsa_cmp_w2=nsa_cmp_w2[l], router_w=router_w, router_b=router_b,
            exp_w_gate=exp_w_gate, exp_w_up=exp_w_up, exp_w_down=exp_w_down,
        )
        x = _layer(l, x, mod[l], p, consts, final=(l == depth - 1))
    return x
```

```python
import functools
import math

import numpy as np
import jax
import jax.numpy as jnp
from jax import lax
from jax.experimental import pallas as pl
from jax.experimental.pallas import tpu as pltpu

F32 = jnp.float32
BF16 = jnp.bfloat16

LANES = 128
HEAD_DIM = 64
N_MIXERS = 4
HEADS = 4
GROUP_WIDTH = HEADS * HEAD_DIM
DIFF_DK = HEAD_DIM // 2
SWA_WINDOW = 128
SWA_KV_HEADS = 2
CMP_BLOCK = 32
CMP_STRIDE = 16
CMP_HIDDEN = 2 * HEAD_DIM
SEL_BLOCK = 64
SEL_TOPK = 16
NSA_WINDOW = 512
FORCE_SCORE = 1e4
NEG_INF = -1e30
MASK_BIG = 2.0 ** 100
N_EXPERTS = 64
EXPERTS_PER_GROUP = 8
EPS = 1e-6
LOG2E = math.log2(math.e)
VMEM_LIMIT = 56 * 1024 * 1024
WIDE_TILES = 4

IN_SPLITS = (
    GROUP_WIDTH, GROUP_WIDTH, GROUP_WIDTH,
    GROUP_WIDTH, SWA_KV_HEADS * HEAD_DIM, SWA_KV_HEADS * HEAD_DIM,
    GROUP_WIDTH, GROUP_WIDTH, GROUP_WIDTH, HEADS,
    GROUP_WIDTH, HEAD_DIM, HEAD_DIM, HEAD_DIM, HEAD_DIM,
    HEAD_DIM, HEAD_DIM, 3 * HEADS,
)

_NT = (((1,), (1,)), ((), ()))


def _alibi_slopes():
    n = 3 * HEADS
    m = 2.0 ** (-8.0 * np.arange(1, n + 1) / n)
    return m.reshape(HEADS, 3).T


SLOPES = _alibi_slopes()


def _dot(a, b):
    return jnp.dot(a, b, preferred_element_type=F32)


def _dot_nt(a, b):
    return lax.dot_general(a, b, _NT, preferred_element_type=F32)


def _split_bf16(a):
    hi = a.astype(BF16)
    lo = (a - hi.astype(F32)).astype(BF16)
    return hi, lo


def _dot_f32(a, b, nt=False):
    d = _dot_nt if nt else _dot
    ah, al = _split_bf16(a)
    bh, bl = _split_bf16(b)
    return d(ah, bh) + (d(ah, bl) + d(al, bh))


def _cparams(sem):
    return pltpu.CompilerParams(dimension_semantics=sem, vmem_limit_bytes=VMEM_LIMIT)


def _mod_kernel(c_ref, w_ref, b_ref, o_ref):
    c = c_ref[...]
    cond = c * jax.nn.sigmoid(c)
    o_ref[0] = _dot_f32(cond, w_ref[0]) + b_ref[0]


def _modulation(c, ada_w, ada_b):
    depth, d, n = ada_w.shape
    b = c.shape[0]
    rows = 8
    cp = jnp.zeros((rows, d), F32).at[:b].set(c)
    tn = 1536
    out = pl.pallas_call(
        _mod_kernel,
        grid=(depth, n // tn),
        in_specs=[
            pl.BlockSpec((rows, d), lambda l, j: (0, 0)),
            pl.BlockSpec((1, d, tn), lambda l, j: (l, 0, j)),
            pl.BlockSpec((1, 1, tn), lambda l, j: (l, 0, j)),
        ],
        out_specs=pl.BlockSpec((1, rows, tn), lambda l, j: (l, 0, j)),
        out_shape=jax.ShapeDtypeStruct((depth, rows, n), F32),
        compiler_params=_cparams(("arbitrary", "arbitrary")),
        name="adaln_mod",
    )(cp, ada_w, ada_b.reshape(depth, 1, n))
    return out[:, :b].reshape(depth, b, 6, d)


W_A, W_B, W_C, W_D, W_G, W_CMP = 768, 768, 768, 768, 128, 128
W_OFFS = np.cumsum((0, W_A, W_B, W_C, W_D, W_G, W_CMP))
W_TOTAL = int(W_OFFS[-1])


def _prep_w_in(w):
    offs = np.cumsum((0,) + IN_SPLITS)
    col = lambda i: w[:, int(offs[i]):int(offs[i + 1])]
    dup = lambda t: jnp.concatenate([t, t], axis=1)
    kb, vb = col(4), col(5)
    gates = jnp.zeros((w.shape[0], W_G), F32).at[:, :3 * HEADS].set(col(17))
    qs64 = LOG2E * HEAD_DIM ** -0.5
    parts = [
        col(0) * (LOG2E * DIFF_DK ** -0.5), col(1), col(2),
        col(3) * qs64,
        dup(kb[:, :HEAD_DIM]), dup(kb[:, HEAD_DIM:]),
        dup(vb[:, :HEAD_DIM]), dup(vb[:, HEAD_DIM:]),
        col(6) * qs64, col(7), col(8),
        col(10) * qs64, dup(col(13)), dup(col(14)), dup(col(15)), dup(col(16)),
    ] + [gates, col(11), col(12)]
    wp = jnp.concatenate(parts, axis=1).astype(BF16)
    wf = jnp.zeros((8, w.shape[0]), F32).at[:HEADS].set(col(9).T).astype(BF16)
    return wp, wf


def _lane_cumsum(y):
    n = y.shape[1]
    lane = lax.broadcasted_iota(jnp.int32, y.shape, 1)
    sh = 1
    while sh < n:
        y = y + jnp.where(lane >= sh, pltpu.roll(y, sh, 1), 0.0)
        sh *= 2
    return y


def _inproj_kernel(x_ref, mod_ref, g_ref, w_ref, wf_ref, bf_ref,
                   oa, ob, oc, od, og, ocmp, ofc, carry):
    i = pl.program_id(1)
    x = x_ref[0]
    ms = jnp.mean(x * x, axis=-1, keepdims=True)
    y = x * lax.rsqrt(ms + EPS) * g_ref[...]
    h = y * (1.0 + mod_ref[1:2, :]) + mod_ref[0:1, :]
    hb = h.astype(BF16)
    o = W_OFFS
    oa[0] = _dot(hb, w_ref[:, o[0]:o[1]]).astype(BF16)
    ob[0] = _dot(hb, w_ref[:, o[1]:o[2]]).astype(BF16)
    oc[0] = _dot(hb, w_ref[:, o[2]:o[3]]).astype(BF16)
    od[0] = _dot(hb, w_ref[:, o[3]:o[4]]).astype(BF16)
    og[0] = jax.nn.sigmoid(_dot(hb, w_ref[:, o[4]:o[5]])).astype(BF16)
    kvc = _dot(hb, w_ref[:, o[5]:o[6]])
    ocmp[0, 0] = kvc[:, :HEAD_DIM]
    ocmp[1, 0] = kvc[:, HEAD_DIM:]
    fl = _dot_nt(wf_ref[...], hb) + bf_ref[:, 0:1]
    ls = jnp.minimum(fl, 0.0) - jnp.log1p(jnp.exp(-jnp.abs(fl)))

    @pl.when(i == 0)
    def _():
        carry[...] = jnp.zeros_like(carry)

    cs = _lane_cumsum(ls) + carry[:, 0:1]
    ofc[0] = cs
    carry[...] = jnp.broadcast_to(cs[:, cs.shape[1] - 1:], carry.shape)


def _in_projection(x, mod_l, g, wp, wf, bf, tm):
    b, s, d = x.shape
    bf8 = jnp.zeros((8, LANES), F32).at[:HEADS, :].set(bf[:, None])
    outs = pl.pallas_call(
        _inproj_kernel,
        grid=(b, s // tm),
        in_specs=[
            pl.BlockSpec((1, tm, d), lambda bi, i: (bi, i, 0)),
            pl.BlockSpec((None, 6, d), lambda bi, i: (bi, 0, 0)),
            pl.BlockSpec((1, d), lambda bi, i: (0, 0)),
            pl.BlockSpec((d, W_TOTAL), lambda bi, i: (0, 0)),
            pl.BlockSpec((8, d), lambda bi, i: (0, 0)),
            pl.BlockSpec((8, LANES), lambda bi, i: (0, 0)),
        ],
        out_specs=[
            pl.BlockSpec((1, tm, W_A), lambda bi, i: (bi, i, 0)),
            pl.BlockSpec((1, tm, W_B), lambda bi, i: (bi, i, 0)),
            pl.BlockSpec((1, tm, W_C), lambda bi, i: (bi, i, 0)),
            pl.BlockSpec((1, tm, W_D), lambda bi, i: (bi, i, 0)),
            pl.BlockSpec((1, tm, W_G), lambda bi, i: (bi, i, 0)),
            pl.BlockSpec((2, 1, tm, HEAD_DIM), lambda bi, i: (0, bi, i, 0)),
            pl.BlockSpec((1, 8, tm), lambda bi, i: (bi, 0, i)),
        ],
        out_shape=[
            jax.ShapeDtypeStruct((b, s, W_A), BF16),
            jax.ShapeDtypeStruct((b, s, W_B), BF16),
            jax.ShapeDtypeStruct((b, s, W_C), BF16),
            jax.ShapeDtypeStruct((b, s, W_D), BF16),
            jax.ShapeDtypeStruct((b, s, W_G), BF16),
            jax.ShapeDtypeStruct((2, b, s, HEAD_DIM), F32),
            jax.ShapeDtypeStruct((b, 8, s), F32),
        ],
        scratch_shapes=[pltpu.VMEM((8, LANES), F32)],
        compiler_params=_cparams(("arbitrary", "arbitrary")),
        name="in_projection",
    )(x, mod_l, g.reshape(1, d), wp, wf, bf8)
    return outs


def _flash_kernel(kind, t, layer_idx, *refs):
    it = iter(refs)
    q_ref, k_ref, v_ref = next(it), next(it), next(it)
    fc_ref = next(it) if kind == "C" else None
    un_ref, use_ref, en_ref = (next(it), next(it), next(it)) if kind == "Dsel" else (None, None, None)
    lam_ref, sg_ref = (next(it), next(it)) if kind == "A" else (None, None)
    o_ref, m_scr, acc_scr = next(it), next(it), next(it)

    p = pl.program_id(1)
    i = pl.program_id(2)
    nstream = 4 if kind == "A" else 2
    width = LANES // nstream
    mixer = {"A": 0, "Dsel": 2}.get(kind)

    q2 = q_ref[0]
    lane = lax.broadcasted_iota(jnp.int32, (t, LANES), 1)
    zero = jnp.zeros_like(q2)
    qms = [jnp.where((lane >= s * width) & (lane < (s + 1) * width), q2, zero)
           for s in range(nstream)]
    if kind == "Dsel":
        qms = [jnp.concatenate([qm, un_ref[0]], axis=1) for qm in qms]
    qcat = jnp.concatenate(qms, axis=0)

    def head_of(s):
        return s // 2 if kind == "A" else s

    slopes = None
    if mixer is not None:
        sl = SLOPES[mixer] * LOG2E
        slopes = [jnp.where(p == 0, float(sl[head_of(s)]), float(sl[2 + head_of(s)]))
                  for s in range(nstream)]

    for s in range(nstream):
        m_scr[s] = jnp.full((t, LANES), NEG_INF, F32)
        acc_scr[s] = jnp.zeros((t, LANES), F32)

    def tile(first_key, nk, diag):
        start = pl.multiple_of(first_key, t)
        k2 = k_ref[0, pl.ds(start, nk), :]
        if kind == "Dsel":
            k2 = jnp.concatenate([k2, en_ref[pl.ds(start, nk), :]], axis=1)
        v2 = v_ref[0, pl.ds(start, nk), :]
        vlane = lax.broadcasted_iota(jnp.int32, (nk, LANES), 1)
        one = jnp.ones_like(v2)
        vhalf = [jnp.where(vlane < HEAD_DIM, v2, one), jnp.where(vlane < HEAD_DIM, one, v2)]
        col = lax.broadcasted_iota(jnp.int32, (1, nk), 1)
        rel = (start - i * t + col).astype(F32)
        if diag:
            mask = (lax.broadcasted_iota(jnp.int32, (t, t), 0)
                    >= lax.broadcasted_iota(jnp.int32, (t, t), 1))
        sc_all = _dot_nt(qcat, k2)
        for s in range(nstream):
            sc = sc_all[s * t:(s + 1) * t]
            if slopes is not None:
                sc = sc + slopes[s] * rel
            if kind == "C":
                sc = sc - LOG2E * fc_ref[0, pl.ds(2 * p + s, 1), pl.ds(start, nk)]
            if diag:
                sc = jnp.where(mask, sc, NEG_INF)
            m_prev = m_scr[s]
            m_next = jnp.maximum(m_prev, jnp.max(sc, axis=1, keepdims=True))
            alpha = jnp.exp2(m_prev - m_next)
            pexp = jnp.exp2((sc - jnp.tile(m_next, (1, nk // LANES))).astype(BF16))
            acc_scr[s] = alpha * acc_scr[s] + _dot(pexp, vhalf[head_of(s)])
            m_scr[s] = m_next

    tile(i * t, t, True)

    if kind == "Dsel":
        nq = pl.num_programs(2)
        base = (pl.program_id(0) * nq + i) * nq

        def body(j, carry):
            @pl.when(use_ref[base + j] != 0)
            def _():
                tile(j * t, t, False)
            return carry

        lax.fori_loop(0, i, body, 0)
    else:
        wide = WIDE_TILES * t

        def body(j, carry):
            tile(j * wide, wide, False)
            return carry

        lax.fori_loop(0, i // WIDE_TILES, body, 0)

        def rest(j, carry):
            tile(j * t, t, False)
            return carry

        lax.fori_loop(i // WIDE_TILES * WIDE_TILES, i, rest, 0)

    outs = [acc_scr[s] / pltpu.roll(acc_scr[s], HEAD_DIM, 1) for s in range(nstream)]
    if kind == "A":
        lam_init = 0.8 - 0.6 * math.exp(-0.3 * layer_idx)
        t1 = jnp.sum(lam_ref[0:1, :] * lam_ref[1:2, :], axis=1, keepdims=True)
        t2 = jnp.sum(lam_ref[2:3, :] * lam_ref[3:4, :], axis=1, keepdims=True)
        lam = jnp.exp(t1) - jnp.exp(t2) + lam_init
        d0 = outs[0] - lam * outs[1]
        d1 = outs[2] - lam * outs[3]
        lo_half = lane < HEAD_DIM
        o = jnp.where(lo_half, d0, d1)
        sq = o * o
        ss_lo = jnp.sum(jnp.where(lo_half, sq, 0.0), axis=1, keepdims=True)
        ss_hi = jnp.sum(jnp.where(lo_half, 0.0, sq), axis=1, keepdims=True)
        ms = jnp.where(lo_half, ss_lo, ss_hi) * (1.0 / HEAD_DIM)
        o = o * lax.rsqrt(ms + EPS) * sg_ref[...] * (1.0 - lam_init)
    else:
        o = jnp.where(lane < HEAD_DIM, outs[0], outs[1])
    o_ref[0] = o.astype(o_ref.dtype)


def _flash(kind, t, layer_idx, q_arr, qoff, k_arr, koff, v_arr, voff, extras):
    b, s, _ = q_arr.shape
    nstream = 4 if kind == "A" else 2
    shared_kv = kind == "Dsel"
    kv_idx = (lambda off: (lambda bi, p, i: (bi, 0, off))) if shared_kv else \
             (lambda off: (lambda bi, p, i: (bi, 0, off + p)))
    in_specs = [
        pl.BlockSpec((1, t, LANES), lambda bi, p, i: (bi, i, qoff + p)),
        pl.BlockSpec((1, s, LANES), kv_idx(koff)),
        pl.BlockSpec((1, s, LANES), kv_idx(voff)),
    ]
    args = [q_arr, k_arr, v_arr]
    if kind == "C":
        fcum, = extras
        in_specs.append(pl.BlockSpec((1, 8, s), lambda bi, p, i: (bi, 0, 0)))
        args.append(fcum)
    elif kind == "Dsel":
        unsel, eneg = extras
        nsel = unsel.shape[-1]
        nq, per = s // t, t // SEL_BLOCK
        used = (unsel.reshape(b, nq, t, nq, per) == 0).any(axis=(2, 4)).astype(jnp.int32).reshape(-1)
        in_specs.append(pl.BlockSpec((1, t, nsel), lambda bi, p, i: (bi, i, 0)))
        in_specs.append(pl.BlockSpec(memory_space=pltpu.SMEM))
        in_specs.append(pl.BlockSpec((s, nsel), lambda bi, p, i: (0, 0)))
        args += [unsel, used, eneg]
    elif kind == "A":
        lam8, sg = extras
        in_specs.append(pl.BlockSpec((8, LANES), lambda bi, p, i: (0, 0)))
        in_specs.append(pl.BlockSpec((1, LANES), lambda bi, p, i: (0, 0)))
        args += [lam8, sg]
    return pl.pallas_call(
        functools.partial(_flash_kernel, kind, t, layer_idx),
        grid=(b, 2, s // t),
        in_specs=in_specs,
        out_specs=pl.BlockSpec((1, t, LANES), lambda bi, p, i: (bi, i, p)),
        out_shape=jax.ShapeDtypeStruct((b, s, 2 * LANES), BF16),
        scratch_shapes=[pltpu.VMEM((nstream, t, LANES), F32)] * 2,
        compiler_params=_cparams(("arbitrary", "arbitrary", "arbitrary")),
        name="flash_" + kind,
    )(*args)


def _window_kernel(kind, t, wpad, *refs):
    it = iter(refs)
    q_ref, k_ref, v_ref = next(it), next(it), next(it)
    sink_ref = next(it) if kind == "B" else None
    o_ref = next(it)
    i = pl.program_id(1)
    window = SWA_WINDOW if kind == "B" else NSA_WINDOW
    sl = SLOPES[1 if kind == "B" else 2] * LOG2E
    nk = wpad + t
    start = pl.multiple_of(jnp.maximum(i * t - wpad, 0), LANES)
    lane = lax.broadcasted_iota(jnp.int32, (t, LANES), 1)
    vlane = lax.broadcasted_iota(jnp.int32, (nk, LANES), 1)
    dist = ((i * t - start) + lax.broadcasted_iota(jnp.int32, (t, nk), 0)
            - lax.broadcasted_iota(jnp.int32, (t, nk), 1))
    valid = (dist >= 0) & (dist < window)
    rel = (start - i * t + lax.broadcasted_iota(jnp.int32, (1, nk), 1)).astype(F32)
    qrow = lax.broadcasted_iota(jnp.int32, (t, 1), 0).astype(F32)

    def masked_q(pair, g):
        q2 = q_ref[0, :, pair * LANES:(pair + 1) * LANES]
        return jnp.where((lane >= g * HEAD_DIM) & (lane < (g + 1) * HEAD_DIM), q2, jnp.zeros_like(q2))

    def kv_tiles(pair):
        cols = slice(pair * LANES, (pair + 1) * LANES) if kind == "B" else slice(0, LANES)
        k2 = k_ref[0, pl.ds(start, nk), cols]
        v2 = v_ref[0, pl.ds(start, nk), cols]
        one = jnp.ones_like(v2)
        return k2, [jnp.where(vlane < HEAD_DIM, v2, one), jnp.where(vlane < HEAD_DIM, one, v2)]

    groups = [[(0, 0), (0, 1)], [(1, 0), (1, 1)]] if kind == "B" else [[(0, 0), (0, 1), (1, 0), (1, 1)]]
    outs = {}
    for group in groups:
        k2, vhalf = kv_tiles(group[0][0])
        sc_all = _dot_nt(jnp.concatenate([masked_q(pair, g) for pair, g in group], axis=0), k2)
        for n, (pair, g) in enumerate(group):
            h = 2 * pair + g
            slope = float(sl[h])
            sc = jnp.where(valid, sc_all[n * t:(n + 1) * t] + slope * rel, NEG_INF)
            m = jnp.max(sc, axis=1, keepdims=True)
            if kind == "B":
                sink = LOG2E * sink_ref[h] + slope * qrow
                m = jnp.maximum(m, sink)
            pv = _dot(jnp.exp2((sc - m).astype(BF16)), vhalf[g])
            l = pltpu.roll(pv, HEAD_DIM, 1)
            if kind == "B":
                l = l + jnp.exp2(sink - m)
            outs[h] = pv / l
    for pair in range(2):
        o_ref[0, :, pair * LANES:(pair + 1) * LANES] = jnp.where(
            lane < HEAD_DIM, outs[2 * pair], outs[2 * pair + 1]).astype(o_ref.dtype)


def _window(kind, t, wpad, q_arr, k_arr, koff, v_arr, voff, extras):
    b, s, _ = q_arr.shape
    kw = 2 * LANES if kind == "B" else LANES
    in_specs = [
        pl.BlockSpec((1, t, 2 * LANES), lambda bi, i: (bi, i, 0)),
        pl.BlockSpec((1, s, kw), lambda bi, i: (bi, 0, koff)),
        pl.BlockSpec((1, s, kw), lambda bi, i: (bi, 0, voff)),
    ]
    args = [q_arr, k_arr, v_arr]
    if kind == "B":
        in_specs.append(pl.BlockSpec(memory_space=pltpu.SMEM))
        args.append(extras[0])
    return pl.pallas_call(
        functools.partial(_window_kernel, kind, t, wpad),
        grid=(b, s // t),
        in_specs=in_specs,
        out_specs=pl.BlockSpec((1, t, 2 * LANES), lambda bi, i: (bi, i, 0)),
        out_shape=jax.ShapeDtypeStruct((b, s, 2 * LANES), BF16),
        compiler_params=_cparams(("arbitrary", "arbitrary")),
        name="window_" + kind,
    )(*args)


def _gelu_tanh(x):
    return 0.5 * x * (1.0 + jnp.tanh(math.sqrt(2.0 / math.pi) * (x + 0.044715 * x * x * x)))


def _compress_kernel(t_ref, w1_ref, pos_ref, b1_ref, w2_ref, o_ref):
    tr = t_ref[0, 0]
    half = tr.shape[1]
    n = tr.shape[0]
    w1 = w1_ref[0]
    u = _dot_f32(tr, w1[:half])
    v = _dot_f32(tr, w1[half:])
    cpos = _dot_f32(pos_ref[0], w1)[0:1] + b1_ref[0]
    hid = u + pltpu.roll(v, n - 1, 0) + cpos
    o_ref[0, 0] = _dot_f32(_gelu_tanh(hid), w2_ref[0])


def _compress(kv_r, w1, pos8, b1, w2d):
    two, b, n, dd = kv_r.shape
    return pl.pallas_call(
        _compress_kernel,
        grid=(two, b),
        in_specs=[
            pl.BlockSpec((1, 1, n, dd), lambda c, bi: (c, bi, 0, 0)),
            pl.BlockSpec((1, 2 * dd, CMP_HIDDEN), lambda c, bi: (c, 0, 0)),
            pl.BlockSpec((1, 8, 2 * dd), lambda c, bi: (c, 0, 0)),
            pl.BlockSpec((1, 1, CMP_HIDDEN), lambda c, bi: (c, 0, 0)),
            pl.BlockSpec((1, CMP_HIDDEN, LANES), lambda c, bi: (c, 0, 0)),
        ],
        out_specs=pl.BlockSpec((1, 1, n, LANES), lambda c, bi: (c, bi, 0, 0)),
        out_shape=jax.ShapeDtypeStruct((two, b, n, LANES), F32),
        compiler_params=_cparams(("arbitrary", "arbitrary")),
        name="nsa_compress",
    )(kv_r, w1, pos8, b1, w2d)


def _cmp_kernel(t, first_tile, ncp_total, q_ref, kc_ref, vc_ref, cov_ref, o_ref, un_ref):
    i = first_tile + pl.program_id(1)
    ncp = kc_ref.shape[2]
    nsel = cov_ref.shape[0]
    kc = kc_ref[0, 0]
    vc = vc_ref[0, 0]
    kch, kcl = _split_bf16(kc)
    vcb = vc.astype(BF16)
    lane = lax.broadcasted_iota(jnp.int32, (t, LANES), 1)
    tq = i * t + lax.broadcasted_iota(jnp.int32, (t, ncp), 0)
    nidx = lax.broadcasted_iota(jnp.int32, (t, ncp), 1)
    cmp_end = nidx * CMP_STRIDE + (CMP_BLOCK - 1)
    valid = (tq >= cmp_end) & (nidx < ncp_total - 1)
    end_row = cmp_end[0:1, :].astype(F32)
    psum = jnp.zeros((t, ncp), F32)
    outs = []
    for pair in range(2):
        q2 = q_ref[0, :, pair * LANES:(pair + 1) * LANES]
        for g in range(2):
            h = 2 * pair + g
            qm = jnp.where((lane >= g * HEAD_DIM) & (lane < (g + 1) * HEAD_DIM), q2, jnp.zeros_like(q2))
            sc = _dot_nt(qm, kch) + _dot_nt(qm, kcl) + float(SLOPES[2][h] * LOG2E) * end_row
            sc = jnp.where(valid, sc, NEG_INF)
            m = jnp.max(sc, axis=1, keepdims=True)
            pe = jnp.exp2(sc - m)
            l = jnp.sum(pe, axis=1, keepdims=True)
            pc = pe * jnp.where(m > 0.5 * NEG_INF, 1.0 / l, 0.0)
            psum = psum + pc
            outs.append(_dot(pc.astype(BF16), vcb))
    lo_half = lane < HEAD_DIM
    o_ref[0, :, 0:LANES] = jnp.where(lo_half, outs[0], outs[1]).astype(o_ref.dtype)
    o_ref[0, :, LANES:2 * LANES] = jnp.where(lo_half, outs[2], outs[3]).astype(o_ref.dtype)
    ph, plo = _split_bf16(psum)
    cov = cov_ref[...]
    imp = _dot_nt(cov, ph) + _dot_nt(cov, plo)
    blk = lax.broadcasted_iota(jnp.int32, (nsel, t), 0)
    cur = (i * t + lax.broadcasted_iota(jnp.int32, (nsel, t), 1)) // SEL_BLOCK
    forced = (blk == 0) | (blk == cur) | (blk == cur - 1)
    score = jnp.where(forced, FORCE_SCORE, jnp.where(blk <= cur, imp, -1.0))
    unsel = jnp.ones((nsel, t), F32)
    for _ in range(min(SEL_TOPK, nsel)):
        mx = jnp.max(score, axis=0, keepdims=True)
        idx = jnp.min(jnp.where(score == mx, blk, nsel), axis=0, keepdims=True)
        hit = blk == idx
        unsel = jnp.where(hit, 0.0, unsel)
        score = jnp.where(hit, -3.0, score)
    un_ref[0] = unsel.T.astype(un_ref.dtype)


CMP_RANGES = 4


def _cmp_attention(t, qd, kvc, cover_t):
    b, s, _ = qd.shape
    ncp_total = kvc.shape[2]
    nsel = cover_t.shape[0]
    steps = s // t // CMP_RANGES
    outs = []
    for r in range(CMP_RANGES):
        ncp = (r + 1) * ncp_total // CMP_RANGES
        first = r * steps
        outs.append(pl.pallas_call(
            functools.partial(_cmp_kernel, t, first, ncp_total),
            grid=(b, steps),
            in_specs=[
                pl.BlockSpec((1, t, 2 * LANES), lambda bi, i, first=first: (bi, first + i, 0)),
                pl.BlockSpec((1, 1, ncp, LANES), lambda bi, i: (0, bi, 0, 0)),
                pl.BlockSpec((1, 1, ncp, LANES), lambda bi, i: (1, bi, 0, 0)),
                pl.BlockSpec((nsel, ncp), lambda bi, i: (0, 0)),
            ],
            out_specs=[
                pl.BlockSpec((1, t, 2 * LANES), lambda bi, i: (bi, i, 0)),
                pl.BlockSpec((1, t, nsel), lambda bi, i: (bi, i, 0)),
            ],
            out_shape=[
                jax.ShapeDtypeStruct((b, steps * t, 2 * LANES), BF16),
                jax.ShapeDtypeStruct((b, steps * t, nsel), BF16),
            ],
            compiler_params=_cparams(("arbitrary", "arbitrary")),
            name="nsa_cmp_select",
        )(qd, kvc, kvc, cover_t))
    return (jnp.concatenate([o for o, _ in outs], axis=1),
            jnp.concatenate([u for _, u in outs], axis=1))


def _cover_matrix(s):
    ncp, nsel = s // CMP_STRIDE, s // SEL_BLOCK
    cs = np.arange(ncp)[:, None] * CMP_STRIDE
    ss = np.arange(nsel)[None, :] * SEL_BLOCK
    cov = np.clip(np.minimum(cs + CMP_BLOCK, ss + SEL_BLOCK) - np.maximum(cs, ss), 0, None) / CMP_BLOCK
    cov[ncp - 1] = 0.0
    return jnp.asarray(cov.T, BF16)


def _expand_neg(s):
    nsel = s // SEL_BLOCK
    e = (np.arange(s)[:, None] // SEL_BLOCK) == np.arange(nsel)[None, :]
    return jnp.asarray(np.where(e, -MASK_BIG, 0.0), BF16)


def _outproj_kernel(ya, yb, yc, ocmp, osel, owin, gt, gx_ref, x_ref, mod_ref, wo_ref, g_ref,
                    rwt_ref, rb_ref, xo_ref, h_ref, ri_ref, rg_ref, cnt_ref, carry):
    first = (pl.program_id(0) == 0) & (pl.program_id(1) == 0)
    gw = GROUP_WIDTH
    g = _dot(gt[0], gx_ref[...])
    yd = (g[:, 0:gw] * ocmp[0].astype(F32) + g[:, gw:2 * gw] * osel[0].astype(F32)
          + g[:, 2 * gw:3 * gw] * owin[0].astype(F32))
    ycat = jnp.concatenate([ya[0], yb[0], yc[0], yd.astype(BF16)], axis=1)
    y = _dot(ycat, wo_ref[...])
    x = x_ref[0] + mod_ref[2:3, :] * y
    xo_ref[0] = x
    ms = jnp.mean(x * x, axis=-1, keepdims=True)
    h = x * lax.rsqrt(ms + EPS) * g_ref[...]
    h = h * (1.0 + mod_ref[4:5, :]) + mod_ref[3:4, :]
    h_ref[0] = h
    tm = h.shape[0]
    ng = LANES // EXPERTS_PER_GROUP
    big = 4 * LANES
    logits = _dot_f32(rwt_ref[...], h, nt=True)
    eidx = lax.broadcasted_iota(jnp.int32, (LANES, tm), 0)
    aff = jax.nn.sigmoid(logits)
    sel = jnp.where(eidx < N_EXPERTS, aff + rb_ref[:, 0:1], NEG_INF)
    sel3 = sel.reshape(ng, EXPERTS_PER_GROUP, tm)
    e3 = eidx.reshape(ng, EXPERTS_PER_GROUP, tm)
    g1 = jnp.max(sel3, axis=1, keepdims=True)
    i1 = jnp.min(jnp.where(sel3 == g1, e3, big), axis=1, keepdims=True)
    sel_b = jnp.where(e3 == i1, NEG_INF, sel3)
    g2 = jnp.max(sel_b, axis=1, keepdims=True)
    i2 = jnp.min(jnp.where(sel_b == g2, e3, big), axis=1, keepdims=True)
    gs = g1 + g2
    gidx = lax.broadcasted_iota(jnp.int32, (ng, 1, tm), 0)
    gm = jnp.max(gs, axis=0, keepdims=True)
    best = gidx == jnp.min(jnp.where(gs == gm, gidx, big), axis=0, keepdims=True)
    e1 = jnp.min(jnp.where(best, i1, big), axis=0)
    e2 = jnp.min(jnp.where(best, i2, big), axis=0)
    oh1 = eidx == e1
    oh2 = eidx == e2
    a1 = jnp.sum(jnp.where(oh1, aff, 0.0), axis=0, keepdims=True)
    a2 = jnp.sum(jnp.where(oh2, aff, 0.0), axis=0, keepdims=True)
    inv = 1.0 / (a1 + a2)
    @pl.when(first)
    def _():
        carry[...] = jnp.zeros_like(carry)

    ohs = jnp.where(oh1 | oh2, 1.0, 0.0)
    rr = lax.broadcasted_iota(jnp.int32, (tm, tm), 0)
    cc = lax.broadcasted_iota(jnp.int32, (tm, tm), 1)
    earlier = jnp.where(rr < cc, 1.0, 0.0).astype(BF16)
    before = _dot(ohs.astype(BF16), earlier) + carry[:, 0:1]
    r1 = jnp.sum(jnp.where(oh1, before, 0.0), axis=0, keepdims=True)
    r2 = jnp.sum(jnp.where(oh2, before, 0.0), axis=0, keepdims=True)
    total = carry[:, 0:1] + jnp.sum(ohs, axis=1, keepdims=True)
    carry[...] = jnp.broadcast_to(total, carry.shape)
    cnt_ref[...] = jnp.broadcast_to(total, cnt_ref.shape).astype(jnp.int32)
    row = lax.broadcasted_iota(jnp.int32, (8, tm), 0)
    ri_ref[0] = jnp.where(row == 0, e1, jnp.where(row == 1, e2, jnp.where(
        row == 2, r1.astype(jnp.int32), jnp.where(row == 3, r2.astype(jnp.int32), 0))))
    rg_ref[0] = jnp.where(row == 0, a1 * inv, jnp.where(row == 1, a2 * inv, 0.0))


def _gate_expander():
    e = np.zeros((W_G, 3 * GROUP_WIDTH), np.float32)
    for h in range(HEADS):
        for r in range(3):
            e[h * 3 + r, r * GROUP_WIDTH + h * HEAD_DIM:r * GROUP_WIDTH + (h + 1) * HEAD_DIM] = 1.0
    return jnp.asarray(e, BF16)


def _out_projection(ya, yb, yc, ocmp, osel, owin, gates, x, mod_l, wo, g, rwt, rb, tm):
    b, s, d = x.shape
    gw = GROUP_WIDTH
    tok = lambda w: pl.BlockSpec((1, tm, w), lambda bi, i: (bi, i, 0))
    full = lambda shp: pl.BlockSpec(shp, lambda bi, i: (0,) * len(shp))
    rowblk = pl.BlockSpec((1, 8, tm), lambda bi, i: (bi, 0, i))
    return pl.pallas_call(
        _outproj_kernel,
        grid=(b, s // tm),
        in_specs=[tok(gw)] * 6 + [tok(W_G), full((W_G, 3 * gw)), tok(d),
                                  pl.BlockSpec((None, 6, d), lambda bi, i: (bi, 0, 0)),
                                  full((d, d)), full((1, d)), full((LANES, d)), full((LANES, 1))],
        out_specs=[tok(d), tok(d), rowblk, rowblk, full((LANES, LANES))],
        out_shape=[
            jax.ShapeDtypeStruct((b, s, d), F32),
            jax.ShapeDtypeStruct((b, s, d), F32),
            jax.ShapeDtypeStruct((b, 8, s), jnp.int32),
            jax.ShapeDtypeStruct((b, 8, s), F32),
            jax.ShapeDtypeStruct((LANES, LANES), jnp.int32),
        ],
        scratch_shapes=[pltpu.VMEM((LANES, LANES), F32)],
        compiler_params=_cparams(("arbitrary", "arbitrary")),
        name="out_projection_router",
    )(ya, yb, yc, ocmp, osel, owin, gates, _gate_expander(), x, mod_l, wo, g.reshape(1, d), rwt, rb)


def _dispatch_kernel(p0_ref, p1_ref, h_ref, xs_ref, sem):
    tm = h_ref.shape[0]

    def issue(r, c):
        for pos_ref in (p0_ref, p1_ref):
            pltpu.make_async_copy(h_ref.at[pl.ds(r, 1), :],
                                  xs_ref.at[pl.ds(pos_ref[r], 1), :], sem).start()
        return c

    lax.fori_loop(0, tm, issue, 0, unroll=8)
    for _ in range(2):
        pltpu.make_async_copy(h_ref, xs_ref.at[pl.ds(0, tm), :], sem).wait()


def _dispatch(h, pos0, pos1, tm):
    t, d = h.shape
    return pl.pallas_call(
        _dispatch_kernel,
        grid=(t // tm,),
        in_specs=[
            pl.BlockSpec((tm,), lambda i: (i,), memory_space=pltpu.SMEM),
            pl.BlockSpec((tm,), lambda i: (i,), memory_space=pltpu.SMEM),
            pl.BlockSpec((tm, d), lambda i: (i, 0)),
        ],
        out_specs=pl.BlockSpec(memory_space=pl.ANY),
        out_shape=jax.ShapeDtypeStruct((2 * t, d), F32),
        scratch_shapes=[pltpu.SemaphoreType.DMA(())],
        compiler_params=_cparams(("arbitrary",)),
        name="moe_dispatch",
    )(pos0, pos1, h)


def _expert_kernel(layer, vb_ref, ve_ref, vlo_ref, vhi_ref, vord_ref, vnext_ref, nv_ref,
                   xs_ref, wg_hbm, wu_hbm, wd_hbm, ys_ref, wgf, wuf, wdf, wgb, wub, wdb, sem):
    v = pl.program_id(0)
    rows = xs_ref.shape[0]
    prev = jnp.maximum(v - 1, 0)
    new_expert = (v == 0) | (ve_ref[v] != ve_ref[prev])
    new_block = (v == 0) | (vb_ref[v] != vb_ref[prev])

    def weight_copies(e, slot):
        return [pltpu.make_async_copy(src.at[layer, e], dst.at[slot], sem.at[slot, n])
                for n, (src, dst) in enumerate(((wg_hbm, wgf), (wu_hbm, wuf), (wd_hbm, wdf)))]

    @pl.when(v < nv_ref[0])
    def _():
        @pl.when(new_expert)
        def _():
            slot = vord_ref[v] % 2

            @pl.when(v == 0)
            def _():
                for c in weight_copies(ve_ref[v], slot):
                    c.start()

            @pl.when(vnext_ref[v] >= 0)
            def _():
                for c in weight_copies(vnext_ref[v], 1 - slot):
                    c.start()

            for c in weight_copies(ve_ref[v], slot):
                c.wait()
            wgb[...] = wgf[slot].astype(BF16)
            wub[...] = wuf[slot].astype(BF16)
            wdb[...] = wdf[slot].astype(BF16)

        x = xs_ref[...].astype(BF16)
        hg = _dot(x, wgb[...])
        hu = _dot(x, wub[...])
        hm = (hg * jax.nn.sigmoid(hg) * hu).astype(BF16)
        y = _dot(hm, wdb[...])
        r = vb_ref[v] * rows + lax.broadcasted_iota(jnp.int32, (rows, 1), 0)
        mine = (r >= vlo_ref[v]) & (r < vhi_ref[v])

        @pl.when(new_block)
        def _():
            ys_ref[...] = jnp.where(mine, y, 0.0)

        @pl.when(jnp.logical_not(new_block))
        def _():
            ys_ref[...] = jnp.where(mine, y, ys_ref[...])


def _experts(xs, meta, layer, wg, wu, wd, rows, nvmax):
    n, d = xs.shape
    de = wg.shape[3]
    blk = lambda v, vb, ve, lo, hi, vo, vn, nv: (vb[v], 0)
    grid_spec = pltpu.PrefetchScalarGridSpec(
        num_scalar_prefetch=7,
        grid=(nvmax,),
        in_specs=[
            pl.BlockSpec((rows, d), blk),
            pl.BlockSpec(memory_space=pl.ANY),
            pl.BlockSpec(memory_space=pl.ANY),
            pl.BlockSpec(memory_space=pl.ANY),
        ],
        out_specs=pl.BlockSpec((rows, d), blk),
        scratch_shapes=[pltpu.VMEM((2, d, de), F32), pltpu.VMEM((2, d, de), F32), pltpu.VMEM((2, de, d), F32),
                        pltpu.VMEM((d, de), BF16), pltpu.VMEM((d, de), BF16), pltpu.VMEM((de, d), BF16),
                        pltpu.SemaphoreType.DMA((2, 3))],
    )
    return pl.pallas_call(
        functools.partial(_expert_kernel, layer),
        grid_spec=grid_spec,
        out_shape=jax.ShapeDtypeStruct((n, d), F32),
        compiler_params=_cparams(("arbitrary",)),
        name="moe_experts",
    )(*meta, xs, wg, wu, wd)


def _combine_kernel(final, p0_ref, p1_ref, x_ref, rg_ref, mod_ref, g_ref, ys_ref, o_ref, buf, sem):
    tm = x_ref.shape[0]

    def issue(r, c):
        for k, pos_ref in enumerate((p0_ref, p1_ref)):
            pltpu.make_async_copy(ys_ref.at[pl.ds(pos_ref[r], 1), :],
                                  buf.at[k, pl.ds(r, 1), :], sem).start()
        return c

    lax.fori_loop(0, tm, issue, 0, unroll=8)
    for k in range(2):
        pltpu.make_async_copy(ys_ref.at[pl.ds(0, tm), :], buf.at[k], sem).wait()
    rg = rg_ref[...]
    y = rg[:, 0:1] * buf[0] + rg[:, 1:2] * buf[1]
    x = x_ref[...] + mod_ref[5:6, :] * y
    if final:
        ms = jnp.mean(x * x, axis=-1, keepdims=True)
        x = x * lax.rsqrt(ms + EPS) * g_ref[...]
    o_ref[...] = x


def _combine(final, pos0, pos1, x, rg, mod_l, g, ys, tm):
    t, d = x.shape
    per_b = t // mod_l.shape[0] // tm
    return pl.pallas_call(
        functools.partial(_combine_kernel, final),
        grid=(t // tm,),
        in_specs=[
            pl.BlockSpec((tm,), lambda i: (i,), memory_space=pltpu.SMEM),
            pl.BlockSpec((tm,), lambda i: (i,), memory_space=pltpu.SMEM),
            pl.BlockSpec((tm, d), lambda i: (i, 0)),
            pl.BlockSpec((tm, 8), lambda i: (i, 0)),
            pl.BlockSpec((None, 6, d), lambda i: (i // per_b, 0, 0)),
            pl.BlockSpec((1, d), lambda i: (0, 0)),
            pl.BlockSpec(memory_space=pl.ANY),
        ],
        out_specs=pl.BlockSpec((tm, d), lambda i: (i, 0)),
        out_shape=jax.ShapeDtypeStruct((t, d), F32),
        scratch_shapes=[pltpu.VMEM((2, tm, d), F32), pltpu.SemaphoreType.DMA(())],
        compiler_params=_cparams(("arbitrary",)),
        name="moe_combine",
    )(pos0, pos1, x, rg, mod_l, g.reshape(1, d), ys)


def _visit_plan(counts, rows, nblocks):
    ne = counts.shape[0]
    nvmax = nblocks + ne - 1
    ends = jnp.cumsum(counts)
    offs = ends - counts
    b_lo = offs // rows
    b_hi = jnp.maximum(ends - 1, 0) // rows
    nvis = jnp.where(counts > 0, b_hi - b_lo + 1, 0)
    vend = jnp.cumsum(nvis)
    vstart = vend - nvis
    nv = vend[-1]
    v = jnp.minimum(jnp.arange(nvmax), nv - 1)
    e = jnp.sum((v[:, None] >= vend[None, :]).astype(jnp.int32), axis=1)
    onehot = e[:, None] == jnp.arange(ne)[None, :]
    pick = lambda a: jnp.sum(jnp.where(onehot, a[None, :], 0), axis=1)
    blk = pick(b_lo) + v - pick(vstart)
    lo = jnp.maximum(pick(offs), blk * rows)
    hi = jnp.minimum(pick(ends), (blk + 1) * rows)
    order = jnp.cumsum((counts > 0).astype(jnp.int32)) - 1
    ids = jnp.where(counts > 0, jnp.arange(ne), ne)
    later = jnp.concatenate([jnp.flip(lax.cummin(jnp.flip(ids)))[1:], jnp.full((1,), ne, ids.dtype)])
    nxt = jnp.where(later < ne, later, -1)
    i32 = lambda a: a.astype(jnp.int32)
    return (i32(blk), i32(e), i32(lo), i32(hi), i32(pick(order)), i32(pick(nxt)),
            i32(nv).reshape(1)), offs, nvmax


def _tiles(s):
    tm = min(512, s)
    return dict(tm=tm, t_full=min(512, s), t_win=min(256, s), t_cmp=min(512, s),
                tm_moe=min(512, s), rows=256)


def _layer(l, x, mod_l, p, consts, final):
    b, s, d = x.shape
    tl = _tiles(s)
    wp, wf = _prep_w_in(p["w_in"])
    qa, qb, qc, qd, gates, kvcmp, fcum = _in_projection(
        x, mod_l, p["norm_attn_g"], wp, wf, p["fox_forget_b"], tl["tm"])
    lam8 = jnp.zeros((8, LANES), F32).at[:4, :DIFF_DK].set(
        jnp.stack([p["diff_lam_q1"], p["diff_lam_k1"], p["diff_lam_q2"], p["diff_lam_k2"]]))
    sg = jnp.tile(p["diff_subln_g"], 2).reshape(1, LANES)
    ya = _flash("A", tl["t_full"], l, qa, 0, qa, 2, qa, 4, (lam8, sg))
    yb = _window("B", tl["t_win"], SWA_WINDOW, qb, qb, 1, qb, 2, (p["swa_sinks"],))
    yc = _flash("C", tl["t_full"], l, qc, 0, qc, 2, qc, 4, (fcum,))
    n16 = s // CMP_STRIDE
    kv_r = kvcmp.reshape(2, b, n16, CMP_STRIDE * HEAD_DIM)
    pos8 = jnp.zeros((2, 8, CMP_BLOCK * HEAD_DIM), F32).at[:, 0].set(
        p["nsa_cmp_pos"].reshape(2, CMP_BLOCK * HEAD_DIM))
    w2d = jnp.concatenate([p["nsa_cmp_w2"], p["nsa_cmp_w2"]], axis=-1)
    kvc = _compress(kv_r, p["nsa_cmp_w1"], pos8, p["nsa_cmp_b1"].reshape(2, 1, CMP_HIDDEN), w2d)
    ocmp, unsel = _cmp_attention(tl["t_cmp"], qd, kvc, consts["cover"])
    osel = _flash("Dsel", tl["t_full"], l, qd, 0, qd, 2, qd, 3, (unsel, consts["eneg"]))
    owin = _window("Dwin", tl["t_win"], NSA_WINDOW, qd, qd, 4, qd, 5, ())
    rwt = jnp.zeros((LANES, d), F32).at[:N_EXPERTS].set(p["router_w"].T)
    rb = jnp.zeros((LANES, 1), F32).at[:N_EXPERTS, 0].set(p["router_b"])
    xm, h2, ri, rg, cnt = _out_projection(
        ya, yb, yc, ocmp, osel, owin, gates, x, mod_l, p["w_out"].astype(BF16),
        p["norm_moe_g"], rwt, rb, tl["tm"])
    t = b * s
    counts = cnt[:N_EXPERTS, 0]
    rows = tl["rows"]
    meta, offs, nvmax = _visit_plan(counts, rows, 2 * t // rows)
    onehot = ri[:, 0:2, :, None] == jnp.arange(N_EXPERTS)[None, None, None, :]
    pos = (jnp.sum(jnp.where(onehot, offs[None, None, None, :], 0), axis=-1) + ri[:, 2:4]).astype(jnp.int32)
    pos0, pos1 = pos[:, 0].reshape(t), pos[:, 1].reshape(t)
    rgt = rg.transpose(0, 2, 1).reshape(t, 8)
    xs = _dispatch(h2.reshape(t, d), pos0, pos1, tl["tm_moe"])
    ys = _experts(xs, meta, l, p["exp_w_gate"], p["exp_w_up"], p["exp_w_down"], rows, nvmax)
    out = _combine(final, pos0, pos1, xm.reshape(t, d), rgt, mod_l, p["norm_final_g"], ys, tl["tm_moe"])
    return out.reshape(b, s, d)


def kernel(x, c, ada_w, ada_b, norm_attn_g, norm_moe_g, norm_final_g, w_in, w_out, diff_lam_q1, diff_lam_k1, diff_lam_q2, diff_lam_k2, diff_subln_g, swa_sinks, fox_forget_b, nsa_cmp_pos, nsa_cmp_w1, nsa_cmp_b1, nsa_cmp_w2, router_w, router_b, exp_w_gate, exp_w_up, exp_w_down):
    depth = ada_w.shape[0]
    s = x.shape[1]
    mod = _modulation(c, ada_w, ada_b)
    consts = dict(cover=_cover_matrix(s), eneg=_expand_neg(s))
    for l in range(depth):
        p = dict(
            norm_attn_g=norm_attn_g[l], norm_moe_g=norm_moe_g[l], norm_final_g=norm_final_g,
            w_in=w_in[l], w_out=w_out[l],
            diff_lam_q1=diff_lam_q1[l], diff_lam_k1=diff_lam_k1[l],
            diff_lam_q2=diff_lam_q2[l], diff_lam_k2=diff_lam_k2[l], diff_subln_g=diff_subln_g[l],
            swa_sinks=swa_sinks[l], fox_forget_b=fox_forget_b[l],
            nsa_cmp_pos=nsa_cmp_pos[l], nsa_cmp_w1=nsa_cmp_w1[l], nsa_cmp_b1=nsa_cmp_b1[l],
            nsa_cmp_w2=nsa_cmp_w2[l], router_w=router_w, router_b=router_b,
            exp_w_gate=exp_w_gate, exp_w_up=exp_w_up, exp_w_down=exp_w_down,
        )
        x = _layer(l, x, mod[l], p, consts, final=(l == depth - 1))
    return x
```

```python
import functools
import math

import numpy as np
import jax
import jax.numpy as jnp
from jax import lax
from jax.experimental import pallas as pl
from jax.experimental.pallas import tpu as pltpu

F32 = jnp.float32
BF16 = jnp.bfloat16

LANES = 128
HEAD_DIM = 64
N_MIXERS = 4
HEADS = 4
GROUP_WIDTH = HEADS * HEAD_DIM
DIFF_DK = HEAD_DIM // 2
SWA_WINDOW = 128
SWA_KV_HEADS = 2
CMP_BLOCK = 32
CMP_STRIDE = 16
CMP_HIDDEN = 2 * HEAD_DIM
SEL_BLOCK = 64
SEL_TOPK = 16
NSA_WINDOW = 512
FORCE_SCORE = 1e4
NEG_INF = -1e30
MASK_BIG = 2.0 ** 100
N_EXPERTS = 64
EXPERTS_PER_GROUP = 8
EPS = 1e-6
LOG2E = math.log2(math.e)
VMEM_LIMIT = 56 * 1024 * 1024
WIDE_TILES = 4

IN_SPLITS = (
    GROUP_WIDTH, GROUP_WIDTH, GROUP_WIDTH,
    GROUP_WIDTH, SWA_KV_HEADS * HEAD_DIM, SWA_KV_HEADS * HEAD_DIM,
    GROUP_WIDTH, GROUP_WIDTH, GROUP_WIDTH, HEADS,
    GROUP_WIDTH, HEAD_DIM, HEAD_DIM, HEAD_DIM, HEAD_DIM,
    HEAD_DIM, HEAD_DIM, 3 * HEADS,
)

_NT = (((1,), (1,)), ((), ()))


def _alibi_slopes():
    n = 3 * HEADS
    m = 2.0 ** (-8.0 * np.arange(1, n + 1) / n)
    return m.reshape(HEADS, 3).T


SLOPES = _alibi_slopes()


def _dot(a, b):
    return jnp.dot(a, b, preferred_element_type=F32)


def _dot_nt(a, b):
    return lax.dot_general(a, b, _NT, preferred_element_type=F32)


def _split_bf16(a):
    hi = a.astype(BF16)
    lo = (a - hi.astype(F32)).astype(BF16)
    return hi, lo


def _dot_f32(a, b, nt=False):
    d = _dot_nt if nt else _dot
    ah, al = _split_bf16(a)
    bh, bl = _split_bf16(b)
    return d(ah, bh) + (d(ah, bl) + d(al, bh))


def _cparams(sem):
    return pltpu.CompilerParams(dimension_semantics=sem, vmem_limit_bytes=VMEM_LIMIT)


def _mod_kernel(c_ref, w_ref, b_ref, o_ref):
    c = c_ref[...]
    cond = c * jax.nn.sigmoid(c)
    o_ref[0] = _dot_f32(cond, w_ref[0]) + b_ref[0]


def _modulation(c, ada_w, ada_b):
    depth, d, n = ada_w.shape
    b = c.shape[0]
    rows = 8
    cp = jnp.zeros((rows, d), F32).at[:b].set(c)
    tn = 1536
    out = pl.pallas_call(
        _mod_kernel,
        grid=(depth, n // tn),
        in_specs=[
            pl.BlockSpec((rows, d), lambda l, j: (0, 0)),
            pl.BlockSpec((1, d, tn), lambda l, j: (l, 0, j)),
            pl.BlockSpec((1, 1, tn), lambda l, j: (l, 0, j)),
        ],
        out_specs=pl.BlockSpec((1, rows, tn), lambda l, j: (l, 0, j)),
        out_shape=jax.ShapeDtypeStruct((depth, rows, n), F32),
        compiler_params=_cparams(("arbitrary", "arbitrary")),
        name="adaln_mod",
    )(cp, ada_w, ada_b.reshape(depth, 1, n))
    return out[:, :b].reshape(depth, b, 6, d)


W_A, W_B, W_C, W_D, W_G, W_CMP = 768, 768, 768, 768, 128, 128
W_OFFS = np.cumsum((0, W_A, W_B, W_C, W_D, W_G, W_CMP))
W_TOTAL = int(W_OFFS[-1])


def _prep_w_in(w):
    offs = np.cumsum((0,) + IN_SPLITS)
    col = lambda i: w[:, int(offs[i]):int(offs[i + 1])]
    dup = lambda t: jnp.concatenate([t, t], axis=1)
    kb, vb = col(4), col(5)
    gates = jnp.zeros((w.shape[0], W_G), F32).at[:, :3 * HEADS].set(col(17))
    qs64 = LOG2E * HEAD_DIM ** -0.5
    parts = [
        col(0) * (LOG2E * DIFF_DK ** -0.5), col(1), col(2),
        col(3) * qs64,
        dup(kb[:, :HEAD_DIM]), dup(kb[:, HEAD_DIM:]),
        dup(vb[:, :HEAD_DIM]), dup(vb[:, HEAD_DIM:]),
        col(6) * qs64, col(7), col(8),
        col(10) * qs64, dup(col(13)), dup(col(14)), dup(col(15)), dup(col(16)),
    ] + [gates, col(11), col(12)]
    wp = jnp.concatenate(parts, axis=1).astype(BF16)
    wf = jnp.zeros((8, w.shape[0]), F32).at[:HEADS].set(col(9).T).astype(BF16)
    return wp, wf


def _lane_cumsum(y):
    n = y.shape[1]
    lane = lax.broadcasted_iota(jnp.int32, y.shape, 1)
    sh = 1
    while sh < n:
        y = y + jnp.where(lane >= sh, pltpu.roll(y, sh, 1), 0.0)
        sh *= 2
    return y


def _inproj_kernel(x_ref, mod_ref, g_ref, w_ref, wf_ref, bf_ref,
                   oa, ob, oc, od, og, ocmp, ofc, carry):
    i = pl.program_id(1)
    x = x_ref[0]
    ms = jnp.mean(x * x, axis=-1, keepdims=True)
    y = x * lax.rsqrt(ms + EPS) * g_ref[...]
    h = y * (1.0 + mod_ref[1:2, :]) + mod_ref[0:1, :]
    hb = h.astype(BF16)
    o = W_OFFS
    oa[0] = _dot(hb, w_ref[:, o[0]:o[1]]).astype(BF16)
    ob[0] = _dot(hb, w_ref[:, o[1]:o[2]]).astype(BF16)
    oc[0] = _dot(hb, w_ref[:, o[2]:o[3]]).astype(BF16)
    od[0] = _dot(hb, w_ref[:, o[3]:o[4]]).astype(BF16)
    og[0] = jax.nn.sigmoid(_dot(hb, w_ref[:, o[4]:o[5]])).astype(BF16)
    kvc = _dot(hb, w_ref[:, o[5]:o[6]])
    ocmp[0, 0] = kvc[:, :HEAD_DIM]
    ocmp[1, 0] = kvc[:, HEAD_DIM:]
    fl = _dot_nt(wf_ref[...], hb) + bf_ref[:, 0:1]
    ls = jnp.minimum(fl, 0.0) - jnp.log1p(jnp.exp(-jnp.abs(fl)))

    @pl.when(i == 0)
    def _():
        carry[...] = jnp.zeros_like(carry)

    cs = _lane_cumsum(ls) + carry[:, 0:1]
    ofc[0] = cs
    carry[...] = jnp.broadcast_to(cs[:, cs.shape[1] - 1:], carry.shape)


def _in_projection(x, mod_l, g, wp, wf, bf, tm):
    b, s, d = x.shape
    bf8 = jnp.zeros((8, LANES), F32).at[:HEADS, :].set(bf[:, None])
    outs = pl.pallas_call(
        _inproj_kernel,
        grid=(b, s // tm),
        in_specs=[
            pl.BlockSpec((1, tm, d), lambda bi, i: (bi, i, 0)),
            pl.BlockSpec((None, 6, d), lambda bi, i: (bi, 0, 0)),
            pl.BlockSpec((1, d), lambda bi, i: (0, 0)),
            pl.BlockSpec((d, W_TOTAL), lambda bi, i: (0, 0)),
            pl.BlockSpec((8, d), lambda bi, i: (0, 0)),
            pl.BlockSpec((8, LANES), lambda bi, i: (0, 0)),
        ],
        out_specs=[
            pl.BlockSpec((1, tm, W_A), lambda bi, i: (bi, i, 0)),
            pl.BlockSpec((1, tm, W_B), lambda bi, i: (bi, i, 0)),
            pl.BlockSpec((1, tm, W_C), lambda bi, i: (bi, i, 0)),
            pl.BlockSpec((1, tm, W_D), lambda bi, i: (bi, i, 0)),
            pl.BlockSpec((1, tm, W_G), lambda bi, i: (bi, i, 0)),
            pl.BlockSpec((2, 1, tm, HEAD_DIM), lambda bi, i: (0, bi, i, 0)),
            pl.BlockSpec((1, 8, tm), lambda bi, i: (bi, 0, i)),
        ],
        out_shape=[
            jax.ShapeDtypeStruct((b, s, W_A), BF16),
            jax.ShapeDtypeStruct((b, s, W_B), BF16),
            jax.ShapeDtypeStruct((b, s, W_C), BF16),
            jax.ShapeDtypeStruct((b, s, W_D), BF16),
            jax.ShapeDtypeStruct((b, s, W_G), BF16),
            jax.ShapeDtypeStruct((2, b, s, HEAD_DIM), F32),
            jax.ShapeDtypeStruct((b, 8, s), F32),
        ],
        scratch_shapes=[pltpu.VMEM((8, LANES), F32)],
        compiler_params=_cparams(("arbitrary", "arbitrary")),
        name="in_projection",
    )(x, mod_l, g.reshape(1, d), wp, wf, bf8)
    return outs


def _flash_kernel(kind, t, layer_idx, *refs):
    it = iter(refs)
    q_ref, k_ref, v_ref = next(it), next(it), next(it)
    fc_ref = next(it) if kind == "C" else None
    un_ref, use_ref, head_ref, en_ref = ((next(it), next(it), next(it), next(it)) if kind == "Dsel"
                                         else (None, None, None, None))
    lam_ref, sg_ref = (next(it), next(it)) if kind == "A" else (None, None)
    o_ref, m_scr, acc_scr = next(it), next(it), next(it)

    p = pl.program_id(1)
    i = pl.program_id(2)
    nstream = 4 if kind == "A" else 2
    width = LANES // nstream
    mixer = {"A": 0, "Dsel": 2}.get(kind)

    q2 = q_ref[0]
    lane = lax.broadcasted_iota(jnp.int32, (t, LANES), 1)
    zero = jnp.zeros_like(q2)
    qms = [jnp.where((lane >= s * width) & (lane < (s + 1) * width), q2, zero)
           for s in range(nstream)]
    if kind == "Dsel":
        qms = [jnp.concatenate([qm, un_ref[0]], axis=1) for qm in qms]
    qcat = jnp.concatenate(qms, axis=0)

    def head_of(s):
        return s // 2 if kind == "A" else s

    slopes = None
    if mixer is not None:
        sl = SLOPES[mixer] * LOG2E
        slopes = [jnp.where(p == 0, float(sl[head_of(s)]), float(sl[2 + head_of(s)]))
                  for s in range(nstream)]

    for s in range(nstream):
        m_scr[s] = jnp.full((t, LANES), NEG_INF, F32)
        acc_scr[s] = jnp.zeros((t, LANES), F32)

    def tile(first_key, nk, diag):
        start = pl.multiple_of(first_key, t)
        k2 = k_ref[0, pl.ds(start, nk), :]
        if kind == "Dsel":
            k2 = jnp.concatenate([k2, en_ref[pl.ds(start, nk), :]], axis=1)
        v2 = v_ref[0, pl.ds(start, nk), :]
        vlane = lax.broadcasted_iota(jnp.int32, (nk, LANES), 1)
        one = jnp.ones_like(v2)
        vhalf = [jnp.where(vlane < HEAD_DIM, v2, one), jnp.where(vlane < HEAD_DIM, one, v2)]
        col = lax.broadcasted_iota(jnp.int32, (1, nk), 1)
        rel = (start - i * t + col).astype(F32)
        if diag:
            mask = (lax.broadcasted_iota(jnp.int32, (t, t), 0)
                    >= lax.broadcasted_iota(jnp.int32, (t, t), 1))
        sc_all = _dot_nt(qcat, k2)
        for s in range(nstream):
            sc = sc_all[s * t:(s + 1) * t]
            if slopes is not None:
                sc = sc + slopes[s] * rel
            if kind == "C":
                sc = sc - LOG2E * fc_ref[0, pl.ds(2 * p + s, 1), pl.ds(start, nk)]
            if diag:
                sc = jnp.where(mask, sc, NEG_INF)
            m_prev = m_scr[s]
            m_next = jnp.maximum(m_prev, jnp.max(sc, axis=1, keepdims=True))
            alpha = jnp.exp2(m_prev - m_next)
            pexp = jnp.exp2((sc - jnp.tile(m_next, (1, nk // LANES))).astype(BF16))
            acc_scr[s] = alpha * acc_scr[s] + _dot(pexp, vhalf[head_of(s)])
            m_scr[s] = m_next

    tile(i * t, t, True)

    if kind == "Dsel":
        nq = pl.num_programs(2)
        base = (pl.program_id(0) * nq + i) * nq

        def body(j, carry):
            @pl.when(use_ref[base + j] != 0)
            def _():
                tile(j * t, t, False)
            return carry

        short = (i > 0) & (head_ref[pl.program_id(0) * nq + i] != 0)

        @pl.when(short)
        def _():
            tile(0, LANES, False)

        lax.fori_loop(jnp.where(short, 1, 0), i, body, 0)
    else:
        wide = WIDE_TILES * t

        def body(j, carry):
            tile(j * wide, wide, False)
            return carry

        lax.fori_loop(0, i // WIDE_TILES, body, 0)

        done = i // WIDE_TILES * WIDE_TILES
        width = WIDE_TILES // 2
        while width >= 1:
            take = ((i - done) // width) * width

            @pl.when(take > 0)
            def _(done=done, width=width):
                tile(done * t, width * t, False)

            done = done + take
            width //= 2

    outs = [acc_scr[s] / pltpu.roll(acc_scr[s], HEAD_DIM, 1) for s in range(nstream)]
    if kind == "A":
        lam_init = 0.8 - 0.6 * math.exp(-0.3 * layer_idx)
        t1 = jnp.sum(lam_ref[0:1, :] * lam_ref[1:2, :], axis=1, keepdims=True)
        t2 = jnp.sum(lam_ref[2:3, :] * lam_ref[3:4, :], axis=1, keepdims=True)
        lam = jnp.exp(t1) - jnp.exp(t2) + lam_init
        d0 = outs[0] - lam * outs[1]
        d1 = outs[2] - lam * outs[3]
        lo_half = lane < HEAD_DIM
        o = jnp.where(lo_half, d0, d1)
        sq = o * o
        ss_lo = jnp.sum(jnp.where(lo_half, sq, 0.0), axis=1, keepdims=True)
        ss_hi = jnp.sum(jnp.where(lo_half, 0.0, sq), axis=1, keepdims=True)
        ms = jnp.where(lo_half, ss_lo, ss_hi) * (1.0 / HEAD_DIM)
        o = o * lax.rsqrt(ms + EPS) * sg_ref[...] * (1.0 - lam_init)
    else:
        o = jnp.where(lane < HEAD_DIM, outs[0], outs[1])
    o_ref[0] = o.astype(o_ref.dtype)


def _flash(kind, t, layer_idx, q_arr, qoff, k_arr, koff, v_arr, voff, extras):
    b, s, _ = q_arr.shape
    nstream = 4 if kind == "A" else 2
    shared_kv = kind == "Dsel"
    kv_idx = (lambda off: (lambda bi, p, i: (bi, 0, off))) if shared_kv else \
             (lambda off: (lambda bi, p, i: (bi, 0, off + p)))
    in_specs = [
        pl.BlockSpec((1, t, LANES), lambda bi, p, i: (bi, i, qoff + p)),
        pl.BlockSpec((1, s, LANES), kv_idx(koff)),
        pl.BlockSpec((1, s, LANES), kv_idx(voff)),
    ]
    args = [q_arr, k_arr, v_arr]
    if kind == "C":
        fcum, = extras
        in_specs.append(pl.BlockSpec((1, 8, s), lambda bi, p, i: (bi, 0, 0)))
        args.append(fcum)
    elif kind == "Dsel":
        unsel, eneg = extras
        nsel = unsel.shape[-1]
        nq, per = s // t, t // SEL_BLOCK
        picked = unsel.reshape(b, nq, t, nq, per) == 0
        used = picked.any(axis=(2, 4)).astype(jnp.int32).reshape(-1)
        head_only = jnp.logical_not(picked[:, :, :, 0, LANES // SEL_BLOCK:].any(axis=(2, 3)))
        head_only = head_only.astype(jnp.int32).reshape(-1)
        in_specs.append(pl.BlockSpec((1, t, nsel), lambda bi, p, i: (bi, i, 0)))
        in_specs.append(pl.BlockSpec(memory_space=pltpu.SMEM))
        in_specs.append(pl.BlockSpec(memory_space=pltpu.SMEM))
        in_specs.append(pl.BlockSpec((s, nsel), lambda bi, p, i: (0, 0)))
        args += [unsel, used, head_only, eneg]
    elif kind == "A":
        lam8, sg = extras
        in_specs.append(pl.BlockSpec((8, LANES), lambda bi, p, i: (0, 0)))
        in_specs.append(pl.BlockSpec((1, LANES), lambda bi, p, i: (0, 0)))
        args += [lam8, sg]
    return pl.pallas_call(
        functools.partial(_flash_kernel, kind, t, layer_idx),
        grid=(b, 2, s // t),
        in_specs=in_specs,
        out_specs=pl.BlockSpec((1, t, LANES), lambda bi, p, i: (bi, i, p)),
        out_shape=jax.ShapeDtypeStruct((b, s, 2 * LANES), BF16),
        scratch_shapes=[pltpu.VMEM((nstream, t, LANES), F32)] * 2,
        compiler_params=_cparams(("arbitrary", "arbitrary", "arbitrary")),
        name="flash_" + kind,
    )(*args)


def _window_kernel(kind, t, wpad, *refs):
    it = iter(refs)
    q_ref, k_ref, v_ref = next(it), next(it), next(it)
    sink_ref = next(it) if kind == "B" else None
    o_ref = next(it)
    i = pl.program_id(1)
    window = SWA_WINDOW if kind == "B" else NSA_WINDOW
    sl = SLOPES[1 if kind == "B" else 2] * LOG2E
    nk = wpad + t
    start = pl.multiple_of(jnp.maximum(i * t - wpad, 0), LANES)
    lane = lax.broadcasted_iota(jnp.int32, (t, LANES), 1)
    vlane = lax.broadcasted_iota(jnp.int32, (nk, LANES), 1)
    dist = ((i * t - start) + lax.broadcasted_iota(jnp.int32, (t, nk), 0)
            - lax.broadcasted_iota(jnp.int32, (t, nk), 1))
    valid = (dist >= 0) & (dist < window)
    rel = (start - i * t + lax.broadcasted_iota(jnp.int32, (1, nk), 1)).astype(F32)
    qrow = lax.broadcasted_iota(jnp.int32, (t, 1), 0).astype(F32)

    def masked_q(pair, g):
        q2 = q_ref[0, :, pair * LANES:(pair + 1) * LANES]
        return jnp.where((lane >= g * HEAD_DIM) & (lane < (g + 1) * HEAD_DIM), q2, jnp.zeros_like(q2))

    def kv_tiles(pair):
        cols = slice(pair * LANES, (pair + 1) * LANES) if kind == "B" else slice(0, LANES)
        k2 = k_ref[0, pl.ds(start, nk), cols]
        v2 = v_ref[0, pl.ds(start, nk), cols]
        one = jnp.ones_like(v2)
        return k2, [jnp.where(vlane < HEAD_DIM, v2, one), jnp.where(vlane < HEAD_DIM, one, v2)]

    groups = [[(0, 0), (0, 1)], [(1, 0), (1, 1)]] if kind == "B" else [[(0, 0), (0, 1), (1, 0), (1, 1)]]
    outs = {}
    for group in groups:
        k2, vhalf = kv_tiles(group[0][0])
        sc_all = _dot_nt(jnp.concatenate([masked_q(pair, g) for pair, g in group], axis=0), k2)
        for n, (pair, g) in enumerate(group):
            h = 2 * pair + g
            slope = float(sl[h])
            sc = jnp.where(valid, sc_all[n * t:(n + 1) * t] + slope * rel, NEG_INF)
            m = jnp.max(sc, axis=1, keepdims=True)
            if kind == "B":
                sink = LOG2E * sink_ref[h] + slope * qrow
                m = jnp.maximum(m, sink)
            pv = _dot(jnp.exp2((sc - m).astype(BF16)), vhalf[g])
            l = pltpu.roll(pv, HEAD_DIM, 1)
            if kind == "B":
                l = l + jnp.exp2(sink - m)
            outs[h] = pv / l
    for pair in range(2):
        o_ref[0, :, pair * LANES:(pair + 1) * LANES] = jnp.where(
            lane < HEAD_DIM, outs[2 * pair], outs[2 * pair + 1]).astype(o_ref.dtype)


def _window(kind, t, wpad, q_arr, k_arr, koff, v_arr, voff, extras):
    b, s, _ = q_arr.shape
    kw = 2 * LANES if kind == "B" else LANES
    in_specs = [
        pl.BlockSpec((1, t, 2 * LANES), lambda bi, i: (bi, i, 0)),
        pl.BlockSpec((1, s, kw), lambda bi, i: (bi, 0, koff)),
        pl.BlockSpec((1, s, kw), lambda bi, i: (bi, 0, voff)),
    ]
    args = [q_arr, k_arr, v_arr]
    if kind == "B":
        in_specs.append(pl.BlockSpec(memory_space=pltpu.SMEM))
        args.append(extras[0])
    return pl.pallas_call(
        functools.partial(_window_kernel, kind, t, wpad),
        grid=(b, s // t),
        in_specs=in_specs,
        out_specs=pl.BlockSpec((1, t, 2 * LANES), lambda bi, i: (bi, i, 0)),
        out_shape=jax.ShapeDtypeStruct((b, s, 2 * LANES), BF16),
        compiler_params=_cparams(("arbitrary", "arbitrary")),
        name="window_" + kind,
    )(*args)


def _gelu_tanh(x):
    return 0.5 * x * (1.0 + jnp.tanh(math.sqrt(2.0 / math.pi) * (x + 0.044715 * x * x * x)))


def _compress_kernel(t_ref, w1_ref, pos_ref, b1_ref, w2_ref, o_ref):
    tr = t_ref[0, 0]
    half = tr.shape[1]
    n = tr.shape[0]
    w1 = w1_ref[0]
    u = _dot_f32(tr, w1[:half])
    v = _dot_f32(tr, w1[half:])
    cpos = _dot_f32(pos_ref[0], w1)[0:1] + b1_ref[0]
    hid = u + pltpu.roll(v, n - 1, 0) + cpos
    o_ref[0, 0] = _dot_f32(_gelu_tanh(hid), w2_ref[0])


def _compress(kv_r, w1, pos8, b1, w2d):
    two, b, n, dd = kv_r.shape
    return pl.pallas_call(
        _compress_kernel,
        grid=(two, b),
        in_specs=[
            pl.BlockSpec((1, 1, n, dd), lambda c, bi: (c, bi, 0, 0)),
            pl.BlockSpec((1, 2 * dd, CMP_HIDDEN), lambda c, bi: (c, 0, 0)),
            pl.BlockSpec((1, 8, 2 * dd), lambda c, bi: (c, 0, 0)),
            pl.BlockSpec((1, 1, CMP_HIDDEN), lambda c, bi: (c, 0, 0)),
            pl.BlockSpec((1, CMP_HIDDEN, LANES), lambda c, bi: (c, 0, 0)),
        ],
        out_specs=pl.BlockSpec((1, 1, n, LANES), lambda c, bi: (c, bi, 0, 0)),
        out_shape=jax.ShapeDtypeStruct((two, b, n, LANES), F32),
        compiler_params=_cparams(("arbitrary", "arbitrary")),
        name="nsa_compress",
    )(kv_r, w1, pos8, b1, w2d)


def _cmp_kernel(t, first_tile, ncp_total, q_ref, kc_ref, vc_ref, cov_ref, o_ref, un_ref):
    i = first_tile + pl.program_id(1)
    ncp = kc_ref.shape[2]
    nsel = cov_ref.shape[0]
    kc = kc_ref[0, 0]
    vc = vc_ref[0, 0]
    kch, kcl = _split_bf16(kc)
    vcb = vc.astype(BF16)
    lane = lax.broadcasted_iota(jnp.int32, (t, LANES), 1)
    tq = i * t + lax.broadcasted_iota(jnp.int32, (t, ncp), 0)
    nidx = lax.broadcasted_iota(jnp.int32, (t, ncp), 1)
    cmp_end = nidx * CMP_STRIDE + (CMP_BLOCK - 1)
    valid = (tq >= cmp_end) & (nidx < ncp_total - 1)
    end_row = cmp_end[0:1, :].astype(F32)
    psum = jnp.zeros((t, ncp), F32)
    outs = []
    for pair in range(2):
        q2 = q_ref[0, :, pair * LANES:(pair + 1) * LANES]
        for g in range(2):
            h = 2 * pair + g
            qm = jnp.where((lane >= g * HEAD_DIM) & (lane < (g + 1) * HEAD_DIM), q2, jnp.zeros_like(q2))
            sc = _dot_nt(qm, kch) + _dot_nt(qm, kcl) + float(SLOPES[2][h] * LOG2E) * end_row
            sc = jnp.where(valid, sc, NEG_INF)
            m = jnp.max(sc, axis=1, keepdims=True)
            pe = jnp.exp2(sc - m)
            l = jnp.sum(pe, axis=1, keepdims=True)
            pc = pe * jnp.where(m > 0.5 * NEG_INF, 1.0 / l, 0.0)
            psum = psum + pc
            outs.append(_dot(pc.astype(BF16), vcb))
    lo_half = lane < HEAD_DIM
    o_ref[0, :, 0:LANES] = jnp.where(lo_half, outs[0], outs[1]).astype(o_ref.dtype)
    o_ref[0, :, LANES:2 * LANES] = jnp.where(lo_half, outs[2], outs[3]).astype(o_ref.dtype)
    ph, plo = _split_bf16(psum)
    cov = cov_ref[...]
    imp = _dot_nt(cov, ph) + _dot_nt(cov, plo)
    blk = lax.broadcasted_iota(jnp.int32, (nsel, t), 0)
    cur = (i * t + lax.broadcasted_iota(jnp.int32, (nsel, t), 1)) // SEL_BLOCK
    forced = (blk == 0) | (blk == cur) | (blk == cur - 1)
    score = jnp.where(forced, FORCE_SCORE, jnp.where(blk <= cur, imp, -1.0))
    unsel = jnp.ones((nsel, t), F32)
    for _ in range(min(SEL_TOPK, nsel)):
        mx = jnp.max(score, axis=0, keepdims=True)
        idx = jnp.min(jnp.where(score == mx, blk, nsel), axis=0, keepdims=True)
        hit = blk == idx
        unsel = jnp.where(hit, 0.0, unsel)
        score = jnp.where(hit, -3.0, score)
    un_ref[0] = unsel.T.astype(un_ref.dtype)


CMP_RANGES = 4


def _cmp_attention(t, qd, kvc, cover_t):
    b, s, _ = qd.shape
    ncp_total = kvc.shape[2]
    nsel = cover_t.shape[0]
    steps = s // t // CMP_RANGES
    outs = []
    for r in range(CMP_RANGES):
        ncp = (r + 1) * ncp_total // CMP_RANGES
        first = r * steps
        outs.append(pl.pallas_call(
            functools.partial(_cmp_kernel, t, first, ncp_total),
            grid=(b, steps),
            in_specs=[
                pl.BlockSpec((1, t, 2 * LANES), lambda bi, i, first=first: (bi, first + i, 0)),
                pl.BlockSpec((1, 1, ncp, LANES), lambda bi, i: (0, bi, 0, 0)),
                pl.BlockSpec((1, 1, ncp, LANES), lambda bi, i: (1, bi, 0, 0)),
                pl.BlockSpec((nsel, ncp), lambda bi, i: (0, 0)),
            ],
            out_specs=[
                pl.BlockSpec((1, t, 2 * LANES), lambda bi, i: (bi, i, 0)),
                pl.BlockSpec((1, t, nsel), lambda bi, i: (bi, i, 0)),
            ],
            out_shape=[
                jax.ShapeDtypeStruct((b, steps * t, 2 * LANES), BF16),
                jax.ShapeDtypeStruct((b, steps * t, nsel), BF16),
            ],
            compiler_params=_cparams(("arbitrary", "arbitrary")),
            name="nsa_cmp_select",
        )(qd, kvc, kvc, cover_t))
    return (jnp.concatenate([o for o, _ in outs], axis=1),
            jnp.concatenate([u for _, u in outs], axis=1))


def _cover_matrix(s):
    ncp, nsel = s // CMP_STRIDE, s // SEL_BLOCK
    cs = np.arange(ncp)[:, None] * CMP_STRIDE
    ss = np.arange(nsel)[None, :] * SEL_BLOCK
    cov = np.clip(np.minimum(cs + CMP_BLOCK, ss + SEL_BLOCK) - np.maximum(cs, ss), 0, None) / CMP_BLOCK
    cov[ncp - 1] = 0.0
    return jnp.asarray(cov.T, BF16)


def _expand_neg(s):
    nsel = s // SEL_BLOCK
    e = (np.arange(s)[:, None] // SEL_BLOCK) == np.arange(nsel)[None, :]
    return jnp.asarray(np.where(e, -MASK_BIG, 0.0), BF16)


def _outproj_kernel(ya, yb, yc, ocmp, osel, owin, gt, gx_ref, x_ref, mod_ref, wo_ref, g_ref,
                    rwt_ref, rb_ref, xo_ref, h_ref, ri_ref, rg_ref, cnt_ref, carry):
    first = (pl.program_id(0) == 0) & (pl.program_id(1) == 0)
    gw = GROUP_WIDTH
    g = _dot(gt[0], gx_ref[...])
    yd = (g[:, 0:gw] * ocmp[0].astype(F32) + g[:, gw:2 * gw] * osel[0].astype(F32)
          + g[:, 2 * gw:3 * gw] * owin[0].astype(F32))
    ycat = jnp.concatenate([ya[0], yb[0], yc[0], yd.astype(BF16)], axis=1)
    y = _dot(ycat, wo_ref[...])
    x = x_ref[0] + mod_ref[2:3, :] * y
    xo_ref[0] = x
    ms = jnp.mean(x * x, axis=-1, keepdims=True)
    h = x * lax.rsqrt(ms + EPS) * g_ref[...]
    h = h * (1.0 + mod_ref[4:5, :]) + mod_ref[3:4, :]
    h_ref[0] = h
    tm = h.shape[0]
    ng = LANES // EXPERTS_PER_GROUP
    big = 4 * LANES
    logits = _dot_f32(rwt_ref[...], h, nt=True)
    eidx = lax.broadcasted_iota(jnp.int32, (LANES, tm), 0)
    aff = jax.nn.sigmoid(logits)
    sel = jnp.where(eidx < N_EXPERTS, aff + rb_ref[:, 0:1], NEG_INF)
    sel3 = sel.reshape(ng, EXPERTS_PER_GROUP, tm)
    e3 = eidx.reshape(ng, EXPERTS_PER_GROUP, tm)
    g1 = jnp.max(sel3, axis=1, keepdims=True)
    i1 = jnp.min(jnp.where(sel3 == g1, e3, big), axis=1, keepdims=True)
    sel_b = jnp.where(e3 == i1, NEG_INF, sel3)
    g2 = jnp.max(sel_b, axis=1, keepdims=True)
    i2 = jnp.min(jnp.where(sel_b == g2, e3, big), axis=1, keepdims=True)
    gs = g1 + g2
    gidx = lax.broadcasted_iota(jnp.int32, (ng, 1, tm), 0)
    gm = jnp.max(gs, axis=0, keepdims=True)
    best = gidx == jnp.min(jnp.where(gs == gm, gidx, big), axis=0, keepdims=True)
    e1 = jnp.min(jnp.where(best, i1, big), axis=0)
    e2 = jnp.min(jnp.where(best, i2, big), axis=0)
    oh1 = eidx == e1
    oh2 = eidx == e2
    a1 = jnp.sum(jnp.where(oh1, aff, 0.0), axis=0, keepdims=True)
    a2 = jnp.sum(jnp.where(oh2, aff, 0.0), axis=0, keepdims=True)
    inv = 1.0 / (a1 + a2)
    @pl.when(first)
    def _():
        carry[...] = jnp.zeros_like(carry)

    ohs = jnp.where(oh1 | oh2, 1.0, 0.0)
    rr = lax.broadcasted_iota(jnp.int32, (tm, tm), 0)
    cc = lax.broadcasted_iota(jnp.int32, (tm, tm), 1)
    earlier = jnp.where(rr < cc, 1.0, 0.0).astype(BF16)
    before = _dot(ohs.astype(BF16), earlier) + carry[:, 0:1]
    r1 = jnp.sum(jnp.where(oh1, before, 0.0), axis=0, keepdims=True)
    r2 = jnp.sum(jnp.where(oh2, before, 0.0), axis=0, keepdims=True)
    total = carry[:, 0:1] + jnp.sum(ohs, axis=1, keepdims=True)
    carry[...] = jnp.broadcast_to(total, carry.shape)
    cnt_ref[...] = jnp.broadcast_to(total, cnt_ref.shape).astype(jnp.int32)
    row = lax.broadcasted_iota(jnp.int32, (8, tm), 0)
    ri_ref[0] = jnp.where(row == 0, e1, jnp.where(row == 1, e2, jnp.where(
        row == 2, r1.astype(jnp.int32), jnp.where(row == 3, r2.astype(jnp.int32), 0))))
    rg_ref[0] = jnp.where(row == 0, a1 * inv, jnp.where(row == 1, a2 * inv, 0.0))


def _gate_expander():
    e = np.zeros((W_G, 3 * GROUP_WIDTH), np.float32)
    for h in range(HEADS):
        for r in range(3):
            e[h * 3 + r, r * GROUP_WIDTH + h * HEAD_DIM:r * GROUP_WIDTH + (h + 1) * HEAD_DIM] = 1.0
    return jnp.asarray(e, BF16)


def _out_projection(ya, yb, yc, ocmp, osel, owin, gates, x, mod_l, wo, g, rwt, rb, tm):
    b, s, d = x.shape
    gw = GROUP_WIDTH
    tok = lambda w: pl.BlockSpec((1, tm, w), lambda bi, i: (bi, i, 0))
    full = lambda shp: pl.BlockSpec(shp, lambda bi, i: (0,) * len(shp))
    rowblk = pl.BlockSpec((1, 8, tm), lambda bi, i: (bi, 0, i))
    return pl.pallas_call(
        _outproj_kernel,
        grid=(b, s // tm),
        in_specs=[tok(gw)] * 6 + [tok(W_G), full((W_G, 3 * gw)), tok(d),
                                  pl.BlockSpec((None, 6, d), lambda bi, i: (bi, 0, 0)),
                                  full((d, d)), full((1, d)), full((LANES, d)), full((LANES, 1))],
        out_specs=[tok(d), tok(d), rowblk, rowblk, full((LANES, LANES))],
        out_shape=[
            jax.ShapeDtypeStruct((b, s, d), F32),
            jax.ShapeDtypeStruct((b, s, d), F32),
            jax.ShapeDtypeStruct((b, 8, s), jnp.int32),
            jax.ShapeDtypeStruct((b, 8, s), F32),
            jax.ShapeDtypeStruct((LANES, LANES), jnp.int32),
        ],
        scratch_shapes=[pltpu.VMEM((LANES, LANES), F32)],
        compiler_params=_cparams(("arbitrary", "arbitrary")),
        name="out_projection_router",
    )(ya, yb, yc, ocmp, osel, owin, gates, _gate_expander(), x, mod_l, wo, g.reshape(1, d), rwt, rb)


def _dispatch_kernel(p0_ref, p1_ref, h_ref, xs_ref, sem):
    tm = h_ref.shape[0]

    def issue(r, c):
        for pos_ref in (p0_ref, p1_ref):
            pltpu.make_async_copy(h_ref.at[pl.ds(r, 1), :],
                                  xs_ref.at[pl.ds(pos_ref[r], 1), :], sem).start()
        return c

    lax.fori_loop(0, tm, issue, 0, unroll=8)
    for _ in range(2):
        pltpu.make_async_copy(h_ref, xs_ref.at[pl.ds(0, tm), :], sem).wait()


def _dispatch(h, pos0, pos1, tm):
    t, d = h.shape
    return pl.pallas_call(
        _dispatch_kernel,
        grid=(t // tm,),
        in_specs=[
            pl.BlockSpec((tm,), lambda i: (i,), memory_space=pltpu.SMEM),
            pl.BlockSpec((tm,), lambda i: (i,), memory_space=pltpu.SMEM),
            pl.BlockSpec((tm, d), lambda i: (i, 0)),
        ],
        out_specs=pl.BlockSpec(memory_space=pl.ANY),
        out_shape=jax.ShapeDtypeStruct((2 * t, d), F32),
        scratch_shapes=[pltpu.SemaphoreType.DMA(())],
        compiler_params=_cparams(("arbitrary",)),
        name="moe_dispatch",
    )(pos0, pos1, h)


def _expert_kernel(layer, vb_ref, ve_ref, vlo_ref, vhi_ref, vord_ref, vnext_ref, nv_ref,
                   xs_ref, wg_hbm, wu_hbm, wd_hbm, ys_ref, wgf, wuf, wdf, wgb, wub, wdb, sem):
    v = pl.program_id(0)
    rows = xs_ref.shape[0]
    prev = jnp.maximum(v - 1, 0)
    new_expert = (v == 0) | (ve_ref[v] != ve_ref[prev])
    new_block = (v == 0) | (vb_ref[v] != vb_ref[prev])

    def weight_copies(e, slot):
        return [pltpu.make_async_copy(src.at[layer, e], dst.at[slot], sem.at[slot, n])
                for n, (src, dst) in enumerate(((wg_hbm, wgf), (wu_hbm, wuf), (wd_hbm, wdf)))]

    @pl.when(v < nv_ref[0])
    def _():
        @pl.when(new_expert)
        def _():
            slot = vord_ref[v] % 2

            @pl.when(v == 0)
            def _():
                for c in weight_copies(ve_ref[v], slot):
                    c.start()

            @pl.when(vnext_ref[v] >= 0)
            def _():
                for c in weight_copies(vnext_ref[v], 1 - slot):
                    c.start()

            for c in weight_copies(ve_ref[v], slot):
                c.wait()
            wgb[...] = wgf[slot].astype(BF16)
            wub[...] = wuf[slot].astype(BF16)
            wdb[...] = wdf[slot].astype(BF16)

        x = xs_ref[...].astype(BF16)
        hg = _dot(x, wgb[...])
        hu = _dot(x, wub[...])
        hm = (hg * jax.nn.sigmoid(hg) * hu).astype(BF16)
        y = _dot(hm, wdb[...])
        r = vb_ref[v] * rows + lax.broadcasted_iota(jnp.int32, (rows, 1), 0)
        mine = (r >= vlo_ref[v]) & (r < vhi_ref[v])

        @pl.when(new_block)
        def _():
            ys_ref[...] = jnp.where(mine, y, 0.0)

        @pl.when(jnp.logical_not(new_block))
        def _():
            ys_ref[...] = jnp.where(mine, y, ys_ref[...])


def _experts(xs, meta, layer, wg, wu, wd, rows, nvmax):
    n, d = xs.shape
    de = wg.shape[3]
    blk = lambda v, vb, ve, lo, hi, vo, vn, nv: (vb[v], 0)
    grid_spec = pltpu.PrefetchScalarGridSpec(
        num_scalar_prefetch=7,
        grid=(nvmax,),
        in_specs=[
            pl.BlockSpec((rows, d), blk),
            pl.BlockSpec(memory_space=pl.ANY),
            pl.BlockSpec(memory_space=pl.ANY),
            pl.BlockSpec(memory_space=pl.ANY),
        ],
        out_specs=pl.BlockSpec((rows, d), blk),
        scratch_shapes=[pltpu.VMEM((2, d, de), F32), pltpu.VMEM((2, d, de), F32), pltpu.VMEM((2, de, d), F32),
                        pltpu.VMEM((d, de), BF16), pltpu.VMEM((d, de), BF16), pltpu.VMEM((de, d), BF16),
                        pltpu.SemaphoreType.DMA((2, 3))],
    )
    return pl.pallas_call(
        functools.partial(_expert_kernel, layer),
        grid_spec=grid_spec,
        out_shape=jax.ShapeDtypeStruct((n, d), F32),
        compiler_params=_cparams(("arbitrary",)),
        name="moe_experts",
    )(*meta, xs, wg, wu, wd)


def _combine_kernel(final, p0_ref, p1_ref, x_ref, rg_ref, mod_ref, g_ref, ys_ref, o_ref, buf, sem):
    tm = x_ref.shape[0]

    def issue(r, c):
        for k, pos_ref in enumerate((p0_ref, p1_ref)):
            pltpu.make_async_copy(ys_ref.at[pl.ds(pos_ref[r], 1), :],
                                  buf.at[k, pl.ds(r, 1), :], sem).start()
        return c

    lax.fori_loop(0, tm, issue, 0, unroll=8)
    for k in range(2):
        pltpu.make_async_copy(ys_ref.at[pl.ds(0, tm), :], buf.at[k], sem).wait()
    rg = rg_ref[...]
    y = rg[:, 0:1] * buf[0] + rg[:, 1:2] * buf[1]
    x = x_ref[...] + mod_ref[5:6, :] * y
    if final:
        ms = jnp.mean(x * x, axis=-1, keepdims=True)
        x = x * lax.rsqrt(ms + EPS) * g_ref[...]
    o_ref[...] = x


def _combine(final, pos0, pos1, x, rg, mod_l, g, ys, tm):
    t, d = x.shape
    per_b = t // mod_l.shape[0] // tm
    return pl.pallas_call(
        functools.partial(_combine_kernel, final),
        grid=(t // tm,),
        in_specs=[
            pl.BlockSpec((tm,), lambda i: (i,), memory_space=pltpu.SMEM),
            pl.BlockSpec((tm,), lambda i: (i,), memory_space=pltpu.SMEM),
            pl.BlockSpec((tm, d), lambda i: (i, 0)),
            pl.BlockSpec((tm, 8), lambda i: (i, 0)),
            pl.BlockSpec((None, 6, d), lambda i: (i // per_b, 0, 0)),
            pl.BlockSpec((1, d), lambda i: (0, 0)),
            pl.BlockSpec(memory_space=pl.ANY),
        ],
        out_specs=pl.BlockSpec((tm, d), lambda i: (i, 0)),
        out_shape=jax.ShapeDtypeStruct((t, d), F32),
        scratch_shapes=[pltpu.VMEM((2, tm, d), F32), pltpu.SemaphoreType.DMA(())],
        compiler_params=_cparams(("arbitrary",)),
        name="moe_combine",
    )(pos0, pos1, x, rg, mod_l, g.reshape(1, d), ys)


def _visit_plan(counts, rows, nblocks):
    ne = counts.shape[0]
    nvmax = nblocks + ne - 1
    ends = jnp.cumsum(counts)
    offs = ends - counts
    b_lo = offs // rows
    b_hi = jnp.maximum(ends - 1, 0) // rows
    nvis = jnp.where(counts > 0, b_hi - b_lo + 1, 0)
    vend = jnp.cumsum(nvis)
    vstart = vend - nvis
    nv = vend[-1]
    v = jnp.minimum(jnp.arange(nvmax), nv - 1)
    e = jnp.sum((v[:, None] >= vend[None, :]).astype(jnp.int32), axis=1)
    onehot = e[:, None] == jnp.arange(ne)[None, :]
    pick = lambda a: jnp.sum(jnp.where(onehot, a[None, :], 0), axis=1)
    blk = pick(b_lo) + v - pick(vstart)
    lo = jnp.maximum(pick(offs), blk * rows)
    hi = jnp.minimum(pick(ends), (blk + 1) * rows)
    order = jnp.cumsum((counts > 0).astype(jnp.int32)) - 1
    ids = jnp.where(counts > 0, jnp.arange(ne), ne)
    later = jnp.concatenate([jnp.flip(lax.cummin(jnp.flip(ids)))[1:], jnp.full((1,), ne, ids.dtype)])
    nxt = jnp.where(later < ne, later, -1)
    i32 = lambda a: a.astype(jnp.int32)
    return (i32(blk), i32(e), i32(lo), i32(hi), i32(pick(order)), i32(pick(nxt)),
            i32(nv).reshape(1)), offs, nvmax


def _tiles(s):
    tm = min(512, s)
    return dict(tm=tm, t_full=min(512, s), t_win=min(256, s), t_cmp=min(512, s),
                tm_dispatch=min(2048, s), tm_combine=min(1024, s), rows=256)


def _layer(l, x, mod_l, p, consts, final):
    b, s, d = x.shape
    tl = _tiles(s)
    wp, wf = _prep_w_in(p["w_in"])
    qa, qb, qc, qd, gates, kvcmp, fcum = _in_projection(
        x, mod_l, p["norm_attn_g"], wp, wf, p["fox_forget_b"], tl["tm"])
    lam8 = jnp.zeros((8, LANES), F32).at[:4, :DIFF_DK].set(
        jnp.stack([p["diff_lam_q1"], p["diff_lam_k1"], p["diff_lam_q2"], p["diff_lam_k2"]]))
    sg = jnp.tile(p["diff_subln_g"], 2).reshape(1, LANES)
    ya = _flash("A", tl["t_full"], l, qa, 0, qa, 2, qa, 4, (lam8, sg))
    yb = _window("B", tl["t_win"], SWA_WINDOW, qb, qb, 1, qb, 2, (p["swa_sinks"],))
    yc = _flash("C", tl["t_full"], l, qc, 0, qc, 2, qc, 4, (fcum,))
    n16 = s // CMP_STRIDE
    kv_r = kvcmp.reshape(2, b, n16, CMP_STRIDE * HEAD_DIM)
    pos8 = jnp.zeros((2, 8, CMP_BLOCK * HEAD_DIM), F32).at[:, 0].set(
        p["nsa_cmp_pos"].reshape(2, CMP_BLOCK * HEAD_DIM))
    w2d = jnp.concatenate([p["nsa_cmp_w2"], p["nsa_cmp_w2"]], axis=-1)
    kvc = _compress(kv_r, p["nsa_cmp_w1"], pos8, p["nsa_cmp_b1"].reshape(2, 1, CMP_HIDDEN), w2d)
    ocmp, unsel = _cmp_attention(tl["t_cmp"], qd, kvc, consts["cover"])
    osel = _flash("Dsel", tl["t_full"], l, qd, 0, qd, 2, qd, 3, (unsel, consts["eneg"]))
    owin = _window("Dwin", tl["t_win"], NSA_WINDOW, qd, qd, 4, qd, 5, ())
    rwt = jnp.zeros((LANES, d), F32).at[:N_EXPERTS].set(p["router_w"].T)
    rb = jnp.zeros((LANES, 1), F32).at[:N_EXPERTS, 0].set(p["router_b"])
    xm, h2, ri, rg, cnt = _out_projection(
        ya, yb, yc, ocmp, osel, owin, gates, x, mod_l, p["w_out"].astype(BF16),
        p["norm_moe_g"], rwt, rb, tl["tm"])
    t = b * s
    counts = cnt[:N_EXPERTS, 0]
    rows = tl["rows"]
    meta, offs, nvmax = _visit_plan(counts, rows, 2 * t // rows)
    onehot = ri[:, 0:2, :, None] == jnp.arange(N_EXPERTS)[None, None, None, :]
    pos = (jnp.sum(jnp.where(onehot, offs[None, None, None, :], 0), axis=-1) + ri[:, 2:4]).astype(jnp.int32)
    pos0, pos1 = pos[:, 0].reshape(t), pos[:, 1].reshape(t)
    rgt = rg.transpose(0, 2, 1).reshape(t, 8)
    xs = _dispatch(h2.reshape(t, d), pos0, pos1, tl["tm_dispatch"])
    ys = _experts(xs, meta, l, p["exp_w_gate"], p["exp_w_up"], p["exp_w_down"], rows, nvmax)
    out = _combine(final, pos0, pos1, xm.reshape(t, d), rgt, mod_l, p["norm_final_g"], ys, tl["tm_combine"])
    return out.reshape(b, s, d)


def kernel(x, c, ada_w, ada_b, norm_attn_g, norm_moe_g, norm_final_g, w_in, w_out, diff_lam_q1, diff_lam_k1, diff_lam_q2, diff_lam_k2, diff_subln_g, swa_sinks, fox_forget_b, nsa_cmp_pos, nsa_cmp_w1, nsa_cmp_b1, nsa_cmp_w2, router_w, router_b, exp_w_gate, exp_w_up, exp_w_down):
    depth = ada_w.shape[0]
    s = x.shape[1]
    mod = _modulation(c, ada_w, ada_b)
    consts = dict(cover=_cover_matrix(s), eneg=_expand_neg(s))
    for l in range(depth):
        p = dict(
            norm_attn_g=norm_attn_g[l], norm_moe_g=norm_moe_g[l], norm_final_g=norm_final_g,
            w_in=w_in[l], w_out=w_out[l],
            diff_lam_q1=diff_lam_q1[l], diff_lam_k1=diff_lam_k1[l],
            diff_lam_q2=diff_lam_q2[l], diff_lam_k2=diff_lam_k2[l], diff_subln_g=diff_subln_g[l],
            swa_sinks=swa_sinks[l], fox_forget_b=fox_forget_b[l],
            nsa_cmp_pos=nsa_cmp_pos[l], nsa_cmp_w1=nsa_cmp_w1[l], nsa_cmp_b1=nsa_cmp_b1[l],
            nsa_cmp_w2=nsa_cmp_w2[l], router_w=router_w, router_b=router_b,
            exp_w_gate=exp_w_gate, exp_w_up=exp_w_up, exp_w_down=exp_w_down,
        )
        x = _layer(l, x, mod[l], p, consts, final=(l == depth - 1))
    return x
```

```python
import functools
import math

import numpy as np
import jax
import jax.numpy as jnp
from jax import lax
from jax.experimental import pallas as pl
from jax.experimental.pallas import tpu as pltpu

F32 = jnp.float32
BF16 = jnp.bfloat16

LANES = 128
HEAD_DIM = 64
HEADS = 4
GROUP_WIDTH = HEADS * HEAD_DIM
DIFF_DK = HEAD_DIM // 2
SWA_WINDOW = 128
SWA_KV_HEADS = 2
CMP_BLOCK = 32
CMP_STRIDE = 16
CMP_HIDDEN = 2 * HEAD_DIM
SEL_BLOCK = 64
SEL_TOPK = 16
NSA_WINDOW = 512
FORCE_SCORE = 1e4
NEG_INF = -1e30
MASK_BIG = 2.0 ** 100
N_EXPERTS = 64
EXPERTS_PER_GROUP = 8
EPS = 1e-6
LOG2E = math.log2(math.e)
VMEM_LIMIT = 56 * 1024 * 1024
WIDE_TILES = 4

IN_SPLITS = (
    GROUP_WIDTH, GROUP_WIDTH, GROUP_WIDTH,
    GROUP_WIDTH, SWA_KV_HEADS * HEAD_DIM, SWA_KV_HEADS * HEAD_DIM,
    GROUP_WIDTH, GROUP_WIDTH, GROUP_WIDTH, HEADS,
    GROUP_WIDTH, HEAD_DIM, HEAD_DIM, HEAD_DIM, HEAD_DIM,
    HEAD_DIM, HEAD_DIM, 3 * HEADS,
)

_NT = (((1,), (1,)), ((), ()))


def _alibi_slopes():
    n = 3 * HEADS
    m = 2.0 ** (-8.0 * np.arange(1, n + 1) / n)
    return m.reshape(HEADS, 3).T


SLOPES = _alibi_slopes()


def _dot(a, b):
    return jnp.dot(a, b, preferred_element_type=F32)


def _dot_nt(a, b):
    return lax.dot_general(a, b, _NT, preferred_element_type=F32)


def _split_bf16(a):
    hi = a.astype(BF16)
    lo = (a - hi.astype(F32)).astype(BF16)
    return hi, lo


def _dot_f32(a, b, nt=False):
    d = _dot_nt if nt else _dot
    ah, al = _split_bf16(a)
    bh, bl = _split_bf16(b)
    return d(ah, bh) + (d(ah, bl) + d(al, bh))


def _cparams(sem):
    return pltpu.CompilerParams(dimension_semantics=sem, vmem_limit_bytes=VMEM_LIMIT)


def _mod_kernel(c_ref, w_ref, b_ref, o_ref):
    c = c_ref[...]
    cond = c * jax.nn.sigmoid(c)
    o_ref[0] = _dot_f32(cond, w_ref[0]) + b_ref[0]


def _modulation(c, ada_w, ada_b):
    depth, d, n = ada_w.shape
    b = c.shape[0]
    rows = 8
    cp = jnp.zeros((rows, d), F32).at[:b].set(c)
    tn = 1536
    out = pl.pallas_call(
        _mod_kernel,
        grid=(depth, n // tn),
        in_specs=[
            pl.BlockSpec((rows, d), lambda l, j: (0, 0)),
            pl.BlockSpec((1, d, tn), lambda l, j: (l, 0, j)),
            pl.BlockSpec((1, 1, tn), lambda l, j: (l, 0, j)),
        ],
        out_specs=pl.BlockSpec((1, rows, tn), lambda l, j: (l, 0, j)),
        out_shape=jax.ShapeDtypeStruct((depth, rows, n), F32),
        compiler_params=_cparams(("arbitrary", "arbitrary")),
        name="adaln_mod",
    )(cp, ada_w, ada_b.reshape(depth, 1, n))
    return out[:, :b].reshape(depth, b, 6, d)


W_A, W_B, W_C, W_D, W_G, W_CMP = 768, 768, 768, 768, 128, 128
W_OFFS = np.cumsum((0, W_A, W_B, W_C, W_D, W_G, W_CMP))
W_TOTAL = int(W_OFFS[-1])


def _prep_w_in(w):
    offs = np.cumsum((0,) + IN_SPLITS)
    col = lambda i: w[:, int(offs[i]):int(offs[i + 1])]
    dup = lambda t: jnp.concatenate([t, t], axis=1)
    kb, vb = col(4), col(5)
    gates = jnp.zeros((w.shape[0], W_G), F32).at[:, :3 * HEADS].set(col(17))
    qs64 = LOG2E * HEAD_DIM ** -0.5
    parts = [
        col(0) * (LOG2E * DIFF_DK ** -0.5), col(1), col(2),
        col(3) * qs64,
        dup(kb[:, :HEAD_DIM]), dup(kb[:, HEAD_DIM:]),
        dup(vb[:, :HEAD_DIM]), dup(vb[:, HEAD_DIM:]),
        col(6) * qs64, col(7), col(8),
        col(10) * qs64, dup(col(13)), dup(col(14)), dup(col(15)), dup(col(16)),
    ] + [gates, col(11), col(12)]
    wp = jnp.concatenate(parts, axis=1).astype(BF16)
    wf = jnp.zeros((8, w.shape[0]), F32).at[:HEADS].set(col(9).T).astype(BF16)
    return wp, wf


def _lane_cumsum(y):
    n = y.shape[1]
    lane = lax.broadcasted_iota(jnp.int32, y.shape, 1)
    sh = 1
    while sh < n:
        y = y + jnp.where(lane >= sh, pltpu.roll(y, sh, 1), 0.0)
        sh *= 2
    return y


def _inproj_kernel(x_ref, mod_ref, g_ref, w_ref, wf_ref, bf_ref,
                   oa, ob, oc, od, og, ocmp, ofc, carry):
    i = pl.program_id(1)
    x = x_ref[0]
    ms = jnp.mean(x * x, axis=-1, keepdims=True)
    y = x * lax.rsqrt(ms + EPS) * g_ref[...]
    h = y * (1.0 + mod_ref[1:2, :]) + mod_ref[0:1, :]
    hb = h.astype(BF16)
    o = W_OFFS
    oa[0] = _dot(hb, w_ref[:, o[0]:o[1]]).astype(BF16)
    ob[0] = _dot(hb, w_ref[:, o[1]:o[2]]).astype(BF16)
    oc[0] = _dot(hb, w_ref[:, o[2]:o[3]]).astype(BF16)
    od[0] = _dot(hb, w_ref[:, o[3]:o[4]]).astype(BF16)
    og[0] = jax.nn.sigmoid(_dot(hb, w_ref[:, o[4]:o[5]])).astype(BF16)
    kvc = _dot(hb, w_ref[:, o[5]:o[6]])
    ocmp[0, 0] = kvc[:, :HEAD_DIM]
    ocmp[1, 0] = kvc[:, HEAD_DIM:]
    fl = _dot_nt(wf_ref[...], hb) + bf_ref[:, 0:1]
    ls = jnp.minimum(fl, 0.0) - jnp.log1p(jnp.exp(-jnp.abs(fl)))

    @pl.when(i == 0)
    def _():
        carry[...] = jnp.zeros_like(carry)

    cs = _lane_cumsum(ls) + carry[:, 0:1]
    ofc[0] = cs
    carry[...] = jnp.broadcast_to(cs[:, cs.shape[1] - 1:], carry.shape)


def _in_projection(x, mod_l, g, wp, wf, bf, tm):
    b, s, d = x.shape
    bf8 = jnp.zeros((8, LANES), F32).at[:HEADS, :].set(bf[:, None])
    outs = pl.pallas_call(
        _inproj_kernel,
        grid=(b, s // tm),
        in_specs=[
            pl.BlockSpec((1, tm, d), lambda bi, i: (bi, i, 0)),
            pl.BlockSpec((None, 6, d), lambda bi, i: (bi, 0, 0)),
            pl.BlockSpec((1, d), lambda bi, i: (0, 0)),
            pl.BlockSpec((d, W_TOTAL), lambda bi, i: (0, 0)),
            pl.BlockSpec((8, d), lambda bi, i: (0, 0)),
            pl.BlockSpec((8, LANES), lambda bi, i: (0, 0)),
        ],
        out_specs=[
            pl.BlockSpec((1, tm, W_A), lambda bi, i: (bi, i, 0)),
            pl.BlockSpec((1, tm, W_B), lambda bi, i: (bi, i, 0)),
            pl.BlockSpec((1, tm, W_C), lambda bi, i: (bi, i, 0)),
            pl.BlockSpec((1, tm, W_D), lambda bi, i: (bi, i, 0)),
            pl.BlockSpec((1, tm, W_G), lambda bi, i: (bi, i, 0)),
            pl.BlockSpec((2, 1, tm, HEAD_DIM), lambda bi, i: (0, bi, i, 0)),
            pl.BlockSpec((1, 8, tm), lambda bi, i: (bi, 0, i)),
        ],
        out_shape=[
            jax.ShapeDtypeStruct((b, s, W_A), BF16),
            jax.ShapeDtypeStruct((b, s, W_B), BF16),
            jax.ShapeDtypeStruct((b, s, W_C), BF16),
            jax.ShapeDtypeStruct((b, s, W_D), BF16),
            jax.ShapeDtypeStruct((b, s, W_G), BF16),
            jax.ShapeDtypeStruct((2, b, s, HEAD_DIM), F32),
            jax.ShapeDtypeStruct((b, 8, s), F32),
        ],
        scratch_shapes=[pltpu.VMEM((8, LANES), F32)],
        compiler_params=_cparams(("arbitrary", "arbitrary")),
        name="in_projection",
    )(x, mod_l, g.reshape(1, d), wp, wf, bf8)
    return outs


def _flash_kernel(kind, t, layer_idx, *refs):
    it = iter(refs)
    q_ref, k_ref, v_ref = next(it), next(it), next(it)
    fc_ref = next(it) if kind == "C" else None
    un_ref, use_ref, head_ref, en_ref = ((next(it), next(it), next(it), next(it)) if kind == "Dsel"
                                         else (None, None, None, None))
    lam_ref, sg_ref, bc_ref = (next(it), next(it), next(it)) if kind == "A" else (None, None, None)
    o_ref, m_scr, acc_scr = next(it), next(it), next(it)

    p = pl.program_id(1)
    i = pl.program_id(2)
    nstream = 4 if kind == "A" else 2
    width = LANES // nstream
    mixer = {"Dsel": 2}.get(kind)

    def head_of(s):
        return s // 2 if kind == "A" else s

    q2 = q_ref[0]
    lane = lax.broadcasted_iota(jnp.int32, (t, LANES), 1)
    zero = jnp.zeros_like(q2)
    qms = [jnp.where((lane >= s * width) & (lane < (s + 1) * width), q2, zero)
           for s in range(nstream)]
    if kind == "Dsel":
        qms = [jnp.concatenate([qm, un_ref[0]], axis=1) for qm in qms]
    if kind == "A":
        pick = [jnp.where((lane >= BIAS_TERMS * h) & (lane < BIAS_TERMS * (h + 1)), 1.0, 0.0).astype(q2.dtype)
                for h in range(2)]
        qms = [jnp.concatenate([qm, pick[head_of(s)]], axis=1) for s, qm in enumerate(qms)]
    qcat = jnp.concatenate(qms, axis=0)

    slopes = None
    if mixer is not None:
        sl = SLOPES[mixer] * LOG2E
        slopes = [jnp.where(p == 0, float(sl[head_of(s)]), float(sl[2 + head_of(s)]))
                  for s in range(nstream)]

    for s in range(nstream):
        m_scr[s] = jnp.full((t, LANES), NEG_INF, F32)
        acc_scr[s] = jnp.zeros((t, LANES), F32)

    def tile(first_key, nk, diag):
        start = pl.multiple_of(first_key, t)
        k2 = k_ref[0, pl.ds(start, nk), :]
        if kind == "Dsel":
            k2 = jnp.concatenate([k2, en_ref[pl.ds(start, nk), :]], axis=1)
        if kind == "A":
            k2 = jnp.concatenate([k2, bc_ref[0, pl.ds(start, nk), :]], axis=1)
        v2 = v_ref[0, pl.ds(start, nk), :]
        vlane = lax.broadcasted_iota(jnp.int32, (nk, LANES), 1)
        one = jnp.ones_like(v2)
        vhalf = [jnp.where(vlane < HEAD_DIM, v2, one), jnp.where(vlane < HEAD_DIM, one, v2)]
        col = lax.broadcasted_iota(jnp.int32, (1, nk), 1)
        rel = (start - i * t + col).astype(F32)
        if diag:
            mask = (lax.broadcasted_iota(jnp.int32, (t, t), 0)
                    >= lax.broadcasted_iota(jnp.int32, (t, t), 1))
        sc_all = _dot_nt(qcat, k2)
        for s in range(nstream):
            sc = sc_all[s * t:(s + 1) * t]
            if slopes is not None:
                sc = sc + slopes[s] * rel
            if kind == "C":
                sc = sc - LOG2E * fc_ref[0, pl.ds(2 * p + s, 1), pl.ds(start, nk)]
            if diag:
                sc = jnp.where(mask, sc, NEG_INF)
            m_prev = m_scr[s]
            m_next = jnp.maximum(m_prev, jnp.max(sc, axis=1, keepdims=True))
            alpha = jnp.exp2(m_prev - m_next)
            pexp = jnp.exp2((sc - jnp.tile(m_next, (1, nk // LANES))).astype(BF16))
            acc_scr[s] = alpha * acc_scr[s] + _dot(pexp, vhalf[head_of(s)])
            m_scr[s] = m_next

    tile(i * t, t, True)

    if kind == "Dsel":
        nq = pl.num_programs(2)
        base = (pl.program_id(0) * nq + i) * nq

        def body(j, carry):
            @pl.when(use_ref[base + j] != 0)
            def _():
                tile(j * t, t, False)
            return carry

        short = (i > 0) & (head_ref[pl.program_id(0) * nq + i] != 0)

        @pl.when(short)
        def _():
            tile(0, LANES, False)

        lax.fori_loop(jnp.where(short, 1, 0), i, body, 0)
    else:
        wide = WIDE_TILES * t

        def body(j, carry):
            tile(j * wide, wide, False)
            return carry

        lax.fori_loop(0, i // WIDE_TILES, body, 0)

        done = i // WIDE_TILES * WIDE_TILES
        width = WIDE_TILES // 2
        while width >= 1:
            take = ((i - done) // width) * width

            @pl.when(take > 0)
            def _(done=done, width=width):
                tile(done * t, width * t, False)

            done = done + take
            width //= 2

    outs = [acc_scr[s] / pltpu.roll(acc_scr[s], HEAD_DIM, 1) for s in range(nstream)]
    if kind == "A":
        lam_init = 0.8 - 0.6 * math.exp(-0.3 * layer_idx)
        t1 = jnp.sum(lam_ref[0:1, :] * lam_ref[1:2, :], axis=1, keepdims=True)
        t2 = jnp.sum(lam_ref[2:3, :] * lam_ref[3:4, :], axis=1, keepdims=True)
        lam = jnp.exp(t1) - jnp.exp(t2) + lam_init
        d0 = outs[0] - lam * outs[1]
        d1 = outs[2] - lam * outs[3]
        lo_half = lane < HEAD_DIM
        o = jnp.where(lo_half, d0, d1)
        sq = o * o
        ss_lo = jnp.sum(jnp.where(lo_half, sq, 0.0), axis=1, keepdims=True)
        ss_hi = jnp.sum(jnp.where(lo_half, 0.0, sq), axis=1, keepdims=True)
        ms = jnp.where(lo_half, ss_lo, ss_hi) * (1.0 / HEAD_DIM)
        o = o * lax.rsqrt(ms + EPS) * sg_ref[...] * (1.0 - lam_init)
    else:
        o = jnp.where(lane < HEAD_DIM, outs[0], outs[1])
    o_ref[0] = o.astype(o_ref.dtype)


BIAS_TERMS = 3


def _alibi_key_columns(s):
    cols = np.zeros((2, s, LANES), np.float32)
    pos = np.arange(s, dtype=np.float64)
    for pair in range(2):
        for h in range(2):
            rest = SLOPES[0][2 * pair + h] * LOG2E * pos
            for n in range(BIAS_TERMS):
                term = rest.astype(jnp.bfloat16).astype(np.float64)
                cols[pair, :, BIAS_TERMS * h + n] = term
                rest = rest - term
    return jnp.asarray(cols, BF16)


def _flash(kind, t, layer_idx, q_arr, qoff, k_arr, koff, v_arr, voff, extras):
    b, s, _ = q_arr.shape
    nstream = 4 if kind == "A" else 2
    shared_kv = kind == "Dsel"
    kv_idx = (lambda off: (lambda bi, p, i: (bi, 0, off))) if shared_kv else \
             (lambda off: (lambda bi, p, i: (bi, 0, off + p)))
    in_specs = [
        pl.BlockSpec((1, t, LANES), lambda bi, p, i: (bi, i, qoff + p)),
        pl.BlockSpec((1, s, LANES), kv_idx(koff)),
        pl.BlockSpec((1, s, LANES), kv_idx(voff)),
    ]
    args = [q_arr, k_arr, v_arr]
    if kind == "C":
        fcum, = extras
        in_specs.append(pl.BlockSpec((1, 8, s), lambda bi, p, i: (bi, 0, 0)))
        args.append(fcum)
    elif kind == "Dsel":
        unsel, eneg = extras
        nsel = unsel.shape[-1]
        nq, per = s // t, t // SEL_BLOCK
        picked = unsel.reshape(b, nq, t, nq, per) == 0
        used = picked.any(axis=(2, 4)).astype(jnp.int32).reshape(-1)
        head_only = jnp.logical_not(picked[:, :, :, 0, LANES // SEL_BLOCK:].any(axis=(2, 3)))
        head_only = head_only.astype(jnp.int32).reshape(-1)
        in_specs.append(pl.BlockSpec((1, t, nsel), lambda bi, p, i: (bi, i, 0)))
        in_specs.append(pl.BlockSpec(memory_space=pltpu.SMEM))
        in_specs.append(pl.BlockSpec(memory_space=pltpu.SMEM))
        in_specs.append(pl.BlockSpec((s, nsel), lambda bi, p, i: (0, 0)))
        args += [unsel, used, head_only, eneg]
    elif kind == "A":
        lam8, sg = extras
        in_specs.append(pl.BlockSpec((8, LANES), lambda bi, p, i: (0, 0)))
        in_specs.append(pl.BlockSpec((1, LANES), lambda bi, p, i: (0, 0)))
        in_specs.append(pl.BlockSpec((1, s, LANES), lambda bi, p, i: (p, 0, 0)))
        args += [lam8, sg, _alibi_key_columns(s)]
    return pl.pallas_call(
        functools.partial(_flash_kernel, kind, t, layer_idx),
        grid=(b, 2, s // t),
        in_specs=in_specs,
        out_specs=pl.BlockSpec((1, t, LANES), lambda bi, p, i: (bi, i, p)),
        out_shape=jax.ShapeDtypeStruct((b, s, 2 * LANES), BF16),
        scratch_shapes=[pltpu.VMEM((nstream, t, LANES), F32)] * 2,
        compiler_params=_cparams(("arbitrary", "arbitrary", "arbitrary")),
        name="flash_" + kind,
    )(*args)


def _window_kernel(kind, t, wpad, *refs):
    it = iter(refs)
    q_ref, k_ref, v_ref = next(it), next(it), next(it)
    sink_ref = next(it) if kind == "B" else None
    o_ref = next(it)
    i = pl.program_id(1)
    window = SWA_WINDOW if kind == "B" else NSA_WINDOW
    sl = SLOPES[1 if kind == "B" else 2] * LOG2E
    nk = wpad + t
    start = pl.multiple_of(jnp.maximum(i * t - wpad, 0), LANES)
    lane = lax.broadcasted_iota(jnp.int32, (t, LANES), 1)
    vlane = lax.broadcasted_iota(jnp.int32, (nk, LANES), 1)
    dist = ((i * t - start) + lax.broadcasted_iota(jnp.int32, (t, nk), 0)
            - lax.broadcasted_iota(jnp.int32, (t, nk), 1))
    valid = (dist >= 0) & (dist < window)
    rel = (start - i * t + lax.broadcasted_iota(jnp.int32, (1, nk), 1)).astype(F32)
    qrow = lax.broadcasted_iota(jnp.int32, (t, 1), 0).astype(F32)

    def masked_q(pair, g):
        q2 = q_ref[0, :, pair * LANES:(pair + 1) * LANES]
        return jnp.where((lane >= g * HEAD_DIM) & (lane < (g + 1) * HEAD_DIM), q2, jnp.zeros_like(q2))

    def kv_tiles(pair):
        cols = slice(pair * LANES, (pair + 1) * LANES) if kind == "B" else slice(0, LANES)
        k2 = k_ref[0, pl.ds(start, nk), cols]
        v2 = v_ref[0, pl.ds(start, nk), cols]
        one = jnp.ones_like(v2)
        return k2, [jnp.where(vlane < HEAD_DIM, v2, one), jnp.where(vlane < HEAD_DIM, one, v2)]

    groups = [[(0, 0), (0, 1)], [(1, 0), (1, 1)]] if kind == "B" else [[(0, 0), (0, 1), (1, 0), (1, 1)]]
    outs = {}
    for group in groups:
        k2, vhalf = kv_tiles(group[0][0])
        sc_all = _dot_nt(jnp.concatenate([masked_q(pair, g) for pair, g in group], axis=0), k2)
        for n, (pair, g) in enumerate(group):
            h = 2 * pair + g
            slope = float(sl[h])
            sc = jnp.where(valid, sc_all[n * t:(n + 1) * t] + slope * rel, NEG_INF)
            m = jnp.max(sc, axis=1, keepdims=True)
            if kind == "B":
                sink = LOG2E * sink_ref[h] + slope * qrow
                m = jnp.maximum(m, sink)
            pv = _dot(jnp.exp2((sc - m).astype(BF16)), vhalf[g])
            l = pltpu.roll(pv, HEAD_DIM, 1)
            if kind == "B":
                l = l + jnp.exp2(sink - m)
            outs[h] = pv / l
    for pair in range(2):
        o_ref[0, :, pair * LANES:(pair + 1) * LANES] = jnp.where(
            lane < HEAD_DIM, outs[2 * pair], outs[2 * pair + 1]).astype(o_ref.dtype)


def _window(kind, t, wpad, q_arr, k_arr, koff, v_arr, voff, extras):
    b, s, _ = q_arr.shape
    kw = 2 * LANES if kind == "B" else LANES
    in_specs = [
        pl.BlockSpec((1, t, 2 * LANES), lambda bi, i: (bi, i, 0)),
        pl.BlockSpec((1, s, kw), lambda bi, i: (bi, 0, koff)),
        pl.BlockSpec((1, s, kw), lambda bi, i: (bi, 0, voff)),
    ]
    args = [q_arr, k_arr, v_arr]
    if kind == "B":
        in_specs.append(pl.BlockSpec(memory_space=pltpu.SMEM))
        args.append(extras[0])
    return pl.pallas_call(
        functools.partial(_window_kernel, kind, t, wpad),
        grid=(b, s // t),
        in_specs=in_specs,
        out_specs=pl.BlockSpec((1, t, 2 * LANES), lambda bi, i: (bi, i, 0)),
        out_shape=jax.ShapeDtypeStruct((b, s, 2 * LANES), BF16),
        compiler_params=_cparams(("arbitrary", "arbitrary")),
        name="window_" + kind,
    )(*args)


def _gelu_tanh(x):
    return 0.5 * x * (1.0 + jnp.tanh(math.sqrt(2.0 / math.pi) * (x + 0.044715 * x * x * x)))


def _compress_kernel(t_ref, w1_ref, pos_ref, b1_ref, w2_ref, o_ref):
    tr = t_ref[0, 0]
    half = tr.shape[1]
    n = tr.shape[0]
    w1 = w1_ref[0]
    u = _dot_f32(tr, w1[:half])
    v = _dot_f32(tr, w1[half:])
    cpos = _dot_f32(pos_ref[0], w1)[0:1] + b1_ref[0]
    hid = u + pltpu.roll(v, n - 1, 0) + cpos
    o_ref[0, 0] = _dot_f32(_gelu_tanh(hid), w2_ref[0])


def _compress(kv_r, w1, pos8, b1, w2d):
    two, b, n, dd = kv_r.shape
    return pl.pallas_call(
        _compress_kernel,
        grid=(two, b),
        in_specs=[
            pl.BlockSpec((1, 1, n, dd), lambda c, bi: (c, bi, 0, 0)),
            pl.BlockSpec((1, 2 * dd, CMP_HIDDEN), lambda c, bi: (c, 0, 0)),
            pl.BlockSpec((1, 8, 2 * dd), lambda c, bi: (c, 0, 0)),
            pl.BlockSpec((1, 1, CMP_HIDDEN), lambda c, bi: (c, 0, 0)),
            pl.BlockSpec((1, CMP_HIDDEN, LANES), lambda c, bi: (c, 0, 0)),
        ],
        out_specs=pl.BlockSpec((1, 1, n, LANES), lambda c, bi: (c, bi, 0, 0)),
        out_shape=jax.ShapeDtypeStruct((two, b, n, LANES), F32),
        compiler_params=_cparams(("arbitrary", "arbitrary")),
        name="nsa_compress",
    )(kv_r, w1, pos8, b1, w2d)


def _cmp_kernel(t, first_tile, ncp_total, q_ref, kc_ref, vc_ref, cov_ref, o_ref, un_ref):
    i = first_tile + pl.program_id(1)
    ncp = kc_ref.shape[2]
    nsel = cov_ref.shape[0]
    kc = kc_ref[0, 0]
    vc = vc_ref[0, 0]
    kch, kcl = _split_bf16(kc)
    vcb = vc.astype(BF16)
    lane = lax.broadcasted_iota(jnp.int32, (t, LANES), 1)
    tq = i * t + lax.broadcasted_iota(jnp.int32, (t, ncp), 0)
    nidx = lax.broadcasted_iota(jnp.int32, (t, ncp), 1)
    cmp_end = nidx * CMP_STRIDE + (CMP_BLOCK - 1)
    valid = (tq >= cmp_end) & (nidx < ncp_total - 1)
    end_row = cmp_end[0:1, :].astype(F32)
    psum = jnp.zeros((t, ncp), F32)
    outs = []
    for pair in range(2):
        q2 = q_ref[0, :, pair * LANES:(pair + 1) * LANES]
        for g in range(2):
            h = 2 * pair + g
            qm = jnp.where((lane >= g * HEAD_DIM) & (lane < (g + 1) * HEAD_DIM), q2, jnp.zeros_like(q2))
            sc = _dot_nt(qm, kch) + _dot_nt(qm, kcl) + float(SLOPES[2][h] * LOG2E) * end_row
            sc = jnp.where(valid, sc, NEG_INF)
            m = jnp.max(sc, axis=1, keepdims=True)
            pe = jnp.exp2(sc - m)
            l = jnp.sum(pe, axis=1, keepdims=True)
            pc = pe * jnp.where(m > 0.5 * NEG_INF, 1.0 / l, 0.0)
            psum = psum + pc
            outs.append(_dot(pc.astype(BF16), vcb))
    lo_half = lane < HEAD_DIM
    o_ref[0, :, 0:LANES] = jnp.where(lo_half, outs[0], outs[1]).astype(o_ref.dtype)
    o_ref[0, :, LANES:2 * LANES] = jnp.where(lo_half, outs[2], outs[3]).astype(o_ref.dtype)
    ph, plo = _split_bf16(psum)
    cov = cov_ref[...]
    imp = _dot_nt(cov, ph) + _dot_nt(cov, plo)
    blk = lax.broadcasted_iota(jnp.int32, (nsel, t), 0)
    cur = (i * t + lax.broadcasted_iota(jnp.int32, (nsel, t), 1)) // SEL_BLOCK
    forced = (blk == 0) | (blk == cur) | (blk == cur - 1)
    score = jnp.where(forced, FORCE_SCORE, jnp.where(blk <= cur, imp, -1.0))
    unsel = jnp.ones((nsel, t), F32)
    for _ in range(min(SEL_TOPK, nsel)):
        mx = jnp.max(score, axis=0, keepdims=True)
        idx = jnp.min(jnp.where(score == mx, blk, nsel), axis=0, keepdims=True)
        hit = blk == idx
        unsel = jnp.where(hit, 0.0, unsel)
        score = jnp.where(hit, -3.0, score)
    un_ref[0] = unsel.T.astype(un_ref.dtype)


CMP_RANGES = 4


def _cmp_attention(t, qd, kvc, cover_t):
    b, s, _ = qd.shape
    ncp_total = kvc.shape[2]
    nsel = cover_t.shape[0]
    steps = s // t // CMP_RANGES
    outs = []
    for r in range(CMP_RANGES):
        ncp = (r + 1) * ncp_total // CMP_RANGES
        first = r * steps
        outs.append(pl.pallas_call(
            functools.partial(_cmp_kernel, t, first, ncp_total),
            grid=(b, steps),
            in_specs=[
                pl.BlockSpec((1, t, 2 * LANES), lambda bi, i, first=first: (bi, first + i, 0)),
                pl.BlockSpec((1, 1, ncp, LANES), lambda bi, i: (0, bi, 0, 0)),
                pl.BlockSpec((1, 1, ncp, LANES), lambda bi, i: (1, bi, 0, 0)),
                pl.BlockSpec((nsel, ncp), lambda bi, i: (0, 0)),
            ],
            out_specs=[
                pl.BlockSpec((1, t, 2 * LANES), lambda bi, i: (bi, i, 0)),
                pl.BlockSpec((1, t, nsel), lambda bi, i: (bi, i, 0)),
            ],
            out_shape=[
                jax.ShapeDtypeStruct((b, steps * t, 2 * LANES), BF16),
                jax.ShapeDtypeStruct((b, steps * t, nsel), BF16),
            ],
            compiler_params=_cparams(("arbitrary", "arbitrary")),
            name="nsa_cmp_select",
        )(qd, kvc, kvc, cover_t))
    return (jnp.concatenate([o for o, _ in outs], axis=1),
            jnp.concatenate([u for _, u in outs], axis=1))


def _cover_matrix(s):
    ncp, nsel = s // CMP_STRIDE, s // SEL_BLOCK
    cs = np.arange(ncp)[:, None] * CMP_STRIDE
    ss = np.arange(nsel)[None, :] * SEL_BLOCK
    cov = np.clip(np.minimum(cs + CMP_BLOCK, ss + SEL_BLOCK) - np.maximum(cs, ss), 0, None) / CMP_BLOCK
    cov[ncp - 1] = 0.0
    return jnp.asarray(cov.T, BF16)


def _expand_neg(s):
    nsel = s // SEL_BLOCK
    e = (np.arange(s)[:, None] // SEL_BLOCK) == np.arange(nsel)[None, :]
    return jnp.asarray(np.where(e, -MASK_BIG, 0.0), BF16)


def _outproj_kernel(ya, yb, yc, ocmp, osel, owin, gt, gx_ref, x_ref, mod_ref, wo_ref, g_ref,
                    rwt_ref, rb_ref, xo_ref, h_ref, ri_ref, rg_ref, cnt_ref, carry):
    first = (pl.program_id(0) == 0) & (pl.program_id(1) == 0)
    gw = GROUP_WIDTH
    g = _dot(gt[0], gx_ref[...])
    yd = (g[:, 0:gw] * ocmp[0].astype(F32) + g[:, gw:2 * gw] * osel[0].astype(F32)
          + g[:, 2 * gw:3 * gw] * owin[0].astype(F32))
    ycat = jnp.concatenate([ya[0], yb[0], yc[0], yd.astype(BF16)], axis=1)
    y = _dot(ycat, wo_ref[...])
    x = x_ref[0] + mod_ref[2:3, :] * y
    xo_ref[0] = x
    ms = jnp.mean(x * x, axis=-1, keepdims=True)
    h = x * lax.rsqrt(ms + EPS) * g_ref[...]
    h = h * (1.0 + mod_ref[4:5, :]) + mod_ref[3:4, :]
    h_ref[0] = h
    tm = h.shape[0]
    ng = LANES // EXPERTS_PER_GROUP
    big = 4 * LANES
    logits = _dot_f32(rwt_ref[...], h, nt=True)
    eidx = lax.broadcasted_iota(jnp.int32, (LANES, tm), 0)
    aff = jax.nn.sigmoid(logits)
    sel = jnp.where(eidx < N_EXPERTS, aff + rb_ref[:, 0:1], NEG_INF)
    sel3 = sel.reshape(ng, EXPERTS_PER_GROUP, tm)
    e3 = eidx.reshape(ng, EXPERTS_PER_GROUP, tm)
    g1 = jnp.max(sel3, axis=1, keepdims=True)
    i1 = jnp.min(jnp.where(sel3 == g1, e3, big), axis=1, keepdims=True)
    sel_b = jnp.where(e3 == i1, NEG_INF, sel3)
    g2 = jnp.max(sel_b, axis=1, keepdims=True)
    i2 = jnp.min(jnp.where(sel_b == g2, e3, big), axis=1, keepdims=True)
    gs = g1 + g2
    gidx = lax.broadcasted_iota(jnp.int32, (ng, 1, tm), 0)
    gm = jnp.max(gs, axis=0, keepdims=True)
    best = gidx == jnp.min(jnp.where(gs == gm, gidx, big), axis=0, keepdims=True)
    e1 = jnp.min(jnp.where(best, i1, big), axis=0)
    e2 = jnp.min(jnp.where(best, i2, big), axis=0)
    oh1 = eidx == e1
    oh2 = eidx == e2
    a1 = jnp.sum(jnp.where(oh1, aff, 0.0), axis=0, keepdims=True)
    a2 = jnp.sum(jnp.where(oh2, aff, 0.0), axis=0, keepdims=True)
    inv = 1.0 / (a1 + a2)
    @pl.when(first)
    def _():
        carry[...] = jnp.zeros_like(carry)

    ohs = jnp.where(oh1 | oh2, 1.0, 0.0)
    rr = lax.broadcasted_iota(jnp.int32, (tm, tm), 0)
    cc = lax.broadcasted_iota(jnp.int32, (tm, tm), 1)
    earlier = jnp.where(rr < cc, 1.0, 0.0).astype(BF16)
    before = _dot(ohs.astype(BF16), earlier) + carry[:, 0:1]
    r1 = jnp.sum(jnp.where(oh1, before, 0.0), axis=0, keepdims=True)
    r2 = jnp.sum(jnp.where(oh2, before, 0.0), axis=0, keepdims=True)
    total = carry[:, 0:1] + jnp.sum(ohs, axis=1, keepdims=True)
    carry[...] = jnp.broadcast_to(total, carry.shape)
    cnt_ref[...] = jnp.broadcast_to(total, cnt_ref.shape).astype(jnp.int32)
    row = lax.broadcasted_iota(jnp.int32, (8, tm), 0)
    ri_ref[0] = jnp.where(row == 0, e1, jnp.where(row == 1, e2, jnp.where(
        row == 2, r1.astype(jnp.int32), jnp.where(row == 3, r2.astype(jnp.int32), 0))))
    rg_ref[0] = jnp.where(row == 0, a1 * inv, jnp.where(row == 1, a2 * inv, 0.0))


def _gate_expander():
    e = np.zeros((W_G, 3 * GROUP_WIDTH), np.float32)
    for h in range(HEADS):
        for r in range(3):
            e[h * 3 + r, r * GROUP_WIDTH + h * HEAD_DIM:r * GROUP_WIDTH + (h + 1) * HEAD_DIM] = 1.0
    return jnp.asarray(e, BF16)


def _out_projection(ya, yb, yc, ocmp, osel, owin, gates, x, mod_l, wo, g, rwt, rb, tm):
    b, s, d = x.shape
    gw = GROUP_WIDTH
    tok = lambda w: pl.BlockSpec((1, tm, w), lambda bi, i: (bi, i, 0))
    full = lambda shp: pl.BlockSpec(shp, lambda bi, i: (0,) * len(shp))
    rowblk = pl.BlockSpec((1, 8, tm), lambda bi, i: (bi, 0, i))
    return pl.pallas_call(
        _outproj_kernel,
        grid=(b, s // tm),
        in_specs=[tok(gw)] * 6 + [tok(W_G), full((W_G, 3 * gw)), tok(d),
                                  pl.BlockSpec((None, 6, d), lambda bi, i: (bi, 0, 0)),
                                  full((d, d)), full((1, d)), full((LANES, d)), full((LANES, 1))],
        out_specs=[tok(d), tok(d), rowblk, rowblk, full((LANES, LANES))],
        out_shape=[
            jax.ShapeDtypeStruct((b, s, d), F32),
            jax.ShapeDtypeStruct((b, s, d), F32),
            jax.ShapeDtypeStruct((b, 8, s), jnp.int32),
            jax.ShapeDtypeStruct((b, 8, s), F32),
            jax.ShapeDtypeStruct((LANES, LANES), jnp.int32),
        ],
        scratch_shapes=[pltpu.VMEM((LANES, LANES), F32)],
        compiler_params=_cparams(("arbitrary", "arbitrary")),
        name="out_projection_router",
    )(ya, yb, yc, ocmp, osel, owin, gates, _gate_expander(), x, mod_l, wo, g.reshape(1, d), rwt, rb)


def _dispatch_kernel(p0_ref, p1_ref, h_ref, xs_ref, sem):
    tm = h_ref.shape[0]

    def issue(r, c):
        for pos_ref in (p0_ref, p1_ref):
            pltpu.make_async_copy(h_ref.at[pl.ds(r, 1), :],
                                  xs_ref.at[pl.ds(pos_ref[r], 1), :], sem).start()
        return c

    lax.fori_loop(0, tm, issue, 0, unroll=8)
    for _ in range(2):
        pltpu.make_async_copy(h_ref, xs_ref.at[pl.ds(0, tm), :], sem).wait()


def _dispatch(h, pos0, pos1, tm):
    t, d = h.shape
    return pl.pallas_call(
        _dispatch_kernel,
        grid=(t // tm,),
        in_specs=[
            pl.BlockSpec((tm,), lambda i: (i,), memory_space=pltpu.SMEM),
            pl.BlockSpec((tm,), lambda i: (i,), memory_space=pltpu.SMEM),
            pl.BlockSpec((tm, d), lambda i: (i, 0)),
        ],
        out_specs=pl.BlockSpec(memory_space=pl.ANY),
        out_shape=jax.ShapeDtypeStruct((2 * t, d), F32),
        scratch_shapes=[pltpu.SemaphoreType.DMA(())],
        compiler_params=_cparams(("arbitrary",)),
        name="moe_dispatch",
    )(pos0, pos1, h)


def _expert_kernel(layer, vb_ref, ve_ref, vlo_ref, vhi_ref, vord_ref, vnext_ref, nv_ref,
                   xs_ref, wg_hbm, wu_hbm, wd_hbm, ys_ref, wgf, wuf, wdf, wgb, wub, wdb, sem):
    v = pl.program_id(0)
    rows = xs_ref.shape[0]
    prev = jnp.maximum(v - 1, 0)
    new_expert = (v == 0) | (ve_ref[v] != ve_ref[prev])
    new_block = (v == 0) | (vb_ref[v] != vb_ref[prev])

    def weight_copies(e, slot):
        return [pltpu.make_async_copy(src.at[layer, e], dst.at[slot], sem.at[slot, n])
                for n, (src, dst) in enumerate(((wg_hbm, wgf), (wu_hbm, wuf), (wd_hbm, wdf)))]

    @pl.when(v < nv_ref[0])
    def _():
        @pl.when(new_expert)
        def _():
            slot = vord_ref[v] % 2

            @pl.when(v == 0)
            def _():
                for c in weight_copies(ve_ref[v], slot):
                    c.start()

            @pl.when(vnext_ref[v] >= 0)
            def _():
                for c in weight_copies(vnext_ref[v], 1 - slot):
                    c.start()

            for c in weight_copies(ve_ref[v], slot):
                c.wait()
            wgb[...] = wgf[slot].astype(BF16)
            wub[...] = wuf[slot].astype(BF16)
            wdb[...] = wdf[slot].astype(BF16)

        x = xs_ref[...].astype(BF16)
        hg = _dot(x, wgb[...])
        hu = _dot(x, wub[...])
        hm = (hg * jax.nn.sigmoid(hg) * hu).astype(BF16)
        y = _dot(hm, wdb[...])
        r = vb_ref[v] * rows + lax.broadcasted_iota(jnp.int32, (rows, 1), 0)
        mine = (r >= vlo_ref[v]) & (r < vhi_ref[v])

        @pl.when(new_block)
        def _():
            ys_ref[...] = jnp.where(mine, y, 0.0)

        @pl.when(jnp.logical_not(new_block))
        def _():
            ys_ref[...] = jnp.where(mine, y, ys_ref[...])


def _experts(xs, meta, layer, wg, wu, wd, rows, nvmax):
    n, d = xs.shape
    de = wg.shape[3]
    blk = lambda v, vb, ve, lo, hi, vo, vn, nv: (vb[v], 0)
    grid_spec = pltpu.PrefetchScalarGridSpec(
        num_scalar_prefetch=7,
        grid=(nvmax,),
        in_specs=[
            pl.BlockSpec((rows, d), blk),
            pl.BlockSpec(memory_space=pl.ANY),
            pl.BlockSpec(memory_space=pl.ANY),
            pl.BlockSpec(memory_space=pl.ANY),
        ],
        out_specs=pl.BlockSpec((rows, d), blk),
        scratch_shapes=[pltpu.VMEM((2, d, de), F32), pltpu.VMEM((2, d, de), F32), pltpu.VMEM((2, de, d), F32),
                        pltpu.VMEM((d, de), BF16), pltpu.VMEM((d, de), BF16), pltpu.VMEM((de, d), BF16),
                        pltpu.SemaphoreType.DMA((2, 3))],
    )
    return pl.pallas_call(
        functools.partial(_expert_kernel, layer),
        grid_spec=grid_spec,
        out_shape=jax.ShapeDtypeStruct((n, d), F32),
        compiler_params=_cparams(("arbitrary",)),
        name="moe_experts",
    )(*meta, xs, wg, wu, wd)


def _combine_kernel(final, p0_ref, p1_ref, x_ref, rg_ref, mod_ref, g_ref, ys_ref, o_ref, buf, sem):
    tm = x_ref.shape[0]

    def issue(r, c):
        for k, pos_ref in enumerate((p0_ref, p1_ref)):
            pltpu.make_async_copy(ys_ref.at[pl.ds(pos_ref[r], 1), :],
                                  buf.at[k, pl.ds(r, 1), :], sem).start()
        return c

    lax.fori_loop(0, tm, issue, 0, unroll=8)
    for k in range(2):
        pltpu.make_async_copy(ys_ref.at[pl.ds(0, tm), :], buf.at[k], sem).wait()
    rg = rg_ref[...]
    y = rg[:, 0:1] * buf[0] + rg[:, 1:2] * buf[1]
    x = x_ref[...] + mod_ref[5:6, :] * y
    if final:
        ms = jnp.mean(x * x, axis=-1, keepdims=True)
        x = x * lax.rsqrt(ms + EPS) * g_ref[...]
    o_ref[...] = x


def _combine(final, pos0, pos1, x, rg, mod_l, g, ys, tm):
    t, d = x.shape
    per_b = t // mod_l.shape[0] // tm
    return pl.pallas_call(
        functools.partial(_combine_kernel, final),
        grid=(t // tm,),
        in_specs=[
            pl.BlockSpec((tm,), lambda i: (i,), memory_space=pltpu.SMEM),
            pl.BlockSpec((tm,), lambda i: (i,), memory_space=pltpu.SMEM),
            pl.BlockSpec((tm, d), lambda i: (i, 0)),
            pl.BlockSpec((tm, 8), lambda i: (i, 0)),
            pl.BlockSpec((None, 6, d), lambda i: (i // per_b, 0, 0)),
            pl.BlockSpec((1, d), lambda i: (0, 0)),
            pl.BlockSpec(memory_space=pl.ANY),
        ],
        out_specs=pl.BlockSpec((tm, d), lambda i: (i, 0)),
        out_shape=jax.ShapeDtypeStruct((t, d), F32),
        scratch_shapes=[pltpu.VMEM((2, tm, d), F32), pltpu.SemaphoreType.DMA(())],
        compiler_params=_cparams(("arbitrary",)),
        name="moe_combine",
    )(pos0, pos1, x, rg, mod_l, g.reshape(1, d), ys)


def _visit_plan(counts, rows, nblocks):
    ne = counts.shape[0]
    nvmax = nblocks + ne - 1
    ends = jnp.cumsum(counts)
    offs = ends - counts
    b_lo = offs // rows
    b_hi = jnp.maximum(ends - 1, 0) // rows
    nvis = jnp.where(counts > 0, b_hi - b_lo + 1, 0)
    vend = jnp.cumsum(nvis)
    vstart = vend - nvis
    nv = vend[-1]
    v = jnp.minimum(jnp.arange(nvmax), nv - 1)
    e = jnp.sum((v[:, None] >= vend[None, :]).astype(jnp.int32), axis=1)
    onehot = e[:, None] == jnp.arange(ne)[None, :]
    pick = lambda a: jnp.sum(jnp.where(onehot, a[None, :], 0), axis=1)
    blk = pick(b_lo) + v - pick(vstart)
    lo = jnp.maximum(pick(offs), blk * rows)
    hi = jnp.minimum(pick(ends), (blk + 1) * rows)
    order = jnp.cumsum((counts > 0).astype(jnp.int32)) - 1
    ids = jnp.where(counts > 0, jnp.arange(ne), ne)
    later = jnp.concatenate([jnp.flip(lax.cummin(jnp.flip(ids)))[1:], jnp.full((1,), ne, ids.dtype)])
    nxt = jnp.where(later < ne, later, -1)
    i32 = lambda a: a.astype(jnp.int32)
    return (i32(blk), i32(e), i32(lo), i32(hi), i32(pick(order)), i32(pick(nxt)),
            i32(nv).reshape(1)), offs, nvmax


def _tiles(s):
    tm = min(512, s)
    return dict(tm=tm, t_full=min(512, s), t_win=min(256, s), t_cmp=min(512, s),
                tm_dispatch=min(2048, s), tm_combine=min(1024, s), rows=256)


def _layer(l, x, mod_l, p, consts, final):
    b, s, d = x.shape
    tl = _tiles(s)
    wp, wf = _prep_w_in(p["w_in"])
    qa, qb, qc, qd, gates, kvcmp, fcum = _in_projection(
        x, mod_l, p["norm_attn_g"], wp, wf, p["fox_forget_b"], tl["tm"])
    lam8 = jnp.zeros((8, LANES), F32).at[:4, :DIFF_DK].set(
        jnp.stack([p["diff_lam_q1"], p["diff_lam_k1"], p["diff_lam_q2"], p["diff_lam_k2"]]))
    sg = jnp.tile(p["diff_subln_g"], 2).reshape(1, LANES)
    ya = _flash("A", tl["t_full"], l, qa, 0, qa, 2, qa, 4, (lam8, sg))
    yb = _window("B", tl["t_win"], SWA_WINDOW, qb, qb, 1, qb, 2, (p["swa_sinks"],))
    yc = _flash("C", tl["t_full"], l, qc, 0, qc, 2, qc, 4, (fcum,))
    n16 = s // CMP_STRIDE
    kv_r = kvcmp.reshape(2, b, n16, CMP_STRIDE * HEAD_DIM)
    pos8 = jnp.zeros((2, 8, CMP_BLOCK * HEAD_DIM), F32).at[:, 0].set(
        p["nsa_cmp_pos"].reshape(2, CMP_BLOCK * HEAD_DIM))
    w2d = jnp.concatenate([p["nsa_cmp_w2"], p["nsa_cmp_w2"]], axis=-1)
    kvc = _compress(kv_r, p["nsa_cmp_w1"], pos8, p["nsa_cmp_b1"].reshape(2, 1, CMP_HIDDEN), w2d)
    ocmp, unsel = _cmp_attention(tl["t_cmp"], qd, kvc, consts["cover"])
    osel = _flash("Dsel", tl["t_full"], l, qd, 0, qd, 2, qd, 3, (unsel, consts["eneg"]))
    owin = _window("Dwin", tl["t_win"], NSA_WINDOW, qd, qd, 4, qd, 5, ())
    rwt = jnp.zeros((LANES, d), F32).at[:N_EXPERTS].set(p["router_w"].T)
    rb = jnp.zeros((LANES, 1), F32).at[:N_EXPERTS, 0].set(p["router_b"])
    xm, h2, ri, rg, cnt = _out_projection(
        ya, yb, yc, ocmp, osel, owin, gates, x, mod_l, p["w_out"].astype(BF16),
        p["norm_moe_g"], rwt, rb, tl["tm"])
    t = b * s
    counts = cnt[:N_EXPERTS, 0]
    rows = tl["rows"]
    meta, offs, nvmax = _visit_plan(counts, rows, 2 * t // rows)
    onehot = ri[:, 0:2, :, None] == jnp.arange(N_EXPERTS)[None, None, None, :]
    pos = (jnp.sum(jnp.where(onehot, offs[None, None, None, :], 0), axis=-1) + ri[:, 2:4]).astype(jnp.int32)
    pos0, pos1 = pos[:, 0].reshape(t), pos[:, 1].reshape(t)
    rgt = rg.transpose(0, 2, 1).reshape(t, 8)
    xs = _dispatch(h2.reshape(t, d), pos0, pos1, tl["tm_dispatch"])
    ys = _experts(xs, meta, l, p["exp_w_gate"], p["exp_w_up"], p["exp_w_down"], rows, nvmax)
    out = _combine(final, pos0, pos1, xm.reshape(t, d), rgt, mod_l, p["norm_final_g"], ys, tl["tm_combine"])
    return out.reshape(b, s, d)


def kernel(x, c, ada_w, ada_b, norm_attn_g, norm_moe_g, norm_final_g, w_in, w_out, diff_lam_q1, diff_lam_k1, diff_lam_q2, diff_lam_k2, diff_subln_g, swa_sinks, fox_forget_b, nsa_cmp_pos, nsa_cmp_w1, nsa_cmp_b1, nsa_cmp_w2, router_w, router_b, exp_w_gate, exp_w_up, exp_w_down):
    depth = ada_w.shape[0]
    s = x.shape[1]
    mod = _modulation(c, ada_w, ada_b)
    consts = dict(cover=_cover_matrix(s), eneg=_expand_neg(s))
    for l in range(depth):
        p = dict(
            norm_attn_g=norm_attn_g[l], norm_moe_g=norm_moe_g[l], norm_final_g=norm_final_g,
            w_in=w_in[l], w_out=w_out[l],
            diff_lam_q1=diff_lam_q1[l], diff_lam_k1=diff_lam_k1[l],
            diff_lam_q2=diff_lam_q2[l], diff_lam_k2=diff_lam_k2[l], diff_subln_g=diff_subln_g[l],
            swa_sinks=swa_sinks[l], fox_forget_b=fox_forget_b[l],
            nsa_cmp_pos=nsa_cmp_pos[l], nsa_cmp_w1=nsa_cmp_w1[l], nsa_cmp_b1=nsa_cmp_b1[l],
            nsa_cmp_w2=nsa_cmp_w2[l], router_w=router_w, router_b=router_b,
            exp_w_gate=exp_w_gate, exp_w_up=exp_w_up, exp_w_down=exp_w_down,
        )
        x = _layer(l, x, mod[l], p, consts, final=(l == depth - 1))
    return x
```

```python
import functools
import math

import numpy as np
import jax
import jax.numpy as jnp
from jax import lax
from jax.experimental import pallas as pl
from jax.experimental.pallas import tpu as pltpu

F32 = jnp.float32
BF16 = jnp.bfloat16

LANES = 128
HEAD_DIM = 64
HEADS = 4
GROUP_WIDTH = HEADS * HEAD_DIM
DIFF_DK = HEAD_DIM // 2
SWA_WINDOW = 128
SWA_KV_HEADS = 2
CMP_BLOCK = 32
CMP_STRIDE = 16
CMP_HIDDEN = 2 * HEAD_DIM
SEL_BLOCK = 64
SEL_TOPK = 16
NSA_WINDOW = 512
FORCE_SCORE = 1e4
NEG_INF = -1e30
MASK_BIG = 2.0 ** 100
N_EXPERTS = 64
EXPERTS_PER_GROUP = 8
EPS = 1e-6
LOG2E = math.log2(math.e)
VMEM_LIMIT = 56 * 1024 * 1024
WIDE_TILES = 4

IN_SPLITS = (
    GROUP_WIDTH, GROUP_WIDTH, GROUP_WIDTH,
    GROUP_WIDTH, SWA_KV_HEADS * HEAD_DIM, SWA_KV_HEADS * HEAD_DIM,
    GROUP_WIDTH, GROUP_WIDTH, GROUP_WIDTH, HEADS,
    GROUP_WIDTH, HEAD_DIM, HEAD_DIM, HEAD_DIM, HEAD_DIM,
    HEAD_DIM, HEAD_DIM, 3 * HEADS,
)

_NT = (((1,), (1,)), ((), ()))


def _alibi_slopes():
    n = 3 * HEADS
    m = 2.0 ** (-8.0 * np.arange(1, n + 1) / n)
    return m.reshape(HEADS, 3).T


SLOPES = _alibi_slopes()


def _dot(a, b):
    return jnp.dot(a, b, preferred_element_type=F32)


def _dot_nt(a, b):
    return lax.dot_general(a, b, _NT, preferred_element_type=F32)


def _split_bf16(a):
    hi = a.astype(BF16)
    lo = (a - hi.astype(F32)).astype(BF16)
    return hi, lo


def _dot_f32(a, b, nt=False):
    d = _dot_nt if nt else _dot
    ah, al = _split_bf16(a)
    bh, bl = _split_bf16(b)
    return d(ah, bh) + (d(ah, bl) + d(al, bh))


def _cparams(sem):
    return pltpu.CompilerParams(dimension_semantics=sem, vmem_limit_bytes=VMEM_LIMIT)


def _mod_kernel(c_ref, w_ref, b_ref, o_ref):
    c = c_ref[...]
    cond = c * jax.nn.sigmoid(c)
    o_ref[0] = _dot_f32(cond, w_ref[0]) + b_ref[0]


def _modulation(c, ada_w, ada_b):
    depth, d, n = ada_w.shape
    b = c.shape[0]
    rows = 8
    cp = jnp.zeros((rows, d), F32).at[:b].set(c)
    tn = 1536
    out = pl.pallas_call(
        _mod_kernel,
        grid=(depth, n // tn),
        in_specs=[
            pl.BlockSpec((rows, d), lambda l, j: (0, 0)),
            pl.BlockSpec((1, d, tn), lambda l, j: (l, 0, j)),
            pl.BlockSpec((1, 1, tn), lambda l, j: (l, 0, j)),
        ],
        out_specs=pl.BlockSpec((1, rows, tn), lambda l, j: (l, 0, j)),
        out_shape=jax.ShapeDtypeStruct((depth, rows, n), F32),
        compiler_params=_cparams(("arbitrary", "arbitrary")),
        name="adaln_mod",
    )(cp, ada_w, ada_b.reshape(depth, 1, n))
    return out[:, :b].reshape(depth, b, 6, d)


W_A, W_B, W_C, W_D, W_G, W_CMP = 768, 768, 768, 768, 128, 128
W_OFFS = np.cumsum((0, W_A, W_B, W_C, W_D, W_G, W_CMP))
W_TOTAL = int(W_OFFS[-1])


def _prep_w_in(w):
    offs = np.cumsum((0,) + IN_SPLITS)
    col = lambda i: w[:, int(offs[i]):int(offs[i + 1])]
    dup = lambda t: jnp.concatenate([t, t], axis=1)
    kb, vb = col(4), col(5)
    gates = jnp.zeros((w.shape[0], W_G), F32).at[:, :3 * HEADS].set(col(17))
    qs64 = LOG2E * HEAD_DIM ** -0.5
    parts = [
        col(0) * (LOG2E * DIFF_DK ** -0.5), col(1), col(2),
        col(3) * qs64,
        dup(kb[:, :HEAD_DIM]), dup(kb[:, HEAD_DIM:]),
        dup(vb[:, :HEAD_DIM]), dup(vb[:, HEAD_DIM:]),
        col(6) * qs64, col(7), col(8),
        col(10) * qs64, dup(col(13)), dup(col(14)), dup(col(15)), dup(col(16)),
    ] + [gates, col(11), col(12)]
    wp = jnp.concatenate(parts, axis=1).astype(BF16)
    wf = jnp.zeros((8, w.shape[0]), F32).at[:HEADS].set(col(9).T).astype(BF16)
    return wp, wf


def _lane_cumsum(y):
    n = y.shape[1]
    lane = lax.broadcasted_iota(jnp.int32, y.shape, 1)
    sh = 1
    while sh < n:
        y = y + jnp.where(lane >= sh, pltpu.roll(y, sh, 1), 0.0)
        sh *= 2
    return y


def _inproj_kernel(x_ref, mod_ref, g_ref, w_ref, wf_ref, bf_ref,
                   oa, ob, oc, od, og, ocmp, ofc, carry):
    i = pl.program_id(1)
    x = x_ref[0]
    ms = jnp.mean(x * x, axis=-1, keepdims=True)
    y = x * lax.rsqrt(ms + EPS) * g_ref[...]
    h = y * (1.0 + mod_ref[1:2, :]) + mod_ref[0:1, :]
    hb = h.astype(BF16)
    o = W_OFFS
    oa[0] = _dot(hb, w_ref[:, o[0]:o[1]]).astype(BF16)
    ob[0] = _dot(hb, w_ref[:, o[1]:o[2]]).astype(BF16)
    oc[0] = _dot(hb, w_ref[:, o[2]:o[3]]).astype(BF16)
    od[0] = _dot(hb, w_ref[:, o[3]:o[4]]).astype(BF16)
    og[0] = jax.nn.sigmoid(_dot(hb, w_ref[:, o[4]:o[5]])).astype(BF16)
    kvc = _dot(hb, w_ref[:, o[5]:o[6]])
    ocmp[0, 0] = kvc[:, :HEAD_DIM]
    ocmp[1, 0] = kvc[:, HEAD_DIM:]
    fl = _dot_nt(wf_ref[...], hb) + bf_ref[:, 0:1]
    ls = jnp.minimum(fl, 0.0) - jnp.log1p(jnp.exp(-jnp.abs(fl)))

    @pl.when(i == 0)
    def _():
        carry[...] = jnp.zeros_like(carry)

    cs = _lane_cumsum(ls) + carry[:, 0:1]
    ofc[0] = cs
    carry[...] = jnp.broadcast_to(cs[:, cs.shape[1] - 1:], carry.shape)


def _in_projection(x, mod_l, g, wp, wf, bf, tm):
    b, s, d = x.shape
    bf8 = jnp.zeros((8, LANES), F32).at[:HEADS, :].set(bf[:, None])
    outs = pl.pallas_call(
        _inproj_kernel,
        grid=(b, s // tm),
        in_specs=[
            pl.BlockSpec((1, tm, d), lambda bi, i: (bi, i, 0)),
            pl.BlockSpec((None, 6, d), lambda bi, i: (bi, 0, 0)),
            pl.BlockSpec((1, d), lambda bi, i: (0, 0)),
            pl.BlockSpec((d, W_TOTAL), lambda bi, i: (0, 0)),
            pl.BlockSpec((8, d), lambda bi, i: (0, 0)),
            pl.BlockSpec((8, LANES), lambda bi, i: (0, 0)),
        ],
        out_specs=[
            pl.BlockSpec((1, tm, W_A), lambda bi, i: (bi, i, 0)),
            pl.BlockSpec((1, tm, W_B), lambda bi, i: (bi, i, 0)),
            pl.BlockSpec((1, tm, W_C), lambda bi, i: (bi, i, 0)),
            pl.BlockSpec((1, tm, W_D), lambda bi, i: (bi, i, 0)),
            pl.BlockSpec((1, tm, W_G), lambda bi, i: (bi, i, 0)),
            pl.BlockSpec((2, 1, tm, HEAD_DIM), lambda bi, i: (0, bi, i, 0)),
            pl.BlockSpec((1, 8, tm), lambda bi, i: (bi, 0, i)),
        ],
        out_shape=[
            jax.ShapeDtypeStruct((b, s, W_A), BF16),
            jax.ShapeDtypeStruct((b, s, W_B), BF16),
            jax.ShapeDtypeStruct((b, s, W_C), BF16),
            jax.ShapeDtypeStruct((b, s, W_D), BF16),
            jax.ShapeDtypeStruct((b, s, W_G), BF16),
            jax.ShapeDtypeStruct((2, b, s, HEAD_DIM), F32),
            jax.ShapeDtypeStruct((b, 8, s), F32),
        ],
        scratch_shapes=[pltpu.VMEM((8, LANES), F32)],
        compiler_params=_cparams(("arbitrary", "arbitrary")),
        name="in_projection",
    )(x, mod_l, g.reshape(1, d), wp, wf, bf8)
    return outs


def _flash_kernel(kind, t, layer_idx, *refs):
    it = iter(refs)
    q_ref, k_ref, v_ref = next(it), next(it), next(it)
    fc_ref = next(it) if kind == "C" else None
    un_ref, use_ref, head_ref, en_ref = ((next(it), next(it), next(it), next(it)) if kind == "Dsel"
                                         else (None, None, None, None))
    lam_ref, sg_ref, bc_ref = (next(it), next(it), next(it)) if kind == "A" else (None, None, None)
    o_ref, m_scr, acc_scr = next(it), next(it), next(it)

    p = pl.program_id(1)
    i = pl.program_id(2)
    nstream = 4 if kind == "A" else 2
    width = LANES // nstream
    mixer = {"Dsel": 2}.get(kind)

    def head_of(s):
        return s // 2 if kind == "A" else s

    q2 = q_ref[0]
    lane = lax.broadcasted_iota(jnp.int32, (t, LANES), 1)
    zero = jnp.zeros_like(q2)
    qms = [jnp.where((lane >= s * width) & (lane < (s + 1) * width), q2, zero)
           for s in range(nstream)]
    if kind == "Dsel":
        qms = [jnp.concatenate([qm, un_ref[0]], axis=1) for qm in qms]
    if kind == "A":
        pick = [jnp.where((lane >= BIAS_TERMS * h) & (lane < BIAS_TERMS * (h + 1)), 1.0, 0.0).astype(q2.dtype)
                for h in range(2)]
        qms = [jnp.concatenate([qm, pick[head_of(s)]], axis=1) for s, qm in enumerate(qms)]
    qcat = jnp.concatenate(qms, axis=0)

    slopes = None
    if mixer is not None:
        sl = SLOPES[mixer] * LOG2E
        slopes = [jnp.where(p == 0, float(sl[head_of(s)]), float(sl[2 + head_of(s)]))
                  for s in range(nstream)]

    for s in range(nstream):
        m_scr[s] = jnp.full((t, LANES), NEG_INF, F32)
        acc_scr[s] = jnp.zeros((t, LANES), F32)

    def tile(first_key, nk, diag):
        start = pl.multiple_of(first_key, t)
        k2 = k_ref[0, pl.ds(start, nk), :]
        if kind == "Dsel":
            k2 = jnp.concatenate([k2, en_ref[pl.ds(start, nk), :]], axis=1)
        if kind == "A":
            k2 = jnp.concatenate([k2, bc_ref[0, pl.ds(start, nk), :]], axis=1)
        v2 = v_ref[0, pl.ds(start, nk), :]
        vlane = lax.broadcasted_iota(jnp.int32, (nk, LANES), 1)
        one = jnp.ones_like(v2)
        vhalf = [jnp.where(vlane < HEAD_DIM, v2, one), jnp.where(vlane < HEAD_DIM, one, v2)]
        col = lax.broadcasted_iota(jnp.int32, (1, nk), 1)
        rel = (start - i * t + col).astype(F32)
        if diag:
            mask = (lax.broadcasted_iota(jnp.int32, (t, t), 0)
                    >= lax.broadcasted_iota(jnp.int32, (t, t), 1))
        sc_all = _dot_nt(qcat, k2)
        for s in range(nstream):
            sc = sc_all[s * t:(s + 1) * t]
            if slopes is not None:
                sc = sc + slopes[s] * rel
            if kind == "C":
                sc = sc - LOG2E * fc_ref[0, pl.ds(2 * p + s, 1), pl.ds(start, nk)]
            if diag:
                sc = jnp.where(mask, sc, NEG_INF)
            m_prev = m_scr[s]
            m_next = jnp.maximum(m_prev, jnp.max(sc, axis=1, keepdims=True))
            alpha = jnp.exp2(m_prev - m_next)
            pexp = jnp.exp2((sc - jnp.tile(m_next, (1, nk // LANES))).astype(BF16))
            acc_scr[s] = alpha * acc_scr[s] + _dot(pexp, vhalf[head_of(s)])
            m_scr[s] = m_next

    tile(i * t, t, True)

    if kind == "Dsel":
        nq = pl.num_programs(2)
        base = (pl.program_id(0) * nq + i) * nq

        def body(j, carry):
            @pl.when(use_ref[base + j] != 0)
            def _():
                tile(j * t, t, False)
            return carry

        short = (i > 0) & (head_ref[pl.program_id(0) * nq + i] != 0)

        @pl.when(short)
        def _():
            tile(0, LANES, False)

        lax.fori_loop(jnp.where(short, 1, 0), i, body, 0)
    else:
        wide = WIDE_TILES * t

        def body(j, carry):
            tile(j * wide, wide, False)
            return carry

        lax.fori_loop(0, i // WIDE_TILES, body, 0)

        done = i // WIDE_TILES * WIDE_TILES
        width = WIDE_TILES // 2
        while width >= 1:
            take = ((i - done) // width) * width

            @pl.when(take > 0)
            def _(done=done, width=width):
                tile(done * t, width * t, False)

            done = done + take
            width //= 2

    outs = [acc_scr[s] / pltpu.roll(acc_scr[s], HEAD_DIM, 1) for s in range(nstream)]
    if kind == "A":
        lam_init = 0.8 - 0.6 * math.exp(-0.3 * layer_idx)
        t1 = jnp.sum(lam_ref[0:1, :] * lam_ref[1:2, :], axis=1, keepdims=True)
        t2 = jnp.sum(lam_ref[2:3, :] * lam_ref[3:4, :], axis=1, keepdims=True)
        lam = jnp.exp(t1) - jnp.exp(t2) + lam_init
        d0 = outs[0] - lam * outs[1]
        d1 = outs[2] - lam * outs[3]
        lo_half = lane < HEAD_DIM
        o = jnp.where(lo_half, d0, d1)
        sq = o * o
        ss_lo = jnp.sum(jnp.where(lo_half, sq, 0.0), axis=1, keepdims=True)
        ss_hi = jnp.sum(jnp.where(lo_half, 0.0, sq), axis=1, keepdims=True)
        ms = jnp.where(lo_half, ss_lo, ss_hi) * (1.0 / HEAD_DIM)
        o = o * lax.rsqrt(ms + EPS) * sg_ref[...] * (1.0 - lam_init)
    else:
        o = jnp.where(lane < HEAD_DIM, outs[0], outs[1])
    o_ref[0] = o.astype(o_ref.dtype)


BIAS_TERMS = 3


def _alibi_key_columns(s):
    cols = np.zeros((2, s, LANES), np.float32)
    pos = np.arange(s, dtype=np.float64)
    for pair in range(2):
        for h in range(2):
            rest = SLOPES[0][2 * pair + h] * LOG2E * pos
            for n in range(BIAS_TERMS):
                term = rest.astype(jnp.bfloat16).astype(np.float64)
                cols[pair, :, BIAS_TERMS * h + n] = term
                rest = rest - term
    return jnp.asarray(cols, BF16)


def _flash(kind, t, layer_idx, q_arr, qoff, k_arr, koff, v_arr, voff, extras):
    b, s, _ = q_arr.shape
    nstream = 4 if kind == "A" else 2
    shared_kv = kind == "Dsel"
    kv_idx = (lambda off: (lambda bi, p, i: (bi, 0, off))) if shared_kv else \
             (lambda off: (lambda bi, p, i: (bi, 0, off + p)))
    in_specs = [
        pl.BlockSpec((1, t, LANES), lambda bi, p, i: (bi, i, qoff + p)),
        pl.BlockSpec((1, s, LANES), kv_idx(koff)),
        pl.BlockSpec((1, s, LANES), kv_idx(voff)),
    ]
    args = [q_arr, k_arr, v_arr]
    if kind == "C":
        fcum, = extras
        in_specs.append(pl.BlockSpec((1, 8, s), lambda bi, p, i: (bi, 0, 0)))
        args.append(fcum)
    elif kind == "Dsel":
        unsel, eneg = extras
        nsel = unsel.shape[-1]
        nq, per = s // t, t // SEL_BLOCK
        picked = unsel.reshape(b, nq, t, nq, per) == 0
        used = picked.any(axis=(2, 4)).astype(jnp.int32).reshape(-1)
        head_only = jnp.logical_not(picked[:, :, :, 0, LANES // SEL_BLOCK:].any(axis=(2, 3)))
        head_only = head_only.astype(jnp.int32).reshape(-1)
        in_specs.append(pl.BlockSpec((1, t, nsel), lambda bi, p, i: (bi, i, 0)))
        in_specs.append(pl.BlockSpec(memory_space=pltpu.SMEM))
        in_specs.append(pl.BlockSpec(memory_space=pltpu.SMEM))
        in_specs.append(pl.BlockSpec((s, nsel), lambda bi, p, i: (0, 0)))
        args += [unsel, used, head_only, eneg]
    elif kind == "A":
        lam8, sg = extras
        in_specs.append(pl.BlockSpec((8, LANES), lambda bi, p, i: (0, 0)))
        in_specs.append(pl.BlockSpec((1, LANES), lambda bi, p, i: (0, 0)))
        in_specs.append(pl.BlockSpec((1, s, LANES), lambda bi, p, i: (p, 0, 0)))
        args += [lam8, sg, _alibi_key_columns(s)]
    return pl.pallas_call(
        functools.partial(_flash_kernel, kind, t, layer_idx),
        grid=(b, 2, s // t),
        in_specs=in_specs,
        out_specs=pl.BlockSpec((1, t, LANES), lambda bi, p, i: (bi, i, p)),
        out_shape=jax.ShapeDtypeStruct((b, s, 2 * LANES), BF16),
        scratch_shapes=[pltpu.VMEM((nstream, t, LANES), F32)] * 2,
        compiler_params=_cparams(("arbitrary", "arbitrary", "arbitrary")),
        name="flash_" + kind,
    )(*args)


def _window_kernel(kind, t, wpad, *refs):
    it = iter(refs)
    q_ref, k_ref, v_ref = next(it), next(it), next(it)
    sink_ref = next(it) if kind == "B" else None
    o_ref = next(it)
    i = pl.program_id(1)
    window = SWA_WINDOW if kind == "B" else NSA_WINDOW
    sl = SLOPES[1 if kind == "B" else 2] * LOG2E
    nk = wpad + t
    start = pl.multiple_of(jnp.maximum(i * t - wpad, 0), LANES)
    lane = lax.broadcasted_iota(jnp.int32, (t, LANES), 1)
    vlane = lax.broadcasted_iota(jnp.int32, (nk, LANES), 1)
    dist = ((i * t - start) + lax.broadcasted_iota(jnp.int32, (t, nk), 0)
            - lax.broadcasted_iota(jnp.int32, (t, nk), 1))
    valid = (dist >= 0) & (dist < window)
    rel = (start - i * t + lax.broadcasted_iota(jnp.int32, (1, nk), 1)).astype(F32)
    qrow = lax.broadcasted_iota(jnp.int32, (t, 1), 0).astype(F32)

    def masked_q(pair, g):
        q2 = q_ref[0, :, pair * LANES:(pair + 1) * LANES]
        return jnp.where((lane >= g * HEAD_DIM) & (lane < (g + 1) * HEAD_DIM), q2, jnp.zeros_like(q2))

    def kv_tiles(pair):
        cols = slice(pair * LANES, (pair + 1) * LANES) if kind == "B" else slice(0, LANES)
        k2 = k_ref[0, pl.ds(start, nk), cols]
        v2 = v_ref[0, pl.ds(start, nk), cols]
        one = jnp.ones_like(v2)
        return k2, [jnp.where(vlane < HEAD_DIM, v2, one), jnp.where(vlane < HEAD_DIM, one, v2)]

    groups = [[(0, 0), (0, 1)], [(1, 0), (1, 1)]] if kind == "B" else [[(0, 0), (0, 1), (1, 0), (1, 1)]]
    outs = {}
    for group in groups:
        k2, vhalf = kv_tiles(group[0][0])
        sc_all = _dot_nt(jnp.concatenate([masked_q(pair, g) for pair, g in group], axis=0), k2)
        for n, (pair, g) in enumerate(group):
            h = 2 * pair + g
            slope = float(sl[h])
            sc = jnp.where(valid, sc_all[n * t:(n + 1) * t] + slope * rel, NEG_INF)
            m = jnp.max(sc, axis=1, keepdims=True)
            if kind == "B":
                sink = LOG2E * sink_ref[h] + slope * qrow
                m = jnp.maximum(m, sink)
            pv = _dot(jnp.exp2((sc - m).astype(BF16)), vhalf[g])
            l = pltpu.roll(pv, HEAD_DIM, 1)
            if kind == "B":
                l = l + jnp.exp2(sink - m)
            outs[h] = pv / l
    for pair in range(2):
        o_ref[0, :, pair * LANES:(pair + 1) * LANES] = jnp.where(
            lane < HEAD_DIM, outs[2 * pair], outs[2 * pair + 1]).astype(o_ref.dtype)


def _window(kind, t, wpad, q_arr, k_arr, koff, v_arr, voff, extras):
    b, s, _ = q_arr.shape
    kw = 2 * LANES if kind == "B" else LANES
    in_specs = [
        pl.BlockSpec((1, t, 2 * LANES), lambda bi, i: (bi, i, 0)),
        pl.BlockSpec((1, s, kw), lambda bi, i: (bi, 0, koff)),
        pl.BlockSpec((1, s, kw), lambda bi, i: (bi, 0, voff)),
    ]
    args = [q_arr, k_arr, v_arr]
    if kind == "B":
        in_specs.append(pl.BlockSpec(memory_space=pltpu.SMEM))
        args.append(extras[0])
    return pl.pallas_call(
        functools.partial(_window_kernel, kind, t, wpad),
        grid=(b, s // t),
        in_specs=in_specs,
        out_specs=pl.BlockSpec((1, t, 2 * LANES), lambda bi, i: (bi, i, 0)),
        out_shape=jax.ShapeDtypeStruct((b, s, 2 * LANES), BF16),
        compiler_params=_cparams(("arbitrary", "arbitrary")),
        name="window_" + kind,
    )(*args)


def _gelu_tanh(x):
    return 0.5 * x * (1.0 + jnp.tanh(math.sqrt(2.0 / math.pi) * (x + 0.044715 * x * x * x)))


def _compress_kernel(t_ref, w1_ref, pos_ref, b1_ref, w2_ref, o_ref):
    tr = t_ref[0, 0]
    half = tr.shape[1]
    n = tr.shape[0]
    w1 = w1_ref[0]
    u = _dot_f32(tr, w1[:half])
    v = _dot_f32(tr, w1[half:])
    cpos = _dot_f32(pos_ref[0], w1)[0:1] + b1_ref[0]
    hid = u + pltpu.roll(v, n - 1, 0) + cpos
    o_ref[0, 0] = _dot_f32(_gelu_tanh(hid), w2_ref[0])


def _compress(kv_r, w1, pos8, b1, w2d):
    two, b, n, dd = kv_r.shape
    return pl.pallas_call(
        _compress_kernel,
        grid=(two, b),
        in_specs=[
            pl.BlockSpec((1, 1, n, dd), lambda c, bi: (c, bi, 0, 0)),
            pl.BlockSpec((1, 2 * dd, CMP_HIDDEN), lambda c, bi: (c, 0, 0)),
            pl.BlockSpec((1, 8, 2 * dd), lambda c, bi: (c, 0, 0)),
            pl.BlockSpec((1, 1, CMP_HIDDEN), lambda c, bi: (c, 0, 0)),
            pl.BlockSpec((1, CMP_HIDDEN, LANES), lambda c, bi: (c, 0, 0)),
        ],
        out_specs=pl.BlockSpec((1, 1, n, LANES), lambda c, bi: (c, bi, 0, 0)),
        out_shape=jax.ShapeDtypeStruct((two, b, n, LANES), F32),
        compiler_params=_cparams(("arbitrary", "arbitrary")),
        name="nsa_compress",
    )(kv_r, w1, pos8, b1, w2d)


def _cmp_kernel(t, first_tile, ncp_total, q_ref, kc_ref, vc_ref, cov_ref, o_ref, un_ref):
    i = first_tile + pl.program_id(1)
    ncp = kc_ref.shape[2]
    nsel = cov_ref.shape[0]
    kc = kc_ref[0, 0]
    vc = vc_ref[0, 0]
    kch, kcl = _split_bf16(kc)
    vcb = vc.astype(BF16)
    lane = lax.broadcasted_iota(jnp.int32, (t, LANES), 1)
    tq = i * t + lax.broadcasted_iota(jnp.int32, (t, ncp), 0)
    nidx = lax.broadcasted_iota(jnp.int32, (t, ncp), 1)
    cmp_end = nidx * CMP_STRIDE + (CMP_BLOCK - 1)
    valid = (tq >= cmp_end) & (nidx < ncp_total - 1)
    end_row = cmp_end[0:1, :].astype(F32)
    psum = jnp.zeros((t, ncp), F32)
    outs = []
    for pair in range(2):
        q2 = q_ref[0, :, pair * LANES:(pair + 1) * LANES]
        for g in range(2):
            h = 2 * pair + g
            qm = jnp.where((lane >= g * HEAD_DIM) & (lane < (g + 1) * HEAD_DIM), q2, jnp.zeros_like(q2))
            sc = _dot_nt(qm, kch) + _dot_nt(qm, kcl) + float(SLOPES[2][h] * LOG2E) * end_row
            sc = jnp.where(valid, sc, NEG_INF)
            m = jnp.max(sc, axis=1, keepdims=True)
            pe = jnp.exp2(sc - m)
            l = jnp.sum(pe, axis=1, keepdims=True)
            pc = pe * jnp.where(m > 0.5 * NEG_INF, 1.0 / l, 0.0)
            psum = psum + pc
            outs.append(_dot(pc.astype(BF16), vcb))
    lo_half = lane < HEAD_DIM
    o_ref[0, :, 0:LANES] = jnp.where(lo_half, outs[0], outs[1]).astype(o_ref.dtype)
    o_ref[0, :, LANES:2 * LANES] = jnp.where(lo_half, outs[2], outs[3]).astype(o_ref.dtype)
    ph, plo = _split_bf16(psum)
    cov = cov_ref[...]
    imp = _dot_nt(cov, ph) + _dot_nt(cov, plo)
    blk = lax.broadcasted_iota(jnp.int32, (nsel, t), 0)
    cur = (i * t + lax.broadcasted_iota(jnp.int32, (nsel, t), 1)) // SEL_BLOCK
    forced = (blk == 0) | (blk == cur) | (blk == cur - 1)
    score = jnp.where(forced, FORCE_SCORE, jnp.where(blk <= cur, imp, -1.0))
    unsel = jnp.ones((nsel, t), F32)
    for _ in range(min(SEL_TOPK, nsel)):
        mx = jnp.max(score, axis=0, keepdims=True)
        idx = jnp.min(jnp.where(score == mx, blk, nsel), axis=0, keepdims=True)
        hit = blk == idx
        unsel = jnp.where(hit, 0.0, unsel)
        score = jnp.where(hit, -3.0, score)
    un_ref[0] = unsel.T.astype(un_ref.dtype)


CMP_RANGES = 4


def _cmp_attention(t, qd, kvc, cover_t):
    b, s, _ = qd.shape
    ncp_total = kvc.shape[2]
    nsel = cover_t.shape[0]
    steps = s // t // CMP_RANGES
    outs = []
    for r in range(CMP_RANGES):
        ncp = (r + 1) * ncp_total // CMP_RANGES
        first = r * steps
        outs.append(pl.pallas_call(
            functools.partial(_cmp_kernel, t, first, ncp_total),
            grid=(b, steps),
            in_specs=[
                pl.BlockSpec((1, t, 2 * LANES), lambda bi, i, first=first: (bi, first + i, 0)),
                pl.BlockSpec((1, 1, ncp, LANES), lambda bi, i: (0, bi, 0, 0)),
                pl.BlockSpec((1, 1, ncp, LANES), lambda bi, i: (1, bi, 0, 0)),
                pl.BlockSpec((nsel, ncp), lambda bi, i: (0, 0)),
            ],
            out_specs=[
                pl.BlockSpec((1, t, 2 * LANES), lambda bi, i: (bi, i, 0)),
                pl.BlockSpec((1, t, nsel), lambda bi, i: (bi, i, 0)),
            ],
            out_shape=[
                jax.ShapeDtypeStruct((b, steps * t, 2 * LANES), BF16),
                jax.ShapeDtypeStruct((b, steps * t, nsel), BF16),
            ],
            compiler_params=_cparams(("arbitrary", "arbitrary")),
            name="nsa_cmp_select",
        )(qd, kvc, kvc, cover_t))
    return (jnp.concatenate([o for o, _ in outs], axis=1),
            jnp.concatenate([u for _, u in outs], axis=1))


def _cover_matrix(s):
    ncp, nsel = s // CMP_STRIDE, s // SEL_BLOCK
    cs = np.arange(ncp)[:, None] * CMP_STRIDE
    ss = np.arange(nsel)[None, :] * SEL_BLOCK
    cov = np.clip(np.minimum(cs + CMP_BLOCK, ss + SEL_BLOCK) - np.maximum(cs, ss), 0, None) / CMP_BLOCK
    cov[ncp - 1] = 0.0
    return jnp.asarray(cov.T, BF16)


def _expand_neg(s):
    nsel = s // SEL_BLOCK
    e = (np.arange(s)[:, None] // SEL_BLOCK) == np.arange(nsel)[None, :]
    return jnp.asarray(np.where(e, -MASK_BIG, 0.0), BF16)


def _outproj_kernel(ya, yb, yc, ocmp, osel, owin, gt, gx_ref, x_ref, mod_ref, wo_ref, g_ref,
                    rwt_ref, rb_ref, xo_ref, h_ref, ri_ref, rg_ref, cnt_ref, carry):
    first = (pl.program_id(0) == 0) & (pl.program_id(1) == 0)
    gw = GROUP_WIDTH
    g = _dot(gt[0], gx_ref[...])
    yd = (g[:, 0:gw] * ocmp[0].astype(F32) + g[:, gw:2 * gw] * osel[0].astype(F32)
          + g[:, 2 * gw:3 * gw] * owin[0].astype(F32))
    ycat = jnp.concatenate([ya[0], yb[0], yc[0], yd.astype(BF16)], axis=1)
    y = _dot(ycat, wo_ref[...])
    x = x_ref[0] + mod_ref[2:3, :] * y
    xo_ref[0] = x
    ms = jnp.mean(x * x, axis=-1, keepdims=True)
    h = x * lax.rsqrt(ms + EPS) * g_ref[...]
    h = h * (1.0 + mod_ref[4:5, :]) + mod_ref[3:4, :]
    h_ref[0] = h
    tm = h.shape[0]
    ng = LANES // EXPERTS_PER_GROUP
    big = 4 * LANES
    logits = _dot_f32(rwt_ref[...], h, nt=True)
    eidx = lax.broadcasted_iota(jnp.int32, (LANES, tm), 0)
    aff = jax.nn.sigmoid(logits)
    sel = jnp.where(eidx < N_EXPERTS, aff + rb_ref[:, 0:1], NEG_INF)
    sel3 = sel.reshape(ng, EXPERTS_PER_GROUP, tm)
    e3 = eidx.reshape(ng, EXPERTS_PER_GROUP, tm)
    g1 = jnp.max(sel3, axis=1, keepdims=True)
    i1 = jnp.min(jnp.where(sel3 == g1, e3, big), axis=1, keepdims=True)
    sel_b = jnp.where(e3 == i1, NEG_INF, sel3)
    g2 = jnp.max(sel_b, axis=1, keepdims=True)
    i2 = jnp.min(jnp.where(sel_b == g2, e3, big), axis=1, keepdims=True)
    gs = g1 + g2
    gidx = lax.broadcasted_iota(jnp.int32, (ng, 1, tm), 0)
    gm = jnp.max(gs, axis=0, keepdims=True)
    best = gidx == jnp.min(jnp.where(gs == gm, gidx, big), axis=0, keepdims=True)
    e1 = jnp.min(jnp.where(best, i1, big), axis=0)
    e2 = jnp.min(jnp.where(best, i2, big), axis=0)
    oh1 = eidx == e1
    oh2 = eidx == e2
    a1 = jnp.sum(jnp.where(oh1, aff, 0.0), axis=0, keepdims=True)
    a2 = jnp.sum(jnp.where(oh2, aff, 0.0), axis=0, keepdims=True)
    inv = 1.0 / (a1 + a2)
    @pl.when(first)
    def _():
        carry[...] = jnp.zeros_like(carry)

    ohs = jnp.where(oh1 | oh2, 1.0, 0.0)
    rr = lax.broadcasted_iota(jnp.int32, (tm, tm), 0)
    cc = lax.broadcasted_iota(jnp.int32, (tm, tm), 1)
    earlier = jnp.where(rr < cc, 1.0, 0.0).astype(BF16)
    before = _dot(ohs.astype(BF16), earlier) + carry[:, 0:1]
    r1 = jnp.sum(jnp.where(oh1, before, 0.0), axis=0, keepdims=True)
    r2 = jnp.sum(jnp.where(oh2, before, 0.0), axis=0, keepdims=True)
    total = carry[:, 0:1] + jnp.sum(ohs, axis=1, keepdims=True)
    carry[...] = jnp.broadcast_to(total, carry.shape)
    cnt_ref[...] = jnp.broadcast_to(total, cnt_ref.shape).astype(jnp.int32)
    row = lax.broadcasted_iota(jnp.int32, (8, tm), 0)
    ri_ref[0] = jnp.where(row == 0, e1, jnp.where(row == 1, e2, jnp.where(
        row == 2, r1.astype(jnp.int32), jnp.where(row == 3, r2.astype(jnp.int32), 0))))
    rg_ref[0] = jnp.where(row == 0, a1 * inv, jnp.where(row == 1, a2 * inv, 0.0))


def _gate_expander():
    e = np.zeros((W_G, 3 * GROUP_WIDTH), np.float32)
    for h in range(HEADS):
        for r in range(3):
            e[h * 3 + r, r * GROUP_WIDTH + h * HEAD_DIM:r * GROUP_WIDTH + (h + 1) * HEAD_DIM] = 1.0
    return jnp.asarray(e, BF16)


def _out_projection(ya, yb, yc, ocmp, osel, owin, gates, x, mod_l, wo, g, rwt, rb, tm):
    b, s, d = x.shape
    gw = GROUP_WIDTH
    tok = lambda w: pl.BlockSpec((1, tm, w), lambda bi, i: (bi, i, 0))
    full = lambda shp: pl.BlockSpec(shp, lambda bi, i: (0,) * len(shp))
    rowblk = pl.BlockSpec((1, 8, tm), lambda bi, i: (bi, 0, i))
    return pl.pallas_call(
        _outproj_kernel,
        grid=(b, s // tm),
        in_specs=[tok(gw)] * 6 + [tok(W_G), full((W_G, 3 * gw)), tok(d),
                                  pl.BlockSpec((None, 6, d), lambda bi, i: (bi, 0, 0)),
                                  full((d, d)), full((1, d)), full((LANES, d)), full((LANES, 1))],
        out_specs=[tok(d), tok(d), rowblk, rowblk, full((LANES, LANES))],
        out_shape=[
            jax.ShapeDtypeStruct((b, s, d), F32),
            jax.ShapeDtypeStruct((b, s, d), F32),
            jax.ShapeDtypeStruct((b, 8, s), jnp.int32),
            jax.ShapeDtypeStruct((b, 8, s), F32),
            jax.ShapeDtypeStruct((LANES, LANES), jnp.int32),
        ],
        scratch_shapes=[pltpu.VMEM((LANES, LANES), F32)],
        compiler_params=_cparams(("arbitrary", "arbitrary")),
        name="out_projection_router",
    )(ya, yb, yc, ocmp, osel, owin, gates, _gate_expander(), x, mod_l, wo, g.reshape(1, d), rwt, rb)


def _dispatch_kernel(p0_ref, p1_ref, h_ref, xs_ref, sem):
    tm = h_ref.shape[0]

    def issue(r, c):
        for pos_ref in (p0_ref, p1_ref):
            pltpu.make_async_copy(h_ref.at[pl.ds(r, 1), :],
                                  xs_ref.at[pl.ds(pos_ref[r], 1), :], sem).start()
        return c

    lax.fori_loop(0, tm, issue, 0, unroll=8)
    for _ in range(2):
        pltpu.make_async_copy(h_ref, xs_ref.at[pl.ds(0, tm), :], sem).wait()


def _dispatch(h, pos0, pos1, tm):
    t, d = h.shape
    return pl.pallas_call(
        _dispatch_kernel,
        grid=(t // tm,),
        in_specs=[
            pl.BlockSpec((tm,), lambda i: (i,), memory_space=pltpu.SMEM),
            pl.BlockSpec((tm,), lambda i: (i,), memory_space=pltpu.SMEM),
            pl.BlockSpec((tm, d), lambda i: (i, 0)),
        ],
        out_specs=pl.BlockSpec(memory_space=pl.ANY),
        out_shape=jax.ShapeDtypeStruct((2 * t, d), F32),
        scratch_shapes=[pltpu.SemaphoreType.DMA(())],
        compiler_params=_cparams(("arbitrary",)),
        name="moe_dispatch",
    )(pos0, pos1, h)


def _expert_kernel(layer, vb_ref, ve_ref, vlo_ref, vhi_ref, vord_ref, vnext_ref, nv_ref,
                   xs_ref, wg_hbm, wu_hbm, wd_hbm, ys_ref, wgf, wuf, wdf, wgb, wub, wdb, sem):
    v = pl.program_id(0)
    rows = xs_ref.shape[0]
    prev = jnp.maximum(v - 1, 0)
    new_expert = (v == 0) | (ve_ref[v] != ve_ref[prev])
    new_block = (v == 0) | (vb_ref[v] != vb_ref[prev])

    def weight_copies(e, slot):
        return [pltpu.make_async_copy(src.at[layer, e], dst.at[slot], sem.at[slot, n])
                for n, (src, dst) in enumerate(((wg_hbm, wgf), (wu_hbm, wuf), (wd_hbm, wdf)))]

    @pl.when(v < nv_ref[0])
    def _():
        @pl.when(new_expert)
        def _():
            slot = vord_ref[v] % 2

            @pl.when(v == 0)
            def _():
                for c in weight_copies(ve_ref[v], slot):
                    c.start()

            @pl.when(vnext_ref[v] >= 0)
            def _():
                for c in weight_copies(vnext_ref[v], 1 - slot):
                    c.start()

            for c in weight_copies(ve_ref[v], slot):
                c.wait()
            wgb[...] = wgf[slot].astype(BF16)
            wub[...] = wuf[slot].astype(BF16)
            wdb[...] = wdf[slot].astype(BF16)

        x = xs_ref[...].astype(BF16)
        hg = _dot(x, wgb[...])
        hu = _dot(x, wub[...])
        hm = (hg * jax.nn.sigmoid(hg) * hu).astype(BF16)
        y = _dot(hm, wdb[...])
        r = vb_ref[v] * rows + lax.broadcasted_iota(jnp.int32, (rows, 1), 0)
        mine = (r >= vlo_ref[v]) & (r < vhi_ref[v])

        @pl.when(new_block)
        def _():
            ys_ref[...] = jnp.where(mine, y, 0.0)

        @pl.when(jnp.logical_not(new_block))
        def _():
            ys_ref[...] = jnp.where(mine, y, ys_ref[...])


def _experts(xs, meta, layer, wg, wu, wd, rows, nvmax):
    n, d = xs.shape
    de = wg.shape[3]
    blk = lambda v, vb, ve, lo, hi, vo, vn, nv: (vb[v], 0)
    grid_spec = pltpu.PrefetchScalarGridSpec(
        num_scalar_prefetch=7,
        grid=(nvmax,),
        in_specs=[
            pl.BlockSpec((rows, d), blk),
            pl.BlockSpec(memory_space=pl.ANY),
            pl.BlockSpec(memory_space=pl.ANY),
            pl.BlockSpec(memory_space=pl.ANY),
        ],
        out_specs=pl.BlockSpec((rows, d), blk),
        scratch_shapes=[pltpu.VMEM((2, d, de), F32), pltpu.VMEM((2, d, de), F32), pltpu.VMEM((2, de, d), F32),
                        pltpu.VMEM((d, de), BF16), pltpu.VMEM((d, de), BF16), pltpu.VMEM((de, d), BF16),
                        pltpu.SemaphoreType.DMA((2, 3))],
    )
    return pl.pallas_call(
        functools.partial(_expert_kernel, layer),
        grid_spec=grid_spec,
        out_shape=jax.ShapeDtypeStruct((n, d), F32),
        compiler_params=_cparams(("arbitrary",)),
        name="moe_experts",
    )(*meta, xs, wg, wu, wd)


def _combine_kernel(final, p0_ref, p1_ref, x_ref, rg_ref, mod_ref, g_ref, ys_ref, o_ref, buf, sem):
    tm = x_ref.shape[0]

    def issue(r, c):
        for k, pos_ref in enumerate((p0_ref, p1_ref)):
            pltpu.make_async_copy(ys_ref.at[pl.ds(pos_ref[r], 1), :],
                                  buf.at[k, pl.ds(r, 1), :], sem).start()
        return c

    lax.fori_loop(0, tm, issue, 0, unroll=8)
    for k in range(2):
        pltpu.make_async_copy(ys_ref.at[pl.ds(0, tm), :], buf.at[k], sem).wait()
    rg = rg_ref[...]
    y = rg[:, 0:1] * buf[0] + rg[:, 1:2] * buf[1]
    x = x_ref[...] + mod_ref[5:6, :] * y
    if final:
        ms = jnp.mean(x * x, axis=-1, keepdims=True)
        x = x * lax.rsqrt(ms + EPS) * g_ref[...]
    o_ref[...] = x


def _combine(final, pos0, pos1, x, rg, mod_l, g, ys, tm):
    t, d = x.shape
    per_b = t // mod_l.shape[0] // tm
    return pl.pallas_call(
        functools.partial(_combine_kernel, final),
        grid=(t // tm,),
        in_specs=[
            pl.BlockSpec((tm,), lambda i: (i,), memory_space=pltpu.SMEM),
            pl.BlockSpec((tm,), lambda i: (i,), memory_space=pltpu.SMEM),
            pl.BlockSpec((tm, d), lambda i: (i, 0)),
            pl.BlockSpec((tm, 8), lambda i: (i, 0)),
            pl.BlockSpec((None, 6, d), lambda i: (i // per_b, 0, 0)),
            pl.BlockSpec((1, d), lambda i: (0, 0)),
            pl.BlockSpec(memory_space=pl.ANY),
        ],
        out_specs=pl.BlockSpec((tm, d), lambda i: (i, 0)),
        out_shape=jax.ShapeDtypeStruct((t, d), F32),
        scratch_shapes=[pltpu.VMEM((2, tm, d), F32), pltpu.SemaphoreType.DMA(())],
        compiler_params=_cparams(("arbitrary",)),
        name="moe_combine",
    )(pos0, pos1, x, rg, mod_l, g.reshape(1, d), ys)


def _visit_plan(counts, rows, nblocks):
    ne = counts.shape[0]
    nvmax = nblocks + ne - 1
    ends = jnp.cumsum(counts)
    offs = ends - counts
    b_lo = offs // rows
    b_hi = jnp.maximum(ends - 1, 0) // rows
    nvis = jnp.where(counts > 0, b_hi - b_lo + 1, 0)
    vend = jnp.cumsum(nvis)
    vstart = vend - nvis
    nv = vend[-1]
    v = jnp.minimum(jnp.arange(nvmax), nv - 1)
    e = jnp.sum((v[:, None] >= vend[None, :]).astype(jnp.int32), axis=1)
    onehot = e[:, None] == jnp.arange(ne)[None, :]
    pick = lambda a: jnp.sum(jnp.where(onehot, a[None, :], 0), axis=1)
    blk = pick(b_lo) + v - pick(vstart)
    lo = jnp.maximum(pick(offs), blk * rows)
    hi = jnp.minimum(pick(ends), (blk + 1) * rows)
    order = jnp.cumsum((counts > 0).astype(jnp.int32)) - 1
    ids = jnp.where(counts > 0, jnp.arange(ne), ne)
    later = jnp.concatenate([jnp.flip(lax.cummin(jnp.flip(ids)))[1:], jnp.full((1,), ne, ids.dtype)])
    nxt = jnp.where(later < ne, later, -1)
    i32 = lambda a: a.astype(jnp.int32)
    return (i32(blk), i32(e), i32(lo), i32(hi), i32(pick(order)), i32(pick(nxt)),
            i32(nv).reshape(1)), offs, nvmax


def _tiles(s):
    tm = min(512, s)
    return dict(tm=tm, t_full=min(512, s), t_win=min(256, s), t_cmp=min(1024, s),
                tm_dispatch=min(2048, s), tm_combine=min(1024, s), rows=256)


def _layer(l, x, mod_l, p, consts, final):
    b, s, d = x.shape
    tl = _tiles(s)
    wp, wf = _prep_w_in(p["w_in"])
    qa, qb, qc, qd, gates, kvcmp, fcum = _in_projection(
        x, mod_l, p["norm_attn_g"], wp, wf, p["fox_forget_b"], tl["tm"])
    lam8 = jnp.zeros((8, LANES), F32).at[:4, :DIFF_DK].set(
        jnp.stack([p["diff_lam_q1"], p["diff_lam_k1"], p["diff_lam_q2"], p["diff_lam_k2"]]))
    sg = jnp.tile(p["diff_subln_g"], 2).reshape(1, LANES)
    ya = _flash("A", tl["t_full"], l, qa, 0, qa, 2, qa, 4, (lam8, sg))
    yb = _window("B", tl["t_win"], SWA_WINDOW, qb, qb, 1, qb, 2, (p["swa_sinks"],))
    yc = _flash("C", tl["t_full"], l, qc, 0, qc, 2, qc, 4, (fcum,))
    n16 = s // CMP_STRIDE
    kv_r = kvcmp.reshape(2, b, n16, CMP_STRIDE * HEAD_DIM)
    pos8 = jnp.zeros((2, 8, CMP_BLOCK * HEAD_DIM), F32).at[:, 0].set(
        p["nsa_cmp_pos"].reshape(2, CMP_BLOCK * HEAD_DIM))
    w2d = jnp.concatenate([p["nsa_cmp_w2"], p["nsa_cmp_w2"]], axis=-1)
    kvc = _compress(kv_r, p["nsa_cmp_w1"], pos8, p["nsa_cmp_b1"].reshape(2, 1, CMP_HIDDEN), w2d)
    ocmp, unsel = _cmp_attention(tl["t_cmp"], qd, kvc, consts["cover"])
    osel = _flash("Dsel", tl["t_full"], l, qd, 0, qd, 2, qd, 3, (unsel, consts["eneg"]))
    owin = _window("Dwin", tl["t_win"], NSA_WINDOW, qd, qd, 4, qd, 5, ())
    rwt = jnp.zeros((LANES, d), F32).at[:N_EXPERTS].set(p["router_w"].T)
    rb = jnp.zeros((LANES, 1), F32).at[:N_EXPERTS, 0].set(p["router_b"])
    xm, h2, ri, rg, cnt = _out_projection(
        ya, yb, yc, ocmp, osel, owin, gates, x, mod_l, p["w_out"].astype(BF16),
        p["norm_moe_g"], rwt, rb, tl["tm"])
    t = b * s
    counts = cnt[:N_EXPERTS, 0]
    rows = tl["rows"]
    meta, offs, nvmax = _visit_plan(counts, rows, 2 * t // rows)
    onehot = ri[:, 0:2, :, None] == jnp.arange(N_EXPERTS)[None, None, None, :]
    pos = (jnp.sum(jnp.where(onehot, offs[None, None, None, :], 0), axis=-1) + ri[:, 2:4]).astype(jnp.int32)
    pos0, pos1 = pos[:, 0].reshape(t), pos[:, 1].reshape(t)
    rgt = rg.transpose(0, 2, 1).reshape(t, 8)
    xs = _dispatch(h2.reshape(t, d), pos0, pos1, tl["tm_dispatch"])
    ys = _experts(xs, meta, l, p["exp_w_gate"], p["exp_w_up"], p["exp_w_down"], rows, nvmax)
    out = _combine(final, pos0, pos1, xm.reshape(t, d), rgt, mod_l, p["norm_final_g"], ys, tl["tm_combine"])
    return out.reshape(b, s, d)


def kernel(x, c, ada_w, ada_b, norm_attn_g, norm_moe_g, norm_final_g, w_in, w_out, diff_lam_q1, diff_lam_k1, diff_lam_q2, diff_lam_k2, diff_subln_g, swa_sinks, fox_forget_b, nsa_cmp_pos, nsa_cmp_w1, nsa_cmp_b1, nsa_cmp_w2, router_w, router_b, exp_w_gate, exp_w_up, exp_w_down):
    depth = ada_w.shape[0]
    s = x.shape[1]
    mod = _modulation(c, ada_w, ada_b)
    consts = dict(cover=_cover_matrix(s), eneg=_expand_neg(s))
    for l in range(depth):
        p = dict(
            norm_attn_g=norm_attn_g[l], norm_moe_g=norm_moe_g[l], norm_final_g=norm_final_g,
            w_in=w_in[l], w_out=w_out[l],
            diff_lam_q1=diff_lam_q1[l], diff_lam_k1=diff_lam_k1[l],
            diff_lam_q2=diff_lam_q2[l], diff_lam_k2=diff_lam_k2[l], diff_subln_g=diff_subln_g[l],
            swa_sinks=swa_sinks[l], fox_forget_b=fox_forget_b[l],
            nsa_cmp_pos=nsa_cmp_pos[l], nsa_cmp_w1=nsa_cmp_w1[l], nsa_cmp_b1=nsa_cmp_b1[l],
            nsa_cmp_w2=nsa_cmp_w2[l], router_w=router_w, router_b=router_b,
            exp_w_gate=exp_w_gate, exp_w_up=exp_w_up, exp_w_down=exp_w_down,
        )
        x = _layer(l, x, mod[l], p, consts, final=(l == depth - 1))
    return x
```

```python
import functools
import math

import numpy as np
import jax
import jax.numpy as jnp
from jax import lax
from jax.experimental import pallas as pl
from jax.experimental.pallas import tpu as pltpu

F32 = jnp.float32
BF16 = jnp.bfloat16

LANES = 128
HEAD_DIM = 64
HEADS = 4
GROUP_WIDTH = HEADS * HEAD_DIM
DIFF_DK = HEAD_DIM // 2
SWA_WINDOW = 128
SWA_KV_HEADS = 2
CMP_BLOCK = 32
CMP_STRIDE = 16
CMP_HIDDEN = 2 * HEAD_DIM
SEL_BLOCK = 64
SEL_TOPK = 16
NSA_WINDOW = 512
FORCE_SCORE = 1e4
NEG_INF = -1e30
MASK_BIG = 2.0 ** 100
N_EXPERTS = 64
EXPERTS_PER_GROUP = 8
EPS = 1e-6
LOG2E = math.log2(math.e)
VMEM_LIMIT = 56 * 1024 * 1024
WIDE_TILES = 4

IN_SPLITS = (
    GROUP_WIDTH, GROUP_WIDTH, GROUP_WIDTH,
    GROUP_WIDTH, SWA_KV_HEADS * HEAD_DIM, SWA_KV_HEADS * HEAD_DIM,
    GROUP_WIDTH, GROUP_WIDTH, GROUP_WIDTH, HEADS,
    GROUP_WIDTH, HEAD_DIM, HEAD_DIM, HEAD_DIM, HEAD_DIM,
    HEAD_DIM, HEAD_DIM, 3 * HEADS,
)

_NT = (((1,), (1,)), ((), ()))


def _alibi_slopes():
    n = 3 * HEADS
    m = 2.0 ** (-8.0 * np.arange(1, n + 1) / n)
    return m.reshape(HEADS, 3).T


SLOPES = _alibi_slopes()


def _dot(a, b):
    return jnp.dot(a, b, preferred_element_type=F32)


def _dot_nt(a, b):
    return lax.dot_general(a, b, _NT, preferred_element_type=F32)


def _split_bf16(a):
    hi = a.astype(BF16)
    lo = (a - hi.astype(F32)).astype(BF16)
    return hi, lo


def _dot_f32(a, b, nt=False):
    d = _dot_nt if nt else _dot
    ah, al = _split_bf16(a)
    bh, bl = _split_bf16(b)
    return d(ah, bh) + (d(ah, bl) + d(al, bh))


def _cparams(sem):
    return pltpu.CompilerParams(dimension_semantics=sem, vmem_limit_bytes=VMEM_LIMIT)


def _mod_kernel(c_ref, w_ref, b_ref, o_ref):
    c = c_ref[...]
    cond = c * jax.nn.sigmoid(c)
    o_ref[0] = _dot_f32(cond, w_ref[0]) + b_ref[0]


def _modulation(c, ada_w, ada_b):
    depth, d, n = ada_w.shape
    b = c.shape[0]
    rows = 8
    cp = jnp.zeros((rows, d), F32).at[:b].set(c)
    tn = 1536
    out = pl.pallas_call(
        _mod_kernel,
        grid=(depth, n // tn),
        in_specs=[
            pl.BlockSpec((rows, d), lambda l, j: (0, 0)),
            pl.BlockSpec((1, d, tn), lambda l, j: (l, 0, j)),
            pl.BlockSpec((1, 1, tn), lambda l, j: (l, 0, j)),
        ],
        out_specs=pl.BlockSpec((1, rows, tn), lambda l, j: (l, 0, j)),
        out_shape=jax.ShapeDtypeStruct((depth, rows, n), F32),
        compiler_params=_cparams(("arbitrary", "arbitrary")),
        name="adaln_mod",
    )(cp, ada_w, ada_b.reshape(depth, 1, n))
    return out[:, :b].reshape(depth, b, 6, d)


W_A, W_B, W_C, W_D, W_G, W_CMP = 768, 768, 768, 768, 128, 128
W_OFFS = np.cumsum((0, W_A, W_B, W_C, W_D, W_G, W_CMP))
W_TOTAL = int(W_OFFS[-1])


def _prep_w_in(w):
    offs = np.cumsum((0,) + IN_SPLITS)
    col = lambda i: w[:, int(offs[i]):int(offs[i + 1])]
    dup = lambda t: jnp.concatenate([t, t], axis=1)
    kb, vb = col(4), col(5)
    gates = jnp.zeros((w.shape[0], W_G), F32).at[:, :3 * HEADS].set(col(17))
    qs64 = LOG2E * HEAD_DIM ** -0.5
    parts = [
        col(0) * (LOG2E * DIFF_DK ** -0.5), col(1), col(2),
        col(3) * qs64,
        dup(kb[:, :HEAD_DIM]), dup(kb[:, HEAD_DIM:]),
        dup(vb[:, :HEAD_DIM]), dup(vb[:, HEAD_DIM:]),
        col(6) * qs64, col(7), col(8),
        col(10) * qs64, dup(col(13)), dup(col(14)), dup(col(15)), dup(col(16)),
    ] + [gates, col(11), col(12)]
    wp = jnp.concatenate(parts, axis=1).astype(BF16)
    wf = jnp.zeros((8, w.shape[0]), F32).at[:HEADS].set(col(9).T).astype(BF16)
    return wp, wf


def _lane_cumsum(y):
    n = y.shape[1]
    lane = lax.broadcasted_iota(jnp.int32, y.shape, 1)
    sh = 1
    while sh < n:
        y = y + jnp.where(lane >= sh, pltpu.roll(y, sh, 1), 0.0)
        sh *= 2
    return y


def _inproj_kernel(x_ref, mod_ref, g_ref, w_ref, wf_ref, bf_ref,
                   oa, ob, oc, od, og, ocmp, ofc, carry):
    i = pl.program_id(1)
    x = x_ref[0]
    ms = jnp.mean(x * x, axis=-1, keepdims=True)
    y = x * lax.rsqrt(ms + EPS) * g_ref[...]
    h = y * (1.0 + mod_ref[1:2, :]) + mod_ref[0:1, :]
    hb = h.astype(BF16)
    o = W_OFFS
    oa[0] = _dot(hb, w_ref[:, o[0]:o[1]]).astype(BF16)
    ob[0] = _dot(hb, w_ref[:, o[1]:o[2]]).astype(BF16)
    oc[0] = _dot(hb, w_ref[:, o[2]:o[3]]).astype(BF16)
    od[0] = _dot(hb, w_ref[:, o[3]:o[4]]).astype(BF16)
    og[0] = jax.nn.sigmoid(_dot(hb, w_ref[:, o[4]:o[5]])).astype(BF16)
    kvc = _dot(hb, w_ref[:, o[5]:o[6]])
    ocmp[0, 0] = kvc[:, :HEAD_DIM]
    ocmp[1, 0] = kvc[:, HEAD_DIM:]
    fl = _dot_nt(wf_ref[...], hb) + bf_ref[:, 0:1]
    ls = jnp.minimum(fl, 0.0) - jnp.log1p(jnp.exp(-jnp.abs(fl)))

    @pl.when(i == 0)
    def _():
        carry[...] = jnp.zeros_like(carry)

    cs = _lane_cumsum(ls) + carry[:, 0:1]
    ofc[0] = cs
    carry[...] = jnp.broadcast_to(cs[:, cs.shape[1] - 1:], carry.shape)


def _in_projection(x, mod_l, g, wp, wf, bf, tm):
    b, s, d = x.shape
    bf8 = jnp.zeros((8, LANES), F32).at[:HEADS, :].set(bf[:, None])
    outs = pl.pallas_call(
        _inproj_kernel,
        grid=(b, s // tm),
        in_specs=[
            pl.BlockSpec((1, tm, d), lambda bi, i: (bi, i, 0)),
            pl.BlockSpec((None, 6, d), lambda bi, i: (bi, 0, 0)),
            pl.BlockSpec((1, d), lambda bi, i: (0, 0)),
            pl.BlockSpec((d, W_TOTAL), lambda bi, i: (0, 0)),
            pl.BlockSpec((8, d), lambda bi, i: (0, 0)),
            pl.BlockSpec((8, LANES), lambda bi, i: (0, 0)),
        ],
        out_specs=[
            pl.BlockSpec((1, tm, W_A), lambda bi, i: (bi, i, 0)),
            pl.BlockSpec((1, tm, W_B), lambda bi, i: (bi, i, 0)),
            pl.BlockSpec((1, tm, W_C), lambda bi, i: (bi, i, 0)),
            pl.BlockSpec((1, tm, W_D), lambda bi, i: (bi, i, 0)),
            pl.BlockSpec((1, tm, W_G), lambda bi, i: (bi, i, 0)),
            pl.BlockSpec((2, 1, tm, HEAD_DIM), lambda bi, i: (0, bi, i, 0)),
            pl.BlockSpec((1, 8, tm), lambda bi, i: (bi, 0, i)),
        ],
        out_shape=[
            jax.ShapeDtypeStruct((b, s, W_A), BF16),
            jax.ShapeDtypeStruct((b, s, W_B), BF16),
            jax.ShapeDtypeStruct((b, s, W_C), BF16),
            jax.ShapeDtypeStruct((b, s, W_D), BF16),
            jax.ShapeDtypeStruct((b, s, W_G), BF16),
            jax.ShapeDtypeStruct((2, b, s, HEAD_DIM), F32),
            jax.ShapeDtypeStruct((b, 8, s), F32),
        ],
        scratch_shapes=[pltpu.VMEM((8, LANES), F32)],
        compiler_params=_cparams(("arbitrary", "arbitrary")),
        name="in_projection",
    )(x, mod_l, g.reshape(1, d), wp, wf, bf8)
    return outs


def _flash_kernel(kind, t, layer_idx, *refs):
    it = iter(refs)
    q_ref, k_ref, v_ref = next(it), next(it), next(it)
    fc_ref = next(it) if kind == "C" else None
    un_ref, use_ref, head_ref, en_ref = ((next(it), next(it), next(it), next(it)) if kind == "Dsel"
                                         else (None, None, None, None))
    lam_ref, sg_ref, bc_ref = (next(it), next(it), next(it)) if kind == "A" else (None, None, None)
    o_ref, m_scr, acc_scr = next(it), next(it), next(it)

    p = pl.program_id(1)
    i = pl.program_id(2)
    nstream = 4 if kind == "A" else 2
    width = LANES // nstream
    mixer = {"Dsel": 2}.get(kind)

    def head_of(s):
        return s // 2 if kind == "A" else s

    q2 = q_ref[0]
    lane = lax.broadcasted_iota(jnp.int32, (t, LANES), 1)
    zero = jnp.zeros_like(q2)
    qms = [jnp.where((lane >= s * width) & (lane < (s + 1) * width), q2, zero)
           for s in range(nstream)]
    if kind == "Dsel":
        qms = [jnp.concatenate([qm, un_ref[0]], axis=1) for qm in qms]
    if kind == "A":
        pick = [jnp.where((lane >= BIAS_TERMS * h) & (lane < BIAS_TERMS * (h + 1)), 1.0, 0.0).astype(q2.dtype)
                for h in range(2)]
        qms = [jnp.concatenate([qm, pick[head_of(s)]], axis=1) for s, qm in enumerate(qms)]
    qcat = jnp.concatenate(qms, axis=0)

    slopes = None
    if mixer is not None:
        sl = SLOPES[mixer] * LOG2E
        slopes = [jnp.where(p == 0, float(sl[head_of(s)]), float(sl[2 + head_of(s)]))
                  for s in range(nstream)]

    for s in range(nstream):
        m_scr[s] = jnp.full((t, LANES), NEG_INF, F32)
        acc_scr[s] = jnp.zeros((t, LANES), F32)

    def tile(first_key, nk, diag):
        start = pl.multiple_of(first_key, t)
        k2 = k_ref[0, pl.ds(start, nk), :]
        if kind == "Dsel":
            k2 = jnp.concatenate([k2, en_ref[pl.ds(start, nk), :]], axis=1)
        if kind == "A":
            k2 = jnp.concatenate([k2, bc_ref[0, pl.ds(start, nk), :]], axis=1)
        v2 = v_ref[0, pl.ds(start, nk), :]
        vlane = lax.broadcasted_iota(jnp.int32, (nk, LANES), 1)
        one = jnp.ones_like(v2)
        vhalf = [jnp.where(vlane < HEAD_DIM, v2, one), jnp.where(vlane < HEAD_DIM, one, v2)]
        col = lax.broadcasted_iota(jnp.int32, (1, nk), 1)
        rel = (start - i * t + col).astype(F32)
        if diag:
            mask = (lax.broadcasted_iota(jnp.int32, (t, t), 0)
                    >= lax.broadcasted_iota(jnp.int32, (t, t), 1))
        sc_all = _dot_nt(qcat, k2)
        for s in range(nstream):
            sc = sc_all[s * t:(s + 1) * t]
            if slopes is not None:
                sc = sc + slopes[s] * rel
            if kind == "C":
                sc = sc - LOG2E * fc_ref[0, pl.ds(2 * p + s, 1), pl.ds(start, nk)]
            if diag:
                sc = jnp.where(mask, sc, NEG_INF)
            m_prev = m_scr[s]
            m_next = jnp.maximum(m_prev, jnp.max(sc, axis=1, keepdims=True))
            alpha = jnp.exp2(m_prev - m_next)
            pexp = jnp.exp2((sc - jnp.tile(m_next, (1, nk // LANES))).astype(BF16))
            acc_scr[s] = alpha * acc_scr[s] + _dot(pexp, vhalf[head_of(s)])
            m_scr[s] = m_next

    tile(i * t, t, True)

    if kind == "Dsel":
        nq = pl.num_programs(2)
        base = (pl.program_id(0) * nq + i) * nq

        def body(j, carry):
            @pl.when(use_ref[base + j] != 0)
            def _():
                tile(j * t, t, False)
            return carry

        short = (i > 0) & (head_ref[pl.program_id(0) * nq + i] != 0)

        @pl.when(short)
        def _():
            tile(0, LANES, False)

        lax.fori_loop(jnp.where(short, 1, 0), i, body, 0)
    else:
        wide = WIDE_TILES * t

        def body(j, carry):
            tile(j * wide, wide, False)
            return carry

        lax.fori_loop(0, i // WIDE_TILES, body, 0)

        done = i // WIDE_TILES * WIDE_TILES
        width = WIDE_TILES // 2
        while width >= 1:
            take = ((i - done) // width) * width

            @pl.when(take > 0)
            def _(done=done, width=width):
                tile(done * t, width * t, False)

            done = done + take
            width //= 2

    outs = [acc_scr[s] / pltpu.roll(acc_scr[s], HEAD_DIM, 1) for s in range(nstream)]
    if kind == "A":
        lam_init = 0.8 - 0.6 * math.exp(-0.3 * layer_idx)
        t1 = jnp.sum(lam_ref[0:1, :] * lam_ref[1:2, :], axis=1, keepdims=True)
        t2 = jnp.sum(lam_ref[2:3, :] * lam_ref[3:4, :], axis=1, keepdims=True)
        lam = jnp.exp(t1) - jnp.exp(t2) + lam_init
        d0 = outs[0] - lam * outs[1]
        d1 = outs[2] - lam * outs[3]
        lo_half = lane < HEAD_DIM
        o = jnp.where(lo_half, d0, d1)
        sq = o * o
        ss_lo = jnp.sum(jnp.where(lo_half, sq, 0.0), axis=1, keepdims=True)
        ss_hi = jnp.sum(jnp.where(lo_half, 0.0, sq), axis=1, keepdims=True)
        ms = jnp.where(lo_half, ss_lo, ss_hi) * (1.0 / HEAD_DIM)
        o = o * lax.rsqrt(ms + EPS) * sg_ref[...] * (1.0 - lam_init)
    else:
        o = jnp.where(lane < HEAD_DIM, outs[0], outs[1])
    o_ref[0] = o.astype(o_ref.dtype)


BIAS_TERMS = 3


def _alibi_key_columns(s):
    cols = np.zeros((2, s, LANES), np.float32)
    pos = np.arange(s, dtype=np.float64)
    for pair in range(2):
        for h in range(2):
            rest = SLOPES[0][2 * pair + h] * LOG2E * pos
            for n in range(BIAS_TERMS):
                term = rest.astype(jnp.bfloat16).astype(np.float64)
                cols[pair, :, BIAS_TERMS * h + n] = term
                rest = rest - term
    return jnp.asarray(cols, BF16)


def _flash(kind, t, layer_idx, q_arr, qoff, k_arr, koff, v_arr, voff, extras):
    b, s, _ = q_arr.shape
    nstream = 4 if kind == "A" else 2
    shared_kv = kind == "Dsel"
    kv_idx = (lambda off: (lambda bi, p, i: (bi, 0, off))) if shared_kv else \
             (lambda off: (lambda bi, p, i: (bi, 0, off + p)))
    in_specs = [
        pl.BlockSpec((1, t, LANES), lambda bi, p, i: (bi, i, qoff + p)),
        pl.BlockSpec((1, s, LANES), kv_idx(koff)),
        pl.BlockSpec((1, s, LANES), kv_idx(voff)),
    ]
    args = [q_arr, k_arr, v_arr]
    if kind == "C":
        fcum, = extras
        in_specs.append(pl.BlockSpec((1, 8, s), lambda bi, p, i: (bi, 0, 0)))
        args.append(fcum)
    elif kind == "Dsel":
        unsel, eneg = extras
        nsel = unsel.shape[-1]
        nq, per = s // t, t // SEL_BLOCK
        picked = unsel.reshape(b, nq, t, nq, per) == 0
        used = picked.any(axis=(2, 4)).astype(jnp.int32).reshape(-1)
        head_only = jnp.logical_not(picked[:, :, :, 0, LANES // SEL_BLOCK:].any(axis=(2, 3)))
        head_only = head_only.astype(jnp.int32).reshape(-1)
        in_specs.append(pl.BlockSpec((1, t, nsel), lambda bi, p, i: (bi, i, 0)))
        in_specs.append(pl.BlockSpec(memory_space=pltpu.SMEM))
        in_specs.append(pl.BlockSpec(memory_space=pltpu.SMEM))
        in_specs.append(pl.BlockSpec((s, nsel), lambda bi, p, i: (0, 0)))
        args += [unsel, used, head_only, eneg]
    elif kind == "A":
        lam8, sg = extras
        in_specs.append(pl.BlockSpec((8, LANES), lambda bi, p, i: (0, 0)))
        in_specs.append(pl.BlockSpec((1, LANES), lambda bi, p, i: (0, 0)))
        in_specs.append(pl.BlockSpec((1, s, LANES), lambda bi, p, i: (p, 0, 0)))
        args += [lam8, sg, _alibi_key_columns(s)]
    return pl.pallas_call(
        functools.partial(_flash_kernel, kind, t, layer_idx),
        grid=(b, 2, s // t),
        in_specs=in_specs,
        out_specs=pl.BlockSpec((1, t, LANES), lambda bi, p, i: (bi, i, p)),
        out_shape=jax.ShapeDtypeStruct((b, s, 2 * LANES), BF16),
        scratch_shapes=[pltpu.VMEM((nstream, t, LANES), F32)] * 2,
        compiler_params=_cparams(("arbitrary", "arbitrary", "arbitrary")),
        name="flash_" + kind,
    )(*args)


def _window_kernel(kind, t, wpad, *refs):
    it = iter(refs)
    q_ref, k_ref, v_ref = next(it), next(it), next(it)
    sink_ref = next(it) if kind == "B" else None
    o_ref = next(it)
    i = pl.program_id(1)
    window = SWA_WINDOW if kind == "B" else NSA_WINDOW
    sl = SLOPES[1 if kind == "B" else 2] * LOG2E
    nk = wpad + t
    start = pl.multiple_of(jnp.maximum(i * t - wpad, 0), LANES)
    lane = lax.broadcasted_iota(jnp.int32, (t, LANES), 1)
    vlane = lax.broadcasted_iota(jnp.int32, (nk, LANES), 1)
    dist = ((i * t - start) + lax.broadcasted_iota(jnp.int32, (t, nk), 0)
            - lax.broadcasted_iota(jnp.int32, (t, nk), 1))
    valid = (dist >= 0) & (dist < window)
    rel = (start - i * t + lax.broadcasted_iota(jnp.int32, (1, nk), 1)).astype(F32)
    qrow = lax.broadcasted_iota(jnp.int32, (t, 1), 0).astype(F32)

    def masked_q(pair, g):
        q2 = q_ref[0, :, pair * LANES:(pair + 1) * LANES]
        return jnp.where((lane >= g * HEAD_DIM) & (lane < (g + 1) * HEAD_DIM), q2, jnp.zeros_like(q2))

    def kv_tiles(pair):
        cols = slice(pair * LANES, (pair + 1) * LANES) if kind == "B" else slice(0, LANES)
        k2 = k_ref[0, pl.ds(start, nk), cols]
        v2 = v_ref[0, pl.ds(start, nk), cols]
        one = jnp.ones_like(v2)
        return k2, [jnp.where(vlane < HEAD_DIM, v2, one), jnp.where(vlane < HEAD_DIM, one, v2)]

    groups = [[(0, 0), (0, 1)], [(1, 0), (1, 1)]] if kind == "B" else [[(0, 0), (0, 1), (1, 0), (1, 1)]]
    outs = {}
    for group in groups:
        k2, vhalf = kv_tiles(group[0][0])
        sc_all = _dot_nt(jnp.concatenate([masked_q(pair, g) for pair, g in group], axis=0), k2)
        for n, (pair, g) in enumerate(group):
            h = 2 * pair + g
            slope = float(sl[h])
            sc = jnp.where(valid, sc_all[n * t:(n + 1) * t] + slope * rel, NEG_INF)
            m = jnp.max(sc, axis=1, keepdims=True)
            if kind == "B":
                sink = LOG2E * sink_ref[h] + slope * qrow
                m = jnp.maximum(m, sink)
            pv = _dot(jnp.exp2((sc - m).astype(BF16)), vhalf[g])
            l = pltpu.roll(pv, HEAD_DIM, 1)
            if kind == "B":
                l = l + jnp.exp2(sink - m)
            outs[h] = pv / l
    for pair in range(2):
        o_ref[0, :, pair * LANES:(pair + 1) * LANES] = jnp.where(
            lane < HEAD_DIM, outs[2 * pair], outs[2 * pair + 1]).astype(o_ref.dtype)


def _window(kind, t, wpad, q_arr, k_arr, koff, v_arr, voff, extras):
    b, s, _ = q_arr.shape
    kw = 2 * LANES if kind == "B" else LANES
    in_specs = [
        pl.BlockSpec((1, t, 2 * LANES), lambda bi, i: (bi, i, 0)),
        pl.BlockSpec((1, s, kw), lambda bi, i: (bi, 0, koff)),
        pl.BlockSpec((1, s, kw), lambda bi, i: (bi, 0, voff)),
    ]
    args = [q_arr, k_arr, v_arr]
    if kind == "B":
        in_specs.append(pl.BlockSpec(memory_space=pltpu.SMEM))
        args.append(extras[0])
    return pl.pallas_call(
        functools.partial(_window_kernel, kind, t, wpad),
        grid=(b, s // t),
        in_specs=in_specs,
        out_specs=pl.BlockSpec((1, t, 2 * LANES), lambda bi, i: (bi, i, 0)),
        out_shape=jax.ShapeDtypeStruct((b, s, 2 * LANES), BF16),
        compiler_params=_cparams(("arbitrary", "arbitrary")),
        name="window_" + kind,
    )(*args)


def _gelu_tanh(x):
    return 0.5 * x * (1.0 + jnp.tanh(math.sqrt(2.0 / math.pi) * (x + 0.044715 * x * x * x)))


def _compress_kernel(t_ref, w1_ref, pos_ref, b1_ref, w2_ref, o_ref):
    tr = t_ref[0, 0]
    half = tr.shape[1]
    n = tr.shape[0]
    w1 = w1_ref[0]
    u = _dot_f32(tr, w1[:half])
    v = _dot_f32(tr, w1[half:])
    cpos = _dot_f32(pos_ref[0], w1)[0:1] + b1_ref[0]
    hid = u + pltpu.roll(v, n - 1, 0) + cpos
    o_ref[0, 0] = _dot_f32(_gelu_tanh(hid), w2_ref[0])


def _compress(kv_r, w1, pos8, b1, w2d):
    two, b, n, dd = kv_r.shape
    return pl.pallas_call(
        _compress_kernel,
        grid=(two, b),
        in_specs=[
            pl.BlockSpec((1, 1, n, dd), lambda c, bi: (c, bi, 0, 0)),
            pl.BlockSpec((1, 2 * dd, CMP_HIDDEN), lambda c, bi: (c, 0, 0)),
            pl.BlockSpec((1, 8, 2 * dd), lambda c, bi: (c, 0, 0)),
            pl.BlockSpec((1, 1, CMP_HIDDEN), lambda c, bi: (c, 0, 0)),
            pl.BlockSpec((1, CMP_HIDDEN, LANES), lambda c, bi: (c, 0, 0)),
        ],
        out_specs=pl.BlockSpec((1, 1, n, LANES), lambda c, bi: (c, bi, 0, 0)),
        out_shape=jax.ShapeDtypeStruct((two, b, n, LANES), F32),
        compiler_params=_cparams(("arbitrary", "arbitrary")),
        name="nsa_compress",
    )(kv_r, w1, pos8, b1, w2d)


def _cmp_kernel(t, first_tile, ncp_total, q_ref, kc_ref, vc_ref, cov_ref, o_ref, un_ref):
    i = first_tile + pl.program_id(1)
    ncp = kc_ref.shape[2]
    nsel = cov_ref.shape[0]
    kc = kc_ref[0, 0]
    vc = vc_ref[0, 0]
    kch, kcl = _split_bf16(kc)
    vcb = vc.astype(BF16)
    lane = lax.broadcasted_iota(jnp.int32, (t, LANES), 1)
    tq = i * t + lax.broadcasted_iota(jnp.int32, (t, ncp), 0)
    nidx = lax.broadcasted_iota(jnp.int32, (t, ncp), 1)
    cmp_end = nidx * CMP_STRIDE + (CMP_BLOCK - 1)
    valid = (tq >= cmp_end) & (nidx < ncp_total - 1)
    end_row = cmp_end[0:1, :].astype(F32)
    psum = jnp.zeros((t, ncp), F32)
    outs = []
    for pair in range(2):
        q2 = q_ref[0, :, pair * LANES:(pair + 1) * LANES]
        for g in range(2):
            h = 2 * pair + g
            qm = jnp.where((lane >= g * HEAD_DIM) & (lane < (g + 1) * HEAD_DIM), q2, jnp.zeros_like(q2))
            sc = _dot_nt(qm, kch) + _dot_nt(qm, kcl) + float(SLOPES[2][h] * LOG2E) * end_row
            sc = jnp.where(valid, sc, NEG_INF)
            m = jnp.max(sc, axis=1, keepdims=True)
            pe = jnp.exp2(sc - m)
            l = jnp.sum(pe, axis=1, keepdims=True)
            pc = pe * jnp.where(m > 0.5 * NEG_INF, 1.0 / l, 0.0)
            psum = psum + pc
            outs.append(_dot(pc.astype(BF16), vcb))
    lo_half = lane < HEAD_DIM
    o_ref[0, :, 0:LANES] = jnp.where(lo_half, outs[0], outs[1]).astype(o_ref.dtype)
    o_ref[0, :, LANES:2 * LANES] = jnp.where(lo_half, outs[2], outs[3]).astype(o_ref.dtype)
    ph, plo = _split_bf16(psum)
    cov = cov_ref[...]
    imp = _dot_nt(cov, ph) + _dot_nt(cov, plo)
    blk = lax.broadcasted_iota(jnp.int32, (nsel, t), 0)
    cur = (i * t + lax.broadcasted_iota(jnp.int32, (nsel, t), 1)) // SEL_BLOCK
    forced = (blk == 0) | (blk == cur) | (blk == cur - 1)
    score = jnp.where(forced, FORCE_SCORE, jnp.where(blk <= cur, imp, -1.0))
    unsel = jnp.ones((nsel, t), F32)
    for _ in range(min(SEL_TOPK, nsel)):
        mx = jnp.max(score, axis=0, keepdims=True)
        idx = jnp.min(jnp.where(score == mx, blk, nsel), axis=0, keepdims=True)
        hit = blk == idx
        unsel = jnp.where(hit, 0.0, unsel)
        score = jnp.where(hit, -3.0, score)
    un_ref[0] = unsel.T.astype(un_ref.dtype)


CMP_RANGES = 4


def _cmp_attention(t, qd, kvc, cover_t):
    b, s, _ = qd.shape
    ncp_total = kvc.shape[2]
    nsel = cover_t.shape[0]
    steps = s // t // CMP_RANGES
    outs = []
    for r in range(CMP_RANGES):
        ncp = (r + 1) * ncp_total // CMP_RANGES
        first = r * steps
        outs.append(pl.pallas_call(
            functools.partial(_cmp_kernel, t, first, ncp_total),
            grid=(b, steps),
            in_specs=[
                pl.BlockSpec((1, t, 2 * LANES), lambda bi, i, first=first: (bi, first + i, 0)),
                pl.BlockSpec((1, 1, ncp, LANES), lambda bi, i: (0, bi, 0, 0)),
                pl.BlockSpec((1, 1, ncp, LANES), lambda bi, i: (1, bi, 0, 0)),
                pl.BlockSpec((nsel, ncp), lambda bi, i: (0, 0)),
            ],
            out_specs=[
                pl.BlockSpec((1, t, 2 * LANES), lambda bi, i: (bi, i, 0)),
                pl.BlockSpec((1, t, nsel), lambda bi, i: (bi, i, 0)),
            ],
            out_shape=[
                jax.ShapeDtypeStruct((b, steps * t, 2 * LANES), BF16),
                jax.ShapeDtypeStruct((b, steps * t, nsel), BF16),
            ],
            compiler_params=_cparams(("arbitrary", "arbitrary")),
            name="nsa_cmp_select",
        )(qd, kvc, kvc, cover_t))
    return (jnp.concatenate([o for o, _ in outs], axis=1),
            jnp.concatenate([u for _, u in outs], axis=1))


def _cover_matrix(s):
    ncp, nsel = s // CMP_STRIDE, s // SEL_BLOCK
    cs = np.arange(ncp)[:, None] * CMP_STRIDE
    ss = np.arange(nsel)[None, :] * SEL_BLOCK
    cov = np.clip(np.minimum(cs + CMP_BLOCK, ss + SEL_BLOCK) - np.maximum(cs, ss), 0, None) / CMP_BLOCK
    cov[ncp - 1] = 0.0
    return jnp.asarray(cov.T, BF16)


def _expand_neg(s):
    nsel = s // SEL_BLOCK
    e = (np.arange(s)[:, None] // SEL_BLOCK) == np.arange(nsel)[None, :]
    return jnp.asarray(np.where(e, -MASK_BIG, 0.0), BF16)


def _outproj_kernel(ya, yb, yc, ocmp, osel, owin, gt, gx_ref, x_ref, mod_ref, wo_ref, g_ref,
                    rwt_ref, rb_ref, xo_ref, h_ref, ri_ref, rg_ref, cnt_ref, carry):
    first = (pl.program_id(0) == 0) & (pl.program_id(1) == 0)
    gw = GROUP_WIDTH
    g = _dot(gt[0], gx_ref[...])
    yd = (g[:, 0:gw] * ocmp[0].astype(F32) + g[:, gw:2 * gw] * osel[0].astype(F32)
          + g[:, 2 * gw:3 * gw] * owin[0].astype(F32))
    ycat = jnp.concatenate([ya[0], yb[0], yc[0], yd.astype(BF16)], axis=1)
    y = _dot(ycat, wo_ref[...])
    x = x_ref[0] + mod_ref[2:3, :] * y
    xo_ref[0] = x
    ms = jnp.mean(x * x, axis=-1, keepdims=True)
    h = x * lax.rsqrt(ms + EPS) * g_ref[...]
    h = h * (1.0 + mod_ref[4:5, :]) + mod_ref[3:4, :]
    h_ref[0] = h
    tm = h.shape[0]
    ng = LANES // EXPERTS_PER_GROUP
    big = 4 * LANES
    logits = _dot_f32(rwt_ref[...], h, nt=True)
    eidx = lax.broadcasted_iota(jnp.int32, (LANES, tm), 0)
    aff = jax.nn.sigmoid(logits)
    sel = jnp.where(eidx < N_EXPERTS, aff + rb_ref[:, 0:1], NEG_INF)
    sel3 = sel.reshape(ng, EXPERTS_PER_GROUP, tm)
    e3 = eidx.reshape(ng, EXPERTS_PER_GROUP, tm)
    g1 = jnp.max(sel3, axis=1, keepdims=True)
    i1 = jnp.min(jnp.where(sel3 == g1, e3, big), axis=1, keepdims=True)
    sel_b = jnp.where(e3 == i1, NEG_INF, sel3)
    g2 = jnp.max(sel_b, axis=1, keepdims=True)
    i2 = jnp.min(jnp.where(sel_b == g2, e3, big), axis=1, keepdims=True)
    gs = g1 + g2
    gidx = lax.broadcasted_iota(jnp.int32, (ng, 1, tm), 0)
    gm = jnp.max(gs, axis=0, keepdims=True)
    best = gidx == jnp.min(jnp.where(gs == gm, gidx, big), axis=0, keepdims=True)
    e1 = jnp.min(jnp.where(best, i1, big), axis=0)
    e2 = jnp.min(jnp.where(best, i2, big), axis=0)
    oh1 = eidx == e1
    oh2 = eidx == e2
    a1 = jnp.sum(jnp.where(oh1, aff, 0.0), axis=0, keepdims=True)
    a2 = jnp.sum(jnp.where(oh2, aff, 0.0), axis=0, keepdims=True)
    inv = 1.0 / (a1 + a2)
    @pl.when(first)
    def _():
        carry[...] = jnp.zeros_like(carry)

    ohs = jnp.where(oh1 | oh2, 1.0, 0.0)
    rr = lax.broadcasted_iota(jnp.int32, (tm, tm), 0)
    cc = lax.broadcasted_iota(jnp.int32, (tm, tm), 1)
    earlier = jnp.where(rr < cc, 1.0, 0.0).astype(BF16)
    before = _dot(ohs.astype(BF16), earlier) + carry[:, 0:1]
    r1 = jnp.sum(jnp.where(oh1, before, 0.0), axis=0, keepdims=True)
    r2 = jnp.sum(jnp.where(oh2, before, 0.0), axis=0, keepdims=True)
    total = carry[:, 0:1] + jnp.sum(ohs, axis=1, keepdims=True)
    carry[...] = jnp.broadcast_to(total, carry.shape)
    cnt_ref[...] = jnp.broadcast_to(total, cnt_ref.shape).astype(jnp.int32)
    row = lax.broadcasted_iota(jnp.int32, (8, tm), 0)
    ri_ref[0] = jnp.where(row == 0, e1, jnp.where(row == 1, e2, jnp.where(
        row == 2, r1.astype(jnp.int32), jnp.where(row == 3, r2.astype(jnp.int32), 0))))
    rg_ref[0] = jnp.where(row == 0, a1 * inv, jnp.where(row == 1, a2 * inv, 0.0))


def _gate_expander():
    e = np.zeros((W_G, 3 * GROUP_WIDTH), np.float32)
    for h in range(HEADS):
        for r in range(3):
            e[h * 3 + r, r * GROUP_WIDTH + h * HEAD_DIM:r * GROUP_WIDTH + (h + 1) * HEAD_DIM] = 1.0
    return jnp.asarray(e, BF16)


def _out_projection(ya, yb, yc, ocmp, osel, owin, gates, x, mod_l, wo, g, rwt, rb, tm):
    b, s, d = x.shape
    gw = GROUP_WIDTH
    tok = lambda w: pl.BlockSpec((1, tm, w), lambda bi, i: (bi, i, 0))
    full = lambda shp: pl.BlockSpec(shp, lambda bi, i: (0,) * len(shp))
    rowblk = pl.BlockSpec((1, 8, tm), lambda bi, i: (bi, 0, i))
    return pl.pallas_call(
        _outproj_kernel,
        grid=(b, s // tm),
        in_specs=[tok(gw)] * 6 + [tok(W_G), full((W_G, 3 * gw)), tok(d),
                                  pl.BlockSpec((None, 6, d), lambda bi, i: (bi, 0, 0)),
                                  full((d, d)), full((1, d)), full((LANES, d)), full((LANES, 1))],
        out_specs=[tok(d), tok(d), rowblk, rowblk, full((LANES, LANES))],
        out_shape=[
            jax.ShapeDtypeStruct((b, s, d), F32),
            jax.ShapeDtypeStruct((b, s, d), F32),
            jax.ShapeDtypeStruct((b, 8, s), jnp.int32),
            jax.ShapeDtypeStruct((b, 8, s), F32),
            jax.ShapeDtypeStruct((LANES, LANES), jnp.int32),
        ],
        scratch_shapes=[pltpu.VMEM((LANES, LANES), F32)],
        compiler_params=_cparams(("arbitrary", "arbitrary")),
        name="out_projection_router",
    )(ya, yb, yc, ocmp, osel, owin, gates, _gate_expander(), x, mod_l, wo, g.reshape(1, d), rwt, rb)


def _dispatch_kernel(p0_ref, p1_ref, h_ref, xs_ref, sem):
    tm = h_ref.shape[0]

    def issue(r, c):
        for k, pos_ref in enumerate((p0_ref, p1_ref)):
            pltpu.make_async_copy(h_ref.at[pl.ds(r, 1), :],
                                  xs_ref.at[pl.ds(pos_ref[r], 1), :], sem).start(priority=k)
        return c

    lax.fori_loop(0, tm, issue, 0, unroll=8)
    for _ in range(2):
        pltpu.make_async_copy(h_ref, xs_ref.at[pl.ds(0, tm), :], sem).wait()


def _dispatch(h, pos0, pos1, tm):
    t, d = h.shape
    return pl.pallas_call(
        _dispatch_kernel,
        grid=(t // tm,),
        in_specs=[
            pl.BlockSpec((tm,), lambda i: (i,), memory_space=pltpu.SMEM),
            pl.BlockSpec((tm,), lambda i: (i,), memory_space=pltpu.SMEM),
            pl.BlockSpec((tm, d), lambda i: (i, 0)),
        ],
        out_specs=pl.BlockSpec(memory_space=pl.ANY),
        out_shape=jax.ShapeDtypeStruct((2 * t, d), F32),
        scratch_shapes=[pltpu.SemaphoreType.DMA(())],
        compiler_params=_cparams(("arbitrary",)),
        name="moe_dispatch",
    )(pos0, pos1, h)


def _expert_kernel(layer, vb_ref, ve_ref, vlo_ref, vhi_ref, vord_ref, vnext_ref, nv_ref,
                   xs_ref, wg_hbm, wu_hbm, wd_hbm, ys_ref, wgf, wuf, wdf, wgb, wub, wdb, sem):
    v = pl.program_id(0)
    rows = xs_ref.shape[0]
    prev = jnp.maximum(v - 1, 0)
    new_expert = (v == 0) | (ve_ref[v] != ve_ref[prev])
    new_block = (v == 0) | (vb_ref[v] != vb_ref[prev])

    def weight_copies(e, slot):
        return [pltpu.make_async_copy(src.at[layer, e], dst.at[slot], sem.at[slot, n])
                for n, (src, dst) in enumerate(((wg_hbm, wgf), (wu_hbm, wuf), (wd_hbm, wdf)))]

    @pl.when(v < nv_ref[0])
    def _():
        @pl.when(new_expert)
        def _():
            slot = vord_ref[v] % 2

            @pl.when(v == 0)
            def _():
                for c in weight_copies(ve_ref[v], slot):
                    c.start()

            @pl.when(vnext_ref[v] >= 0)
            def _():
                for c in weight_copies(vnext_ref[v], 1 - slot):
                    c.start()

            for c in weight_copies(ve_ref[v], slot):
                c.wait()
            wgb[...] = wgf[slot].astype(BF16)
            wub[...] = wuf[slot].astype(BF16)
            wdb[...] = wdf[slot].astype(BF16)

        x = xs_ref[...].astype(BF16)
        hg = _dot(x, wgb[...])
        hu = _dot(x, wub[...])
        hm = (hg * jax.nn.sigmoid(hg) * hu).astype(BF16)
        y = _dot(hm, wdb[...])
        r = vb_ref[v] * rows + lax.broadcasted_iota(jnp.int32, (rows, 1), 0)
        mine = (r >= vlo_ref[v]) & (r < vhi_ref[v])

        @pl.when(new_block)
        def _():
            ys_ref[...] = jnp.where(mine, y, 0.0)

        @pl.when(jnp.logical_not(new_block))
        def _():
            ys_ref[...] = jnp.where(mine, y, ys_ref[...])


def _experts(xs, meta, layer, wg, wu, wd, rows, nvmax):
    n, d = xs.shape
    de = wg.shape[3]
    blk = lambda v, vb, ve, lo, hi, vo, vn, nv: (vb[v], 0)
    grid_spec = pltpu.PrefetchScalarGridSpec(
        num_scalar_prefetch=7,
        grid=(nvmax,),
        in_specs=[
            pl.BlockSpec((rows, d), blk),
            pl.BlockSpec(memory_space=pl.ANY),
            pl.BlockSpec(memory_space=pl.ANY),
            pl.BlockSpec(memory_space=pl.ANY),
        ],
        out_specs=pl.BlockSpec((rows, d), blk),
        scratch_shapes=[pltpu.VMEM((2, d, de), F32), pltpu.VMEM((2, d, de), F32), pltpu.VMEM((2, de, d), F32),
                        pltpu.VMEM((d, de), BF16), pltpu.VMEM((d, de), BF16), pltpu.VMEM((de, d), BF16),
                        pltpu.SemaphoreType.DMA((2, 3))],
    )
    return pl.pallas_call(
        functools.partial(_expert_kernel, layer),
        grid_spec=grid_spec,
        out_shape=jax.ShapeDtypeStruct((n, d), F32),
        compiler_params=_cparams(("arbitrary",)),
        name="moe_experts",
    )(*meta, xs, wg, wu, wd)


def _combine_kernel(final, p0_ref, p1_ref, x_ref, rg_ref, mod_ref, g_ref, ys_ref, o_ref, buf, sem):
    tm = x_ref.shape[0]

    def issue(r, c):
        for k, pos_ref in enumerate((p0_ref, p1_ref)):
            pltpu.make_async_copy(ys_ref.at[pl.ds(pos_ref[r], 1), :],
                                  buf.at[k, pl.ds(r, 1), :], sem).start(priority=k)
        return c

    lax.fori_loop(0, tm, issue, 0, unroll=8)
    for k in range(2):
        pltpu.make_async_copy(ys_ref.at[pl.ds(0, tm), :], buf.at[k], sem).wait()
    rg = rg_ref[...]
    y = rg[:, 0:1] * buf[0] + rg[:, 1:2] * buf[1]
    x = x_ref[...] + mod_ref[5:6, :] * y
    if final:
        ms = jnp.mean(x * x, axis=-1, keepdims=True)
        x = x * lax.rsqrt(ms + EPS) * g_ref[...]
    o_ref[...] = x


def _combine(final, pos0, pos1, x, rg, mod_l, g, ys, tm):
    t, d = x.shape
    per_b = t // mod_l.shape[0] // tm
    return pl.pallas_call(
        functools.partial(_combine_kernel, final),
        grid=(t // tm,),
        in_specs=[
            pl.BlockSpec((tm,), lambda i: (i,), memory_space=pltpu.SMEM),
            pl.BlockSpec((tm,), lambda i: (i,), memory_space=pltpu.SMEM),
            pl.BlockSpec((tm, d), lambda i: (i, 0)),
            pl.BlockSpec((tm, 8), lambda i: (i, 0)),
            pl.BlockSpec((None, 6, d), lambda i: (i // per_b, 0, 0)),
            pl.BlockSpec((1, d), lambda i: (0, 0)),
            pl.BlockSpec(memory_space=pl.ANY),
        ],
        out_specs=pl.BlockSpec((tm, d), lambda i: (i, 0)),
        out_shape=jax.ShapeDtypeStruct((t, d), F32),
        scratch_shapes=[pltpu.VMEM((2, tm, d), F32), pltpu.SemaphoreType.DMA(())],
        compiler_params=_cparams(("arbitrary",)),
        name="moe_combine",
    )(pos0, pos1, x, rg, mod_l, g.reshape(1, d), ys)


def _visit_plan(counts, rows, nblocks):
    ne = counts.shape[0]
    nvmax = nblocks + ne - 1
    ends = jnp.cumsum(counts)
    offs = ends - counts
    b_lo = offs // rows
    b_hi = jnp.maximum(ends - 1, 0) // rows
    nvis = jnp.where(counts > 0, b_hi - b_lo + 1, 0)
    vend = jnp.cumsum(nvis)
    vstart = vend - nvis
    nv = vend[-1]
    v = jnp.minimum(jnp.arange(nvmax), nv - 1)
    e = jnp.sum((v[:, None] >= vend[None, :]).astype(jnp.int32), axis=1)
    onehot = e[:, None] == jnp.arange(ne)[None, :]
    pick = lambda a: jnp.sum(jnp.where(onehot, a[None, :], 0), axis=1)
    blk = pick(b_lo) + v - pick(vstart)
    lo = jnp.maximum(pick(offs), blk * rows)
    hi = jnp.minimum(pick(ends), (blk + 1) * rows)
    order = jnp.cumsum((counts > 0).astype(jnp.int32)) - 1
    ids = jnp.where(counts > 0, jnp.arange(ne), ne)
    later = jnp.concatenate([jnp.flip(lax.cummin(jnp.flip(ids)))[1:], jnp.full((1,), ne, ids.dtype)])
    nxt = jnp.where(later < ne, later, -1)
    i32 = lambda a: a.astype(jnp.int32)
    return (i32(blk), i32(e), i32(lo), i32(hi), i32(pick(order)), i32(pick(nxt)),
            i32(nv).reshape(1)), offs, nvmax


def _tiles(s):
    tm = min(512, s)
    return dict(tm=tm, t_full=min(512, s), t_win=min(256, s), t_cmp=min(1024, s),
                tm_dispatch=min(2048, s), tm_combine=min(1024, s), rows=256)


def _layer(l, x, mod_l, p, consts, final):
    b, s, d = x.shape
    tl = _tiles(s)
    wp, wf = _prep_w_in(p["w_in"])
    qa, qb, qc, qd, gates, kvcmp, fcum = _in_projection(
        x, mod_l, p["norm_attn_g"], wp, wf, p["fox_forget_b"], tl["tm"])
    lam8 = jnp.zeros((8, LANES), F32).at[:4, :DIFF_DK].set(
        jnp.stack([p["diff_lam_q1"], p["diff_lam_k1"], p["diff_lam_q2"], p["diff_lam_k2"]]))
    sg = jnp.tile(p["diff_subln_g"], 2).reshape(1, LANES)
    ya = _flash("A", tl["t_full"], l, qa, 0, qa, 2, qa, 4, (lam8, sg))
    yb = _window("B", tl["t_win"], SWA_WINDOW, qb, qb, 1, qb, 2, (p["swa_sinks"],))
    yc = _flash("C", tl["t_full"], l, qc, 0, qc, 2, qc, 4, (fcum,))
    n16 = s // CMP_STRIDE
    kv_r = kvcmp.reshape(2, b, n16, CMP_STRIDE * HEAD_DIM)
    pos8 = jnp.zeros((2, 8, CMP_BLOCK * HEAD_DIM), F32).at[:, 0].set(
        p["nsa_cmp_pos"].reshape(2, CMP_BLOCK * HEAD_DIM))
    w2d = jnp.concatenate([p["nsa_cmp_w2"], p["nsa_cmp_w2"]], axis=-1)
    kvc = _compress(kv_r, p["nsa_cmp_w1"], pos8, p["nsa_cmp_b1"].reshape(2, 1, CMP_HIDDEN), w2d)
    ocmp, unsel = _cmp_attention(tl["t_cmp"], qd, kvc, consts["cover"])
    osel = _flash("Dsel", tl["t_full"], l, qd, 0, qd, 2, qd, 3, (unsel, consts["eneg"]))
    owin = _window("Dwin", tl["t_win"], NSA_WINDOW, qd, qd, 4, qd, 5, ())
    rwt = jnp.zeros((LANES, d), F32).at[:N_EXPERTS].set(p["router_w"].T)
    rb = jnp.zeros((LANES, 1), F32).at[:N_EXPERTS, 0].set(p["router_b"])
    xm, h2, ri, rg, cnt = _out_projection(
        ya, yb, yc, ocmp, osel, owin, gates, x, mod_l, p["w_out"].astype(BF16),
        p["norm_moe_g"], rwt, rb, tl["tm"])
    t = b * s
    counts = cnt[:N_EXPERTS, 0]
    rows = tl["rows"]
    meta, offs, nvmax = _visit_plan(counts, rows, 2 * t // rows)
    onehot = ri[:, 0:2, :, None] == jnp.arange(N_EXPERTS)[None, None, None, :]
    pos = (jnp.sum(jnp.where(onehot, offs[None, None, None, :], 0), axis=-1) + ri[:, 2:4]).astype(jnp.int32)
    pos0, pos1 = pos[:, 0].reshape(t), pos[:, 1].reshape(t)
    rgt = rg.transpose(0, 2, 1).reshape(t, 8)
    xs = _dispatch(h2.reshape(t, d), pos0, pos1, tl["tm_dispatch"])
    ys = _experts(xs, meta, l, p["exp_w_gate"], p["exp_w_up"], p["exp_w_down"], rows, nvmax)
    out = _combine(final, pos0, pos1, xm.reshape(t, d), rgt, mod_l, p["norm_final_g"], ys, tl["tm_combine"])
    return out.reshape(b, s, d)


def kernel(x, c, ada_w, ada_b, norm_attn_g, norm_moe_g, norm_final_g, w_in, w_out, diff_lam_q1, diff_lam_k1, diff_lam_q2, diff_lam_k2, diff_subln_g, swa_sinks, fox_forget_b, nsa_cmp_pos, nsa_cmp_w1, nsa_cmp_b1, nsa_cmp_w2, router_w, router_b, exp_w_gate, exp_w_up, exp_w_down):
    depth = ada_w.shape[0]
    s = x.shape[1]
    mod = _modulation(c, ada_w, ada_b)
    consts = dict(cover=_cover_matrix(s), eneg=_expand_neg(s))
    for l in range(depth):
        p = dict(
            norm_attn_g=norm_attn_g[l], norm_moe_g=norm_moe_g[l], norm_final_g=norm_final_g,
            w_in=w_in[l], w_out=w_out[l],
            diff_lam_q1=diff_lam_q1[l], diff_lam_k1=diff_lam_k1[l],
            diff_lam_q2=diff_lam_q2[l], diff_lam_k2=diff_lam_k2[l], diff_subln_g=diff_subln_g[l],
            swa_sinks=swa_sinks[l], fox_forget_b=fox_forget_b[l],
            nsa_cmp_pos=nsa_cmp_pos[l], nsa_cmp_w1=nsa_cmp_w1[l], nsa_cmp_b1=nsa_cmp_b1[l],
            nsa_cmp_w2=nsa_cmp_w2[l], router_w=router_w, router_b=router_b,
            exp_w_gate=exp_w_gate, exp_w_up=exp_w_up, exp_w_down=exp_w_down,
        )
        x = _layer(l, x, mod[l], p, consts, final=(l == depth - 1))
    return x
```
